```python
import math
import jax
import jax.numpy as jnp
from jax import lax
import numpy as np

D_MODEL = 1024
BATCH = 2
SEQ = 8192
DEPTH = 1
DEC_BATCH = 32
DEC_SEQ = 8
PAST_LEN = 16384
PAGE_SIZE = 128

H_MOBA = 8
HD_MOBA = 64
MOBA_W = H_MOBA * HD_MOBA
MOBA_BLOCK = 256
MOBA_TOPK = 3
Q_BLOCK = 128
H_GLA = 4
DK_GLA = 64
DV_GLA = 128
GLA_KW = H_GLA * DK_GLA
GLA_VW = H_GLA * DV_GLA
GLA_RANK = 16
GLA_TAU = 16.0
GLA_CHUNK = 64
N_MEM = 256
H_CROSS = 4
HD_CROSS = 128
CROSS_W = H_CROSS * HD_CROSS
D_FF = 2816
CONV_W = 3
N_BRANCH = 3
EPS = 1e-6
IN_SIZES = (MOBA_W, MOBA_W, MOBA_W, GLA_KW, GLA_KW, GLA_VW, GLA_VW, GLA_RANK, CROSS_W)
IN_COLS = sum(IN_SIZES)

kernel_name = 'moba_gla_memxattn_convffn_step'


def rmsnorm(x, g):
    xf = x.astype(jnp.float32)
    y = xf * lax.rsqrt(jnp.mean(xf * xf, axis=-1, keepdims=True) + EPS)
    return (y * g.astype(jnp.float32)).astype(x.dtype)


def moba_prompt(q, k, v):
    B, S, H, hd = q.shape
    scale = hd ** -0.5
    nblk = -(-S // MOBA_BLOCK)
    pad = nblk * MOBA_BLOCK - S
    padw = ((0, 0), (0, pad), (0, 0), (0, 0))
    kb = jnp.pad(k, padw).reshape(B, nblk, MOBA_BLOCK, H, hd).transpose(0, 3, 1, 2, 4)
    vb = jnp.pad(v, padw).reshape(B, nblk, MOBA_BLOCK, H, hd).transpose(0, 3, 1, 2, 4)
    kmean = kb.astype(jnp.float32).mean(axis=3)
    topk = min(MOBA_TOPK, nblk - 1)
    nq = S // Q_BLOCK
    qb = q.reshape(B, nq, Q_BLOCK, H, hd).transpose(1, 0, 3, 2, 4)
    bi = jnp.arange(B)[:, None, None, None]
    hi = jnp.arange(H)[None, :, None, None]
    blk_ids = jnp.arange(nblk)

    def one_block(args):
        c, qc = args
        cur = (c * Q_BLOCK) // MOBA_BLOCK
        qpos = c * Q_BLOCK + jnp.arange(Q_BLOCK)
        kpos = cur * MOBA_BLOCK + jnp.arange(MOBA_BLOCK)
        k_own = lax.dynamic_index_in_dim(kb, cur, axis=2, keepdims=False)
        v_own = lax.dynamic_index_in_dim(vb, cur, axis=2, keepdims=False)
        s_own = jnp.einsum('bhqd,bhkd->bhqk', qc, k_own).astype(jnp.float32) * scale
        s_own = jnp.where(kpos[None, :] <= qpos[:, None], s_own, -jnp.inf)
        if topk == 0:
            p = jax.nn.softmax(s_own, axis=-1).astype(v.dtype)
            return jnp.einsum('bhqk,bhkd->bhqd', p, v_own)
        gate = jnp.einsum('bhqd,bhnd->bhqn', qc.astype(jnp.float32), kmean)
        gate = jnp.where(blk_ids < cur, gate, -jnp.inf)
        _, idx = lax.top_k(gate, topk)
        valid = idx < cur
        kg = kb[bi, hi, idx]
        vg = vb[bi, hi, idx]
        s_sel = jnp.einsum('bhqd,bhqnkd->bhqnk', qc, kg).astype(jnp.float32) * scale
        s_sel = jnp.where(valid[..., None], s_sel, -jnp.inf).reshape(B, H, Q_BLOCK, topk * MOBA_BLOCK)
        p = jax.nn.softmax(jnp.concatenate([s_own, s_sel], axis=-1), axis=-1).astype(v.dtype)
        p_own = p[..., :MOBA_BLOCK]
        p_sel = p[..., MOBA_BLOCK:].reshape(B, H, Q_BLOCK, topk, MOBA_BLOCK)
        return (jnp.einsum('bhqk,bhkd->bhqd', p_own, v_own)
                + jnp.einsum('bhqnk,bhqnkd->bhqd', p_sel, vg))

    out = lax.map(one_block, (jnp.arange(nq), qb))
    return out.transpose(1, 0, 3, 2, 4).reshape(B, S, H, hd)


def moba_sample(q, k_new, v_new, cache_k, cache_v, page_table):
    DB, DS, H, hd = q.shape
    scale = hd ** -0.5
    past = page_table.shape[1] * PAGE_SIZE
    k_past = cache_k[page_table].reshape(DB, past, H, hd)
    nfull = past // MOBA_BLOCK
    cur_start = nfull * MOBA_BLOCK
    qpos = past + jnp.arange(DS)
    own_pos = jnp.arange(cur_start, past)
    v_own_past = cache_v[page_table[:, own_pos // PAGE_SIZE], own_pos % PAGE_SIZE]
    k_own = jnp.concatenate([k_past[:, cur_start:], k_new.astype(k_past.dtype)], axis=1)
    v_own = jnp.concatenate([v_own_past, v_new.astype(v_own_past.dtype)], axis=1)
    kpos = jnp.concatenate([own_pos, qpos])
    qh = q.transpose(0, 2, 1, 3)
    s_own = jnp.einsum('bhqd,bkhd->bhqk', qh, k_own).astype(jnp.float32) * scale
    s_own = jnp.where(kpos[None, :] <= qpos[:, None], s_own, -jnp.inf)
    n_own = k_own.shape[1]
    topk = min(MOBA_TOPK, nfull)
    if topk == 0:
        p = jax.nn.softmax(s_own, axis=-1).astype(v_own.dtype)
        o = jnp.einsum('bhqk,bkhd->bhqd', p, v_own)
        return o.transpose(0, 2, 1, 3)
    kmean = k_past[:, :cur_start].reshape(DB, nfull, MOBA_BLOCK, H, hd).astype(jnp.float32).mean(axis=2)
    gate = jnp.einsum('bhqd,bnhd->bhqn', qh.astype(jnp.float32), kmean)
    _, idx = lax.top_k(gate, topk)
    pos = idx[..., None] * MOBA_BLOCK + jnp.arange(MOBA_BLOCK)
    bi = jnp.arange(DB)[:, None, None, None, None]
    hi = jnp.arange(H)[None, :, None, None, None]
    kg = k_past[bi, pos, hi]
    vg = cache_v[page_table[bi, pos // PAGE_SIZE], pos % PAGE_SIZE, hi]
    s_sel = jnp.einsum('bhqd,bhqnkd->bhqnk', qh, kg).astype(jnp.float32) * scale
    s_sel = s_sel.reshape(DB, H, DS, topk * MOBA_BLOCK)
    p = jax.nn.softmax(jnp.concatenate([s_own, s_sel], axis=-1), axis=-1).astype(v_own.dtype)
    p_own = p[..., :n_own]
    p_sel = p[..., n_own:].reshape(DB, H, DS, topk, MOBA_BLOCK)
    o = (jnp.einsum('bhqk,bkhd->bhqd', p_own, v_own)
         + jnp.einsum('bhqnk,bhqnkd->bhqd', p_sel, vg.astype(v_own.dtype)))
    return o.transpose(0, 2, 1, 3)


def gla_chunked(q, k, v, log_a, state0):
    B, L, H, dk = q.shape
    dv = v.shape[-1]
    C = math.gcd(L, GLA_CHUNK)
    n = L // C

    def blocks(t):
        return t.astype(jnp.float32).reshape(B, n, C, H, t.shape[-1]).transpose(1, 0, 3, 2, 4)

    qs = blocks(q) * (dk ** -0.5)
    ks, vs, gs = blocks(k), blocks(v), blocks(log_a)
    causal = jnp.tril(jnp.ones((C, C), dtype=bool))

    def step(S, inp):
        qc, kc, vc, gc = inp
        G = jnp.cumsum(gc, axis=2)
        o_inter = jnp.einsum('bhcd,bhde->bhce', qc * jnp.exp(G), S)
        diff = G[:, :, :, None, :] - G[:, :, None, :, :]
        decay = jnp.exp(jnp.where(causal[:, :, None], diff, -jnp.inf))
        A = jnp.einsum('bhid,bhjd,bhijd->bhij', qc, kc, decay)
        o_intra = jnp.einsum('bhij,bhje->bhie', A, vc)
        G_last = G[:, :, -1]
        S_new = (jnp.exp(G_last)[..., None] * S
                 + jnp.einsum('bhjd,bhje->bhde', kc * jnp.exp(G_last[:, :, None] - G), vc))
        return S_new, o_inter + o_intra

    S_fin, o = lax.scan(step, state0.astype(jnp.float32), (qs, ks, vs, gs))
    return o.transpose(1, 0, 3, 2, 4).reshape(B, L, H, dv), S_fin


def cross_attn(q, mk, mv):
    scale = q.shape[-1] ** -0.5
    s = jnp.einsum('bqhd,bmhd->bhqm', q, mk.astype(q.dtype)).astype(jnp.float32) * scale
    p = jax.nn.softmax(s, axis=-1).astype(q.dtype)
    return jnp.einsum('bhqm,bmhd->bqhd', p, mv.astype(q.dtype))


def memory_kv(mem, norm_mem, w_mem_kv):
    B, M, _ = mem.shape
    mk, mv = jnp.split(rmsnorm(mem, norm_mem) @ w_mem_kv, 2, axis=-1)
    return mk.reshape(B, M, H_CROSS, HD_CROSS), mv.reshape(B, M, H_CROSS, HD_CROSS)


def conv_ffn(hn, prev, w_up, w_conv, b_conv, w_down):
    L = hn.shape[1]
    u, g = jnp.split(hn @ w_up, 2, axis=-1)
    ext = jnp.concatenate([prev.astype(u.dtype), u], axis=1)
    c = b_conv
    for i in range(CONV_W):
        c = c + w_conv[i] * ext[:, i:i + L]
    out = (jax.nn.gelu(c) * g) @ w_down
    return out, ext[:, L:]


def mixer_layer(x, moba_fn, gla_state0, mem_k, mem_v, conv_prev,
                norm_mix, w_in, w_gla_a2, b_gla_a, norm_gla, w_br_moba, w_br_gla, w_br_cross,
                w_gate, b_gate, w_out, norm_ffn, w_up, w_conv, b_conv, w_down):
    B, L, _ = x.shape
    f32 = jnp.float32
    xn = rmsnorm(x, norm_mix)
    points = np.cumsum(IN_SIZES)[:-1].tolist()
    q_m, k_m, v_m, q_g, k_g, v_g, r_g, a_g, q_c = jnp.split(xn @ w_in, points, axis=-1)

    def split_heads(t, h):
        return t.reshape(B, L, h, t.shape[-1] // h)

    k_m = split_heads(k_m, H_MOBA)
    v_m = split_heads(v_m, H_MOBA)
    o_m = moba_fn(split_heads(q_m, H_MOBA), k_m, v_m).reshape(B, L, MOBA_W).astype(x.dtype)
    log_a = jax.nn.log_sigmoid((a_g @ w_gla_a2 + b_gla_a).astype(f32)) / GLA_TAU
    o_g, gla_state = gla_chunked(split_heads(q_g, H_GLA), split_heads(k_g, H_GLA),
                                 split_heads(v_g, H_GLA), split_heads(log_a, H_GLA), gla_state0)
    o_g = o_g * lax.rsqrt(jnp.mean(o_g * o_g, axis=-1, keepdims=True) + EPS)
    o_g = (o_g.reshape(B, L, GLA_VW) * norm_gla.astype(f32) * jax.nn.silu(r_g.astype(f32))).astype(x.dtype)
    o_c = cross_attn(split_heads(q_c, H_CROSS), mem_k, mem_v).reshape(B, L, CROSS_W)
    gates = jax.nn.sigmoid((xn @ w_gate + b_gate).astype(f32)).astype(x.dtype)
    g_m, g_g, g_c = jnp.split(gates, N_BRANCH, axis=-1)
    merged = g_m * (o_m @ w_br_moba) + g_g * (o_g @ w_br_gla) + g_c * (o_c @ w_br_cross)
    h = x + merged @ w_out
    f, conv_state = conv_ffn(rmsnorm(h, norm_ffn), conv_prev, w_up, w_conv, b_conv, w_down)
    return h + f, k_m, v_m, gla_state, conv_state


def setup_inputs(seed: int = 0) -> dict:
    key = jax.random.key(seed)
    ks = iter(jax.random.split(key, 40))
    f32 = jnp.float32

    def nrm(shape, scale):
        return jax.random.normal(next(ks), shape, f32) * scale

    def gain(shape):
        return 1.0 + 0.02 * jax.random.normal(next(ks), shape, f32)

    n_pages = PAST_LEN // PAGE_SIZE
    n_used = DEC_BATCH * n_pages
    n_phys = n_used + max(1, n_used // 4)
    x_prompt = nrm((BATCH, SEQ, D_MODEL), 1.0)
    x_sample = nrm((DEC_BATCH, DEC_SEQ, D_MODEL), 1.0)
    cache_moba_k = nrm((DEPTH, n_phys, PAGE_SIZE, H_MOBA, HD_MOBA), 1.0)
    cache_moba_v = nrm((DEPTH, n_phys, PAGE_SIZE, H_MOBA, HD_MOBA), 1.0)
    state_gla = nrm((DEPTH, DEC_BATCH, H_GLA, DK_GLA, DV_GLA), 0.5)
    state_conv = nrm((DEPTH, DEC_BATCH, CONV_W - 1, D_FF), 1.0)
    cache_mem_k = nrm((DEPTH, DEC_BATCH, N_MEM, H_CROSS, HD_CROSS), 1.0)
    cache_mem_v = nrm((DEPTH, DEC_BATCH, N_MEM, H_CROSS, HD_CROSS), 1.0)
    page_table = jax.random.permutation(next(ks), n_phys)[:n_used].reshape(DEC_BATCH, n_pages).astype(jnp.int32)
    mem_prompt = nrm((BATCH, N_MEM, D_MODEL), 1.0)
    return {
        'x_prompt': x_prompt,
        'x_sample': x_sample,
        'cache_moba_k': cache_moba_k,
        'cache_moba_v': cache_moba_v,
        'state_gla': state_gla,
        'state_conv': state_conv,
        'cache_mem_k': cache_mem_k,
        'cache_mem_v': cache_mem_v,
        'page_table': page_table,
        'mem_prompt': mem_prompt,
        'norm_mix': gain((DEPTH, D_MODEL)),
        'w_in': nrm((DEPTH, D_MODEL, IN_COLS), D_MODEL ** -0.5),
        'w_gla_a2': nrm((DEPTH, GLA_RANK, GLA_KW), GLA_RANK ** -0.5),
        'b_gla_a': nrm((DEPTH, GLA_KW), 0.1),
        'norm_gla': gain((DEPTH, GLA_VW)),
        'norm_mem': gain((DEPTH, D_MODEL)),
        'w_mem_kv': nrm((DEPTH, D_MODEL, 2 * CROSS_W), D_MODEL ** -0.5),
        'w_br_moba': nrm((DEPTH, MOBA_W, D_MODEL), MOBA_W ** -0.5),
        'w_br_gla': nrm((DEPTH, GLA_VW, D_MODEL), GLA_VW ** -0.5),
        'w_br_cross': nrm((DEPTH, CROSS_W, D_MODEL), CROSS_W ** -0.5),
        'w_gate': nrm((DEPTH, D_MODEL, N_BRANCH * D_MODEL), D_MODEL ** -0.5),
        'b_gate': nrm((DEPTH, N_BRANCH * D_MODEL), 0.1),
        'w_out': nrm((DEPTH, D_MODEL, D_MODEL), D_MODEL ** -0.5),
        'norm_ffn': gain((DEPTH, D_MODEL)),
        'w_up': nrm((DEPTH, D_MODEL, 2 * D_FF), D_MODEL ** -0.5),
        'w_conv': nrm((DEPTH, CONV_W, D_FF), CONV_W ** -0.5),
        'b_conv': nrm((DEPTH, D_FF), 0.02),
        'w_down': nrm((DEPTH, D_FF, D_MODEL), D_FF ** -0.5),
        'norm_final': gain((D_MODEL,)),
    }


def reference(x_prompt, x_sample, cache_moba_k, cache_moba_v, state_gla, state_conv, cache_mem_k, cache_mem_v,
              page_table, mem_prompt, norm_mix, w_in, w_gla_a2, b_gla_a, norm_gla, norm_mem, w_mem_kv,
              w_br_moba, w_br_gla, w_br_cross, w_gate, b_gate, w_out, norm_ffn, w_up, w_conv, b_conv,
              w_down, norm_final):
    hp, hs = x_prompt, x_sample
    kp_l, vp_l, ks_l, vs_l, gp_l, gs_l, cp_l, cs_l, mkp_l, mvp_l = ([] for _ in range(10))
    for l in range(DEPTH):
        lw = (norm_mix[l], w_in[l], w_gla_a2[l], b_gla_a[l], norm_gla[l], w_br_moba[l], w_br_gla[l],
              w_br_cross[l], w_gate[l], b_gate[l], w_out[l], norm_ffn[l], w_up[l], w_conv[l], b_conv[l],
              w_down[l])
        mk_p, mv_p = memory_kv(mem_prompt, norm_mem[l], w_mem_kv[l])
        gla0 = jnp.zeros((hp.shape[0], H_GLA, DK_GLA, DV_GLA), jnp.float32)
        conv0 = jnp.zeros((hp.shape[0], CONV_W - 1, D_FF), hp.dtype)
        hp, k_p, v_p, g_p, c_p = mixer_layer(hp, moba_prompt, gla0, mk_p, mv_p, conv0, *lw)
        moba_s = functools.partial(moba_sample, cache_k=cache_moba_k[l], cache_v=cache_moba_v[l],
                                   page_table=page_table)
        hs, k_s, v_s, g_s, c_s = mixer_layer(hs, moba_s, state_gla[l], cache_mem_k[l], cache_mem_v[l],
                                             state_conv[l], *lw)
        kp_l.append(k_p); vp_l.append(v_p); ks_l.append(k_s); vs_l.append(v_s)
        gp_l.append(g_p); gs_l.append(g_s); cp_l.append(c_p); cs_l.append(c_s)
        mkp_l.append(mk_p); mvp_l.append(mv_p)
    y_prompt = rmsnorm(hp, norm_final)
    y_sample = rmsnorm(hs, norm_final)
    return (y_prompt, y_sample,
            jnp.stack(kp_l), jnp.stack(vp_l), jnp.stack(ks_l), jnp.stack(vs_l),
            jnp.stack(gp_l), jnp.stack(gs_l), jnp.stack(cp_l), jnp.stack(cs_l),
            jnp.stack(mkp_l), jnp.stack(mvp_l))


import functools
```

```python
import functools

import jax
import jax.numpy as jnp
from jax import lax
from jax.experimental import pallas as pl
from jax.experimental.pallas import tpu as pltpu

F32 = jnp.float32
BF16 = jnp.bfloat16
NEG_INF = float("-inf")

EPS = 1e-6
H_MOBA, HD_MOBA = 8, 64
MOBA_W = H_MOBA * HD_MOBA
MOBA_BLOCK = 256
MOBA_TOPK = 3
PAGE_SIZE = 128
H_GLA, DK_GLA, DV_GLA = 4, 64, 128
GLA_KW, GLA_VW = H_GLA * DK_GLA, H_GLA * DV_GLA
GLA_RANK = 16
GLA_TAU = 16.0
GLA_CHUNK = 64
GLA_SUB = 16
H_CROSS, HD_CROSS = 4, 128
CROSS_W = H_CROSS * HD_CROSS
N_BRANCH = 3
CONV_W = 3

LANES = 128
SUBLANES = 8
VMEM_LIMIT_BYTES = 56 * 1024 * 1024

ROW_TILE = 256
PAGES_PER_STEP = 8
A_PAD = LANES

C_QM, C_K, C_V = 0, MOBA_W, 2 * MOBA_W
C_QG = 3 * MOBA_W
C_KG = C_QG + GLA_KW
C_VG = C_KG + GLA_KW
C_RG = C_VG + GLA_VW
C_QC = C_RG + GLA_VW
C_A = C_QC + CROSS_W
IN_COLS_PAD = C_A + A_PAD


def _params(n_axes):
    return pltpu.CompilerParams(
        dimension_semantics=("arbitrary",) * n_axes,
        vmem_limit_bytes=VMEM_LIMIT_BYTES,
    )


def _const_spec(shape):
    nd = len(shape)
    return pl.BlockSpec(shape, lambda *_: (0,) * nd, pipeline_mode=pl.Buffered(1))


def _rms(x, w):
    return x * lax.rsqrt(jnp.mean(x * x, axis=-1, keepdims=True) + EPS) * w


def _sigmoid(x):
    return 1.0 / (1.0 + jnp.exp(-x))


def _dot(a, b):
    return jnp.dot(a, b, preferred_element_type=F32)


def _dot_nt(a, b):
    return lax.dot_general(a, b, (((1,), (1,)), ((), ())), preferred_element_type=F32)


def _dot_tn(a, b):
    return lax.dot_general(a, b, (((0,), (0,)), ((), ())), preferred_element_type=F32)


def _proj_kernel(x_ref, nw_ref, w_ref, wa2_ref, ba_ref,
                 qm_ref, k_ref, v_ref, qg_ref, kg_ref, vg_ref, rg_ref, qc_ref, la_ref, ksum_ref):
    xb = _rms(x_ref[...], nw_ref[...]).astype(BF16)

    def mm(lo, hi):
        return _dot(xb, w_ref[:, lo:hi])

    qm_ref[...] = mm(C_QM, C_K).astype(qm_ref.dtype)
    k = mm(C_K, C_V)
    k_ref[...] = k
    for g in range(k.shape[0] // MOBA_BLOCK):
        ksum_ref[g] = jnp.sum(k[g * MOBA_BLOCK:(g + 1) * MOBA_BLOCK], axis=0, keepdims=True)
    v_ref[...] = mm(C_V, C_QG)
    qg_ref[...] = mm(C_QG, C_KG).astype(qg_ref.dtype)
    kg_ref[...] = mm(C_KG, C_VG).astype(kg_ref.dtype)
    vg_ref[...] = mm(C_VG, C_RG).astype(vg_ref.dtype)
    rg_ref[...] = mm(C_RG, C_QC)
    qc_ref[...] = mm(C_QC, C_A).astype(qc_ref.dtype)
    z = _dot(mm(C_A, IN_COLS_PAD).astype(BF16), wa2_ref[...]) + ba_ref[...]
    la_ref[...] = (jnp.minimum(z, 0.0) - jnp.log1p(jnp.exp(-jnp.abs(z)))) * (1.0 / GLA_TAU)


def _project(x2d, nw, w_in_p, wa2_p, ba, act_dtype):
    n, d = x2d.shape
    assert n % ROW_TILE == 0
    tm = ROW_TILE
    row = lambda w: pl.BlockSpec((tm, w), lambda i: (i, 0))
    widths = (MOBA_W, MOBA_W, MOBA_W, GLA_KW, GLA_KW, GLA_VW, GLA_VW, CROSS_W, GLA_KW)
    dtypes = (act_dtype, F32, F32, act_dtype, act_dtype, act_dtype, F32, act_dtype, F32)
    out_shape = [jax.ShapeDtypeStruct((n, w), dt) for w, dt in zip(widths, dtypes)]
    out_shape.append(jax.ShapeDtypeStruct((n // MOBA_BLOCK, 1, MOBA_W), F32))
    out_specs = [row(w) for w in widths]
    out_specs.append(pl.BlockSpec((tm // MOBA_BLOCK, 1, MOBA_W), lambda i: (i, 0, 0)))
    return pl.pallas_call(
        _proj_kernel,
        grid=(n // tm,),
        in_specs=[row(d), _const_spec((1, d)), _const_spec(w_in_p.shape),
                  _const_spec(wa2_p.shape), _const_spec((1, GLA_KW))],
        out_specs=out_specs,
        out_shape=out_shape,
        compiler_params=_params(1),
        name="proj",
    )(x2d, nw, w_in_p, wa2_p, ba)


def _memkv_kernel(m_ref, nw_ref, w_ref, mk_ref, mv_ref):
    mb = _rms(m_ref[...], nw_ref[...]).astype(BF16)
    mk_ref[...] = _dot(mb, w_ref[:, :CROSS_W])
    mv_ref[...] = _dot(mb, w_ref[:, CROSS_W:])


def _memory_kv(mem2d, nw, w_kv):
    n, d = mem2d.shape
    tm = ROW_TILE
    assert n % tm == 0
    return pl.pallas_call(
        _memkv_kernel,
        grid=(n // tm,),
        in_specs=[pl.BlockSpec((tm, d), lambda i: (i, 0)), _const_spec((1, d)), _const_spec(w_kv.shape)],
        out_specs=[pl.BlockSpec((tm, CROSS_W), lambda i: (i, 0))] * 2,
        out_shape=[jax.ShapeDtypeStruct((n, CROSS_W), F32)] * 2,
        compiler_params=_params(1),
        name="memkv",
    )(mem2d, nw, w_kv)


def _select_topk(gate, n_valid, axis):
    n = gate.shape[axis]
    idx = lax.broadcasted_iota(jnp.int32, gate.shape, axis).astype(F32)
    g = jnp.where(idx < n_valid, gate, NEG_INF)
    bias = jnp.full(gate.shape, NEG_INF, F32)
    for _ in range(min(MOBA_TOPK, n)):
        m = jnp.max(g, axis=axis, keepdims=True)
        first = jnp.min(jnp.where(g == m, idx, float(n)), axis=axis, keepdims=True)
        pick = (idx == first) & (m > NEG_INF)
        bias = jnp.where(pick, 0.0, bias)
        g = jnp.where(pick, NEG_INF, g)
    return bias


def _moba_prompt_kernel(q_ref, k_ref, v_ref, ksum_ref, o_ref, kb_sc, vb_sc):
    t = pl.program_id(2)
    tq = q_ref.shape[0]
    blk = MOBA_BLOCK
    scale = HD_MOBA ** -0.5

    @pl.when(t == 0)
    def _():
        kb_sc[...] = k_ref[...].astype(BF16)
        vb_sc[...] = v_ref[...].astype(BF16)

    q = q_ref[...]
    kmean = ksum_ref[...] * (1.0 / blk)
    km_hi = kmean.astype(BF16)
    km_lo = (kmean - km_hi.astype(F32)).astype(BF16)
    lane = lax.broadcasted_iota(jnp.int32, (1, LANES), 1)
    row = lax.broadcasted_iota(jnp.int32, (tq, blk), 0)
    col = lax.broadcasted_iota(jnp.int32, (tq, blk), 1)
    nblk = ksum_ref.shape[0]
    blk_id = lax.broadcasted_iota(jnp.int32, (tq, nblk), 1)

    own = pl.ds(pl.multiple_of(t * blk, blk), blk)
    k_own = kb_sc[own, :]
    v_own = vb_sc[own, :]

    outs = []
    for half in range(LANES // HD_MOBA):
        in_head = (lane >= half * HD_MOBA) & (lane < (half + 1) * HD_MOBA)
        qh = jnp.where(in_head, q, jnp.zeros_like(q))
        gate = _dot_nt(qh, km_hi) + _dot_nt(qh, km_lo)
        sel_bias = _select_topk(gate, t.astype(F32), axis=1)

        s = _dot_nt(qh, k_own) * scale
        s = jnp.where(col <= row, s, NEG_INF)
        m0 = jnp.max(s, axis=-1, keepdims=True)
        p = jnp.exp(s - m0)
        l0 = jnp.sum(p, axis=-1, keepdims=True)
        acc0 = _dot(p.astype(BF16), v_own)

        def body(j, carry):
            m_i, l_i, acc = carry
            rows = pl.ds(pl.multiple_of(j * blk, blk), blk)
            sj = _dot_nt(qh, kb_sc[rows, :]) * scale
            bias_j = jnp.max(jnp.where(blk_id == j, sel_bias, NEG_INF), axis=-1, keepdims=True)
            sj = sj + bias_j
            m_new = jnp.maximum(m_i, jnp.max(sj, axis=-1, keepdims=True))
            alpha = jnp.exp(m_i - m_new)
            pj = jnp.exp(sj - m_new)
            l_new = alpha * l_i + jnp.sum(pj, axis=-1, keepdims=True)
            acc_new = alpha * acc + _dot(pj.astype(BF16), vb_sc[rows, :])
            return m_new, l_new, acc_new

        _, l_f, acc_f = lax.fori_loop(0, t, body, (m0, l0, acc0))
        outs.append((in_head, acc_f / l_f))

    o = jnp.zeros((tq, LANES), F32)
    for in_head, oh in outs:
        o = jnp.where(in_head, oh, o)
    o_ref[...] = o.astype(o_ref.dtype)


def _moba_prompt(qm, k, v, ksum, n_seq, seq_len):
    assert seq_len % MOBA_BLOCK == 0
    nblk = seq_len // MOBA_BLOCK
    tq = MOBA_BLOCK
    n = n_seq * seq_len
    n_pair = MOBA_W // LANES
    seq_block = pl.BlockSpec((seq_len, LANES), lambda b, hp, t: (b, hp))
    return pl.pallas_call(
        _moba_prompt_kernel,
        grid=(n_seq, n_pair, nblk),
        in_specs=[
            pl.BlockSpec((tq, LANES), lambda b, hp, t: (b * nblk + t, hp)),
            seq_block, seq_block,
            pl.BlockSpec((nblk, LANES), lambda b, hp, t: (b, hp)),
        ],
        out_specs=pl.BlockSpec((tq, LANES), lambda b, hp, t: (b * nblk + t, hp)),
        out_shape=jax.ShapeDtypeStruct((n, MOBA_W), BF16),
        scratch_shapes=[pltpu.VMEM((seq_len, LANES), BF16), pltpu.VMEM((seq_len, LANES), BF16)],
        compiler_params=_params(3),
        name="moba_prompt",
    )(qm, k, v, ksum)


def _moba_sample_kernel(pt_ref, qbd_ref, kn_ref, vn_ref, *rest, n_q):
    del pt_ref
    pps = PAGES_PER_STEP
    k_pages, v_pages = rest[:pps], rest[pps:2 * pps]
    o_ref, m_sc, l_sc, ksum_sc, acc_sc = rest[2 * pps:]
    s_id = pl.program_id(1)
    n_step = pl.num_programs(1)
    nblk = m_sc.shape[1]
    n_hq = qbd_ref.shape[1]
    scale = HD_MOBA ** -0.5
    qbd = qbd_ref[0]
    hq_row = lax.broadcasted_iota(jnp.int32, (n_hq, MOBA_W), 0)
    w_lane = lax.broadcasted_iota(jnp.int32, (n_hq, MOBA_W), 1)
    head_mask = (w_lane // HD_MOBA) == (hq_row // n_q)
    blk_col = lax.broadcasted_iota(jnp.int32, (n_hq, nblk), 1)
    blk_row = lax.broadcasted_iota(jnp.int32, (nblk, MOBA_W), 0)
    pages_per_blk = MOBA_BLOCK // PAGE_SIZE

    for i in range(pps // pages_per_blk):
        kb = jnp.concatenate([k_pages[i * pages_per_blk + j][0] for j in range(pages_per_blk)], axis=0)
        vb = jnp.concatenate([v_pages[i * pages_per_blk + j][0] for j in range(pages_per_blk)], axis=0)
        b_idx = s_id * (pps // pages_per_blk) + i
        ksum = jnp.sum(kb, axis=0, keepdims=True)
        s = _dot_nt(qbd, kb.astype(BF16)) * scale
        m = jnp.max(s, axis=-1, keepdims=True)
        p = jnp.exp(s - m)
        l = jnp.sum(p, axis=-1, keepdims=True)
        pv = _dot(p.astype(BF16), vb.astype(BF16))
        acc_sc[b_idx] = jnp.where(head_mask, pv, 0.0)
        m_sc[...] = jnp.where(blk_col == b_idx, m, m_sc[...])
        l_sc[...] = jnp.where(blk_col == b_idx, l, l_sc[...])
        ksum_sc[...] = jnp.where(blk_row == b_idx, ksum, ksum_sc[...])

    @pl.when(s_id == n_step - 1)
    def _():
        kmean = ksum_sc[...] * (1.0 / MOBA_BLOCK)
        km_hi = kmean.astype(BF16)
        km_lo = (kmean - km_hi.astype(F32)).astype(BF16)
        gate = _dot_nt(qbd, km_hi) + _dot_nt(qbd, km_lo)
        sel_bias = _select_topk(gate, float(nblk), axis=1)
        m_all = m_sc[...] + sel_bias
        kn = kn_ref[0].astype(BF16)
        vn = vn_ref[0].astype(BF16)
        n_new = kn.shape[0]
        s_own = _dot_nt(qbd, kn) * scale
        key_i = lax.broadcasted_iota(jnp.int32, (n_hq, n_new), 1)
        q_i = lax.broadcasted_iota(jnp.int32, (n_hq, n_new), 0) % n_q
        s_own = jnp.where(key_i <= q_i, s_own, NEG_INF)
        m_tot = jnp.maximum(jnp.max(s_own, axis=-1, keepdims=True),
                            jnp.max(m_all, axis=-1, keepdims=True))
        p_own = jnp.exp(s_own - m_tot)
        c = jnp.exp(m_all - m_tot)
        l_tot = jnp.sum(p_own, axis=-1, keepdims=True) + jnp.sum(c * l_sc[...], axis=-1, keepdims=True)
        out = jnp.where(head_mask, _dot(p_own.astype(BF16), vn), 0.0)
        for b in range(nblk):
            out = out + c[:, b:b + 1] * acc_sc[b]
        out = out / l_tot
        o = out[0:n_q]
        for h in range(1, H_MOBA):
            o = o + out[h * n_q:(h + 1) * n_q]
        o_ref[0] = o


def _moba_sample(q, k_new, v_new, cache_k, cache_v, page_table):
    n_seq, n_q, _ = q.shape
    n_pages = page_table.shape[1]
    past = n_pages * PAGE_SIZE
    assert past % MOBA_BLOCK == 0 and n_pages % PAGES_PER_STEP == 0
    assert n_q <= MOBA_BLOCK and n_q % SUBLANES == 0
    nblk = past // MOBA_BLOCK
    n_hq = H_MOBA * n_q
    lane_head = jnp.arange(MOBA_W) // HD_MOBA
    qbd = jnp.where(lane_head[None, None, None, :] == jnp.arange(H_MOBA)[None, :, None, None],
                    q[:, None, :, :], 0.0).reshape(n_seq, n_hq, MOBA_W).astype(BF16)
    n_new = max(2 * SUBLANES, n_q)
    pad = ((0, 0), (0, n_new - n_q), (0, 0))
    kn = jnp.pad(k_new, pad)
    vn = jnp.pad(v_new, pad)

    def page_spec(i):
        return pl.BlockSpec((1, PAGE_SIZE, MOBA_W), lambda b, s, pt: (pt[b, s * PAGES_PER_STEP + i], 0, 0))

    seq3 = lambda r: pl.BlockSpec((1, r, MOBA_W), lambda b, s, pt: (b, 0, 0))
    grid_spec = pltpu.PrefetchScalarGridSpec(
        num_scalar_prefetch=1,
        grid=(n_seq, n_pages // PAGES_PER_STEP),
        in_specs=[seq3(n_hq), seq3(n_new), seq3(n_new)]
        + [page_spec(i) for i in range(PAGES_PER_STEP)] * 2,
        out_specs=seq3(n_q),
        scratch_shapes=[
            pltpu.VMEM((n_hq, nblk), F32),
            pltpu.VMEM((n_hq, nblk), F32),
            pltpu.VMEM((nblk, MOBA_W), F32),
            pltpu.VMEM((nblk, n_hq, MOBA_W), F32),
        ],
    )
    return pl.pallas_call(
        functools.partial(_moba_sample_kernel, n_q=n_q),
        grid_spec=grid_spec,
        out_shape=jax.ShapeDtypeStruct((n_seq, n_q, MOBA_W), F32),
        compiler_params=_params(2),
        name="moba_sample",
    )(page_table, qbd, kn, vn, *([cache_k] * PAGES_PER_STEP), *([cache_v] * PAGES_PER_STEP))


def _cumsum_rows(x):
    n = x.shape[0]
    row = lax.broadcasted_iota(jnp.int32, x.shape, 0)
    s = 1
    while s < n:
        x = x + jnp.where(row >= s, pltpu.roll(x, s, axis=0), 0.0)
        s *= 2
    return x


def _gla_chunk(q, k, v, la, st, cast):
    c = q.shape[0]
    sub = min(GLA_SUB, c)
    lane = lax.broadcasted_iota(jnp.int32, (1, GLA_KW), 1)
    head_masks = [(lane >= h * DK_GLA) & (lane < (h + 1) * DK_GLA) for h in range(H_GLA)]
    g = _cumsum_rows(la)
    krow = lax.broadcasted_iota(jnp.int32, (c, GLA_KW), 0)
    a_row = lax.broadcasted_iota(jnp.int32, (H_GLA * sub, c), 0)
    a_col = lax.broadcasted_iota(jnp.int32, (H_GLA * sub, c), 1)

    a_parts = []
    for i in range(c // sub):
        g_ref = jnp.zeros((1, GLA_KW), F32) if i == 0 else g[i * sub - 1:i * sub]
        qt = q[i * sub:(i + 1) * sub] * jnp.exp(g[i * sub:(i + 1) * sub] - g_ref)
        kt = jnp.where(krow < (i + 1) * sub, k * jnp.exp(g_ref - g), 0.0)
        q_stack = jnp.concatenate([jnp.where(hm, qt, 0.0) for hm in head_masks], axis=0)
        a = _dot_nt(cast(q_stack), cast(kt))
        a_parts.append(jnp.where(a_col <= i * sub + a_row % sub, a, 0.0))

    qe = q * jnp.exp(g)
    g_last = g[c - 1:c]
    k_dec = cast(k * jnp.exp(g_last - g))
    st_c = cast(st)
    outs = []
    st_new = st * jnp.exp(g_last)
    for h in range(H_GLA):
        a_h = jnp.concatenate([a[h * sub:(h + 1) * sub] for a in a_parts], axis=0)
        v_h = cast(v[:, h * DV_GLA:(h + 1) * DV_GLA])
        o_intra = _dot(cast(a_h), v_h)
        o_inter = _dot_nt(cast(jnp.where(head_masks[h], qe, 0.0)), st_c)
        outs.append(o_intra + o_inter)
        st_new = st_new + jnp.where(head_masks[h], _dot_tn(v_h, k_dec), 0.0)
    return outs, st_new


def _gla_kernel(q_ref, k_ref, v_ref, la_ref, r_ref, s0_ref, nw_ref, o_ref, sfin_ref, st_sc, *, chunk):
    t = pl.program_id(1)

    @pl.when(t == 0)
    def _():
        st_sc[...] = s0_ref[0]

    cast = (lambda a: a.astype(BF16)) if chunk >= 2 * SUBLANES else (lambda a: a)
    st = st_sc[...]
    for c in range(q_ref.shape[0] // chunk):
        rows = slice(c * chunk, (c + 1) * chunk)
        q = q_ref[rows, :].astype(F32) * (DK_GLA ** -0.5)
        k = k_ref[rows, :].astype(F32)
        outs, st = _gla_chunk(q, k, v_ref[rows, :], la_ref[rows, :], st, cast)
        for h, o in enumerate(outs):
            lanes = slice(h * DV_GLA, (h + 1) * DV_GLA)
            r = r_ref[rows, lanes]
            o = o * lax.rsqrt(jnp.mean(o * o, axis=-1, keepdims=True) + EPS)
            o_ref[rows, lanes] = (o * nw_ref[:, lanes] * (r * _sigmoid(r))).astype(o_ref.dtype)
    st_sc[...] = st

    @pl.when(t == pl.num_programs(1) - 1)
    def _():
        sfin_ref[0] = st


def _gla(qg, kg, vg, la, rg, state_t, nw, n_seq, seq_len, out_dtype):
    import math
    chunk = math.gcd(seq_len, GLA_CHUNK)
    tl = min(seq_len, ROW_TILE)
    assert seq_len % tl == 0 and tl % chunk == 0
    nt = seq_len // tl
    n = n_seq * seq_len
    row = lambda w: pl.BlockSpec((tl, w), lambda b, t: (b * nt + t, 0))
    st_spec = pl.BlockSpec((1, DV_GLA, GLA_KW), lambda b, t: (b, 0, 0))
    return pl.pallas_call(
        functools.partial(_gla_kernel, chunk=chunk),
        grid=(n_seq, nt),
        in_specs=[row(GLA_KW), row(GLA_KW), row(GLA_VW), row(GLA_KW), row(GLA_VW), st_spec,
                  _const_spec((1, GLA_VW))],
        out_specs=[row(GLA_VW), st_spec],
        out_shape=[jax.ShapeDtypeStruct((n, GLA_VW), out_dtype),
                   jax.ShapeDtypeStruct((n_seq, DV_GLA, GLA_KW), F32)],
        scratch_shapes=[pltpu.VMEM((DV_GLA, GLA_KW), F32)],
        compiler_params=_params(2),
        name="gla",
    )(qg, kg, vg, la, rg, state_t, nw)


def _state_to_t(s):
    n = s.shape[0]
    return jnp.transpose(s, (0, 3, 1, 2)).reshape(n, DV_GLA, GLA_KW)


def _state_from_t(st):
    n = st.shape[0]
    return jnp.transpose(st.reshape(n, DV_GLA, H_GLA, DK_GLA), (0, 2, 3, 1))


def _cross_kernel(q_ref, mk_ref, mv_ref, o_ref):
    scale = HD_CROSS ** -0.5
    small = q_ref.shape[0] < 2 * SUBLANES
    cast = (lambda a: a) if small else (lambda a: a.astype(BF16))
    for h in range(H_CROSS):
        lanes = slice(h * HD_CROSS, (h + 1) * HD_CROSS)
        s = _dot_nt(cast(q_ref[:, lanes]), cast(mk_ref[:, lanes])) * scale
        m = jnp.max(s, axis=-1, keepdims=True)
        p = jnp.exp(s - m)
        l = jnp.sum(p, axis=-1, keepdims=True)
        o_ref[:, lanes] = (_dot(cast(p), cast(mv_ref[:, lanes])) / l).astype(o_ref.dtype)


def _cross(qc, mk, mv, n_seq, seq_len, n_mem, out_dtype):
    tl = min(seq_len, ROW_TILE)
    nt = seq_len // tl
    row = pl.BlockSpec((tl, CROSS_W), lambda b, t: (b * nt + t, 0))
    mem = pl.BlockSpec((n_mem, CROSS_W), lambda b, t: (b, 0))
    return pl.pallas_call(
        _cross_kernel,
        grid=(n_seq, nt),
        in_specs=[row, mem, mem],
        out_specs=row,
        out_shape=jax.ShapeDtypeStruct((n_seq * seq_len, CROSS_W), out_dtype),
        compiler_params=_params(2),
        name="cross",
    )(qc, mk, mv)


def _merge_kernel(x_ref, om_ref, og_ref, oc_ref, nw_ref, wg_ref, bg_ref, wbm_ref, wbg_ref, wbc_ref,
                  wo_ref, h_ref):
    x = x_ref[...]
    d = x.shape[1]
    xb = _rms(x, nw_ref[...]).astype(BF16)
    merged = jnp.zeros(x.shape, F32)
    for i, (o_ref, wb_ref) in enumerate(((om_ref, wbm_ref), (og_ref, wbg_ref), (oc_ref, wbc_ref))):
        cols = slice(i * d, (i + 1) * d)
        gate = _sigmoid(_dot(xb, wg_ref[:, cols]) + bg_ref[:, cols])
        merged = merged + gate * _dot(o_ref[...].astype(BF16), wb_ref[...])
    h_ref[...] = x + _dot(merged.astype(BF16), wo_ref[...])


def _merge(x2d, om, og, oc, nw, wg, bg, wbm, wbg, wbc, wo):
    n, d = x2d.shape
    tm = ROW_TILE
    row = lambda w: pl.BlockSpec((tm, w), lambda i: (i, 0))
    return pl.pallas_call(
        _merge_kernel,
        grid=(n // tm,),
        in_specs=[row(d), row(MOBA_W), row(GLA_VW), row(CROSS_W), _const_spec((1, d)),
                  _const_spec(wg.shape), _const_spec(bg.shape), _const_spec(wbm.shape),
                  _const_spec(wbg.shape), _const_spec(wbc.shape), _const_spec(wo.shape)],
        out_specs=row(d),
        out_shape=jax.ShapeDtypeStruct((n, d), F32),
        compiler_params=_params(1),
        name="merge",
    )(x2d, om, og, oc, nw, wg, bg, wbm, wbg, wbc, wo)


def _gelu_tanh(x):
    return 0.5 * x * (1.0 + jnp.tanh(0.7978845608028654 * (x + 0.044715 * (x * x * x))))


def _ffn_body(h_ref, nw_ref, wup_ref, wc_ref, bc_ref, wdn_ref, nf_ref, y_ref, prev_rows, store_u):
    h = h_ref[...]
    hb = _rms(h, nw_ref[...]).astype(BF16)
    d_ff = wdn_ref.shape[0]
    n_col_chunks = 2
    fc = d_ff // n_col_chunks
    assert fc % LANES == 0
    acc = jnp.zeros(h.shape, F32)
    for c in range(n_col_chunks):
        cols = slice(c * fc, (c + 1) * fc)
        u = _dot(hb, wup_ref[:, cols])
        gate = _dot(hb, wup_ref[:, d_ff + c * fc:d_ff + (c + 1) * fc])
        u1, u2 = prev_rows(cols, u)
        store_u(cols, u)
        conv = bc_ref[:, cols] + wc_ref[0:1, cols] * u2 + wc_ref[1:2, cols] * u1 + wc_ref[2:3, cols] * u
        act = (_gelu_tanh(conv) * gate).astype(BF16)
        acc = acc + _dot(act, wdn_ref[cols, :])
    y_ref[...] = _rms(h + acc, nf_ref[...])


def _ffn_long_kernel(h_ref, nw_ref, wup_ref, wc_ref, bc_ref, wdn_ref, nf_ref, y_ref, cs_ref, carry_sc):
    tm = h_ref.shape[0]

    @pl.when(pl.program_id(1) == 0)
    def _():
        carry_sc[...] = jnp.zeros(carry_sc.shape, F32)

    row = lax.broadcasted_iota(jnp.int32, (tm, 1), 0)

    def prev_rows(cols, u):
        last = carry_sc[SUBLANES - 1:SUBLANES, cols]
        last2 = carry_sc[SUBLANES - 2:SUBLANES - 1, cols]
        u1 = jnp.where(row == 0, last, pltpu.roll(u, 1, axis=0))
        u2 = jnp.where(row == 0, last2, jnp.where(row == 1, last, pltpu.roll(u, 2, axis=0)))
        return u1, u2

    def store_u(cols, u):
        carry_sc[:, cols] = u[tm - SUBLANES:tm]
        cs_ref[0, :, cols] = u[tm - (CONV_W - 1):tm]

    _ffn_body(h_ref, nw_ref, wup_ref, wc_ref, bc_ref, wdn_ref, nf_ref, y_ref, prev_rows, store_u)


def _ffn_short_kernel(h_ref, p1_ref, p2_ref, nw_ref, wup_ref, wc_ref, bc_ref, wdn_ref, nf_ref,
                      y_ref, u_ref, *, seq_len):
    tm = h_ref.shape[0]
    pos = lax.broadcasted_iota(jnp.int32, (tm, 1), 0) % seq_len

    def prev_rows(cols, u):
        u1 = jnp.where(pos >= 1, pltpu.roll(u, 1, axis=0), p1_ref[:, cols])
        u2 = jnp.where(pos >= 2, pltpu.roll(u, 2, axis=0), p2_ref[:, cols])
        return u1, u2

    def store_u(cols, u):
        u_ref[:, cols] = u

    _ffn_body(h_ref, nw_ref, wup_ref, wc_ref, bc_ref, wdn_ref, nf_ref, y_ref, prev_rows, store_u)


def _ffn_weights_specs(d, wup, wc, bc, wdn):
    return [_const_spec((1, d)), _const_spec(wup.shape), _const_spec(wc.shape), _const_spec(bc.shape),
            _const_spec(wdn.shape), _const_spec((1, d))]


def _ffn_long(h2d, n_seq, seq_len, nw, wup, wc, bc, wdn, nf):
    n, d = h2d.shape
    d_ff = wdn.shape[0]
    tm = ROW_TILE
    assert seq_len % tm == 0
    nt = seq_len // tm
    row = pl.BlockSpec((tm, d), lambda b, t: (b * nt + t, 0))
    return pl.pallas_call(
        _ffn_long_kernel,
        grid=(n_seq, nt),
        in_specs=[row] + _ffn_weights_specs(d, wup, wc, bc, wdn),
        out_specs=[row, pl.BlockSpec((1, CONV_W - 1, d_ff), lambda b, t: (b, 0, 0))],
        out_shape=[jax.ShapeDtypeStruct((n, d), F32),
                   jax.ShapeDtypeStruct((n_seq, CONV_W - 1, d_ff), F32)],
        scratch_shapes=[pltpu.VMEM((SUBLANES, d_ff), F32)],
        compiler_params=_params(2),
        name="ffn_long",
    )(h2d, nw, wup, wc, bc, wdn, nf)


def _ffn_short(h2d, conv_prev, n_seq, seq_len, nw, wup, wc, bc, wdn, nf):
    n, d = h2d.shape
    d_ff = wdn.shape[0]
    tm = ROW_TILE
    assert tm % seq_len == 0 and n % tm == 0 and seq_len >= CONV_W - 1
    zeros = jnp.zeros((n_seq, seq_len, d_ff), F32)
    p1 = zeros.at[:, 0].set(conv_prev[:, 1]).reshape(n, d_ff)
    p2 = zeros.at[:, 0].set(conv_prev[:, 0]).at[:, 1].set(conv_prev[:, 1]).reshape(n, d_ff)
    row = lambda w: pl.BlockSpec((tm, w), lambda i: (i, 0))
    y, u = pl.pallas_call(
        functools.partial(_ffn_short_kernel, seq_len=seq_len),
        grid=(n // tm,),
        in_specs=[row(d), row(d_ff), row(d_ff)] + _ffn_weights_specs(d, wup, wc, bc, wdn),
        out_specs=[row(d), row(d_ff)],
        out_shape=[jax.ShapeDtypeStruct((n, d), F32), jax.ShapeDtypeStruct((n, d_ff), F32)],
        compiler_params=_params(1),
        name="ffn_short",
    )(h2d, p1, p2, nw, wup, wc, bc, wdn, nf)
    return y, u.reshape(n_seq, seq_len, d_ff)[:, seq_len - (CONV_W - 1):]


def _prep_weights(norm_mix, w_in, w_gla_a2, b_gla_a, norm_gla, norm_mem, w_mem_kv, w_br_moba, w_br_gla,
                  w_br_cross, w_gate, b_gate, w_out, norm_ffn, w_up, w_conv, b_conv, w_down, norm_final):
    d = w_in.shape[0]
    o_a = C_QC
    w_in_p = jnp.concatenate(
        [w_in[:, :o_a], w_in[:, o_a + GLA_RANK:], w_in[:, o_a:o_a + GLA_RANK],
         jnp.zeros((d, A_PAD - GLA_RANK), w_in.dtype)], axis=1).astype(BF16)
    wa2_p = jnp.concatenate([w_gla_a2, jnp.zeros((A_PAD - GLA_RANK, GLA_KW), w_gla_a2.dtype)],
                            axis=0).astype(BF16)
    r2 = lambda a: a.reshape(1, -1)
    return dict(
        norm_mix=r2(norm_mix), w_in=w_in_p, wa2=wa2_p, ba=r2(b_gla_a), norm_gla=r2(norm_gla),
        norm_mem=r2(norm_mem), w_mem_kv=w_mem_kv.astype(BF16), wbm=w_br_moba.astype(BF16),
        wbg=w_br_gla.astype(BF16), wbc=w_br_cross.astype(BF16), wg=w_gate.astype(BF16), bg=r2(b_gate),
        wo=w_out.astype(BF16), norm_ffn=r2(norm_ffn), wup=w_up.astype(BF16), wc=w_conv, bc=r2(b_conv),
        wdn=w_down.astype(BF16), norm_final=r2(norm_final))


def kernel(x_prompt, x_sample, cache_moba_k, cache_moba_v, state_gla, state_conv, cache_mem_k, cache_mem_v, page_table, mem_prompt, norm_mix, w_in, w_gla_a2, b_gla_a, norm_gla, norm_mem, w_mem_kv, w_br_moba, w_br_gla, w_br_cross, w_gate, b_gate, w_out, norm_ffn, w_up, w_conv, b_conv, w_down, norm_final):
    depth = w_in.shape[0]
    assert depth == 1
    bp, sp, d = x_prompt.shape
    bs, ss, _ = x_sample.shape
    n_mem = mem_prompt.shape[1]
    n_phys = cache_moba_k.shape[1]
    w = _prep_weights(norm_mix[0], w_in[0], w_gla_a2[0], b_gla_a[0], norm_gla[0], norm_mem[0], w_mem_kv[0],
                      w_br_moba[0], w_br_gla[0], w_br_cross[0], w_gate[0], b_gate[0], w_out[0], norm_ffn[0],
                      w_up[0], w_conv[0], b_conv[0], w_down[0], norm_final)

    def mix(x2d, o_m, o_g, o_c):
        return _merge(x2d, o_m, o_g, o_c, w["norm_mix"], w["wg"], w["bg"], w["wbm"], w["wbg"], w["wbc"], w["wo"])

    ffn_w = (w["norm_ffn"], w["wup"], w["wc"], w["bc"], w["wdn"], w["norm_final"])

    xp = x_prompt.reshape(bp * sp, d)
    qm, k_p, v_p, qg, kg, vg, rg, qc, la, ksum = _project(xp, w["norm_mix"], w["w_in"], w["wa2"], w["ba"], BF16)
    o_m = _moba_prompt(qm, k_p, v_p, ksum.reshape(-1, MOBA_W), bp, sp)
    mk_p, mv_p = _memory_kv(mem_prompt.reshape(bp * n_mem, d), w["norm_mem"], w["w_mem_kv"])
    o_g, gla_p = _gla(qg, kg, vg, la, rg, jnp.zeros((bp, DV_GLA, GLA_KW), F32), w["norm_gla"], bp, sp, BF16)
    o_c = _cross(qc, mk_p, mv_p, bp, sp, n_mem, BF16)
    h_p = mix(xp, o_m, o_g, o_c)
    y_p, conv_p = _ffn_long(h_p, bp, sp, *ffn_w)

    xs = x_sample.reshape(bs * ss, d)
    qm, k_s, v_s, qg, kg, vg, rg, qc, la, _ = _project(xs, w["norm_mix"], w["w_in"], w["wa2"], w["ba"], F32)
    r3 = lambda a: a.reshape(bs, ss, MOBA_W)
    o_m = _moba_sample(r3(qm), r3(k_s), r3(v_s), cache_moba_k[0].reshape(n_phys, PAGE_SIZE, MOBA_W),
                       cache_moba_v[0].reshape(n_phys, PAGE_SIZE, MOBA_W), page_table).reshape(bs * ss, MOBA_W)
    o_g, gla_s = _gla(qg, kg, vg, la, rg, _state_to_t(state_gla[0]), w["norm_gla"], bs, ss, F32)
    o_c = _cross(qc, cache_mem_k[0].reshape(bs * n_mem, CROSS_W), cache_mem_v[0].reshape(bs * n_mem, CROSS_W),
                 bs, ss, n_mem, F32)
    h_s = mix(xs, o_m, o_g, o_c)
    y_s, conv_s = _ffn_short(h_s, state_conv[0], bs, ss, *ffn_w)

    kv5 = lambda a, b, s: a.reshape(1, b, s, H_MOBA, HD_MOBA)
    return (y_p.reshape(bp, sp, d), y_s.reshape(bs, ss, d),
            kv5(k_p, bp, sp), kv5(v_p, bp, sp), kv5(k_s, bs, ss), kv5(v_s, bs, ss),
            _state_from_t(gla_p)[None], _state_from_t(gla_s)[None],
            conv_p[None], conv_s[None],
            mk_p.reshape(1, bp, n_mem, H_CROSS, HD_CROSS), mv_p.reshape(1, bp, n_mem, H_CROSS, HD_CROSS))
```

```python
import functools
import math

import jax
import jax.numpy as jnp
from jax import lax
from jax.experimental import pallas as pl
from jax.experimental.pallas import tpu as pltpu

F32 = jnp.float32
BF16 = jnp.bfloat16
NEG_INF = float("-inf")

EPS = 1e-6
H_MOBA, HD_MOBA = 8, 64
MOBA_W = H_MOBA * HD_MOBA
MOBA_BLOCK = 256
MOBA_TOPK = 3
PAGE_SIZE = 128
H_GLA, DK_GLA, DV_GLA = 4, 64, 128
GLA_KW, GLA_VW = H_GLA * DK_GLA, H_GLA * DV_GLA
GLA_RANK = 16
GLA_TAU = 16.0
GLA_CHUNK = 64
GLA_SUB = 16
H_CROSS, HD_CROSS = 4, 128
CROSS_W = H_CROSS * HD_CROSS
N_BRANCH = 3
CONV_W = 3

LANES = 128
SUBLANES = 8
VMEM_LIMIT_BYTES = 56 * 1024 * 1024

MASK_BIAS = -1e30
MOBA_GROUP = 4
ROW_TILE = 256
PAGES_PER_STEP = 8
A_PAD = LANES

C_QM, C_K, C_V = 0, MOBA_W, 2 * MOBA_W
C_QG = 3 * MOBA_W
C_KG = C_QG + GLA_KW
C_VG = C_KG + GLA_KW
C_RG = C_VG + GLA_VW
C_QC = C_RG + GLA_VW
C_A = C_QC + CROSS_W
IN_COLS_PAD = C_A + A_PAD


def _params(n_axes):
    return pltpu.CompilerParams(
        dimension_semantics=("arbitrary",) * n_axes,
        vmem_limit_bytes=VMEM_LIMIT_BYTES,
    )


def _const_spec(shape):
    nd = len(shape)
    return pl.BlockSpec(shape, lambda *_: (0,) * nd, pipeline_mode=pl.Buffered(1))


def _rms(x, w):
    return x * lax.rsqrt(jnp.mean(x * x, axis=-1, keepdims=True) + EPS) * w


def _sigmoid(x):
    return 1.0 / (1.0 + jnp.exp(-x))


def _dot(a, b):
    return jnp.dot(a, b, preferred_element_type=F32)


def _dot_nt(a, b):
    return lax.dot_general(a, b, (((1,), (1,)), ((), ())), preferred_element_type=F32)


def _dot_tn(a, b):
    return lax.dot_general(a, b, (((0,), (0,)), ((), ())), preferred_element_type=F32)


def _proj_kernel(x_ref, nw_ref, w_ref, wa2_ref, ba_ref,
                 qm_ref, k_ref, v_ref, qg_ref, kg_ref, vg_ref, rg_ref, qc_ref, la_ref, ksum_ref,
                 kb_ref, vb_ref):
    xb = _rms(x_ref[...], nw_ref[...]).astype(BF16)

    def mm(lo, hi):
        return _dot(xb, w_ref[:, lo:hi])

    qm_ref[...] = mm(C_QM, C_K).astype(qm_ref.dtype)
    k = mm(C_K, C_V)
    k_ref[...] = k
    kb_ref[...] = k.astype(BF16)
    for g in range(k.shape[0] // MOBA_BLOCK):
        ksum_ref[g] = jnp.sum(k[g * MOBA_BLOCK:(g + 1) * MOBA_BLOCK], axis=0, keepdims=True)
    v = mm(C_V, C_QG)
    v_ref[...] = v
    vb_ref[...] = v.astype(BF16)
    qg_ref[...] = mm(C_QG, C_KG).astype(qg_ref.dtype)
    kg_ref[...] = mm(C_KG, C_VG).astype(kg_ref.dtype)
    vg_ref[...] = mm(C_VG, C_RG).astype(vg_ref.dtype)
    rg_ref[...] = mm(C_RG, C_QC)
    qc_ref[...] = mm(C_QC, C_A).astype(qc_ref.dtype)
    z = _dot(mm(C_A, IN_COLS_PAD).astype(BF16), wa2_ref[...]) + ba_ref[...]
    la_ref[...] = (jnp.minimum(z, 0.0) - jnp.log1p(jnp.exp(-jnp.abs(z)))) * (1.0 / GLA_TAU)


def _project(x2d, nw, w_in_p, wa2_p, ba, act_dtype):
    n, d = x2d.shape
    assert n % ROW_TILE == 0
    tm = ROW_TILE
    row = lambda w: pl.BlockSpec((tm, w), lambda i: (i, 0))
    widths = (MOBA_W, MOBA_W, MOBA_W, GLA_KW, GLA_KW, GLA_VW, GLA_VW, CROSS_W, GLA_KW)
    dtypes = (act_dtype, F32, F32, act_dtype, act_dtype, act_dtype, F32, act_dtype, F32)
    out_shape = [jax.ShapeDtypeStruct((n, w), dt) for w, dt in zip(widths, dtypes)]
    out_shape.append(jax.ShapeDtypeStruct((n // MOBA_BLOCK, 1, MOBA_W), F32))
    out_shape += [jax.ShapeDtypeStruct((n, MOBA_W), BF16)] * 2
    out_specs = [row(w) for w in widths]
    out_specs.append(pl.BlockSpec((tm // MOBA_BLOCK, 1, MOBA_W), lambda i: (i, 0, 0)))
    out_specs += [row(MOBA_W)] * 2
    return pl.pallas_call(
        _proj_kernel,
        grid=(n // tm,),
        in_specs=[row(d), _const_spec((1, d)), _const_spec(w_in_p.shape),
                  _const_spec(wa2_p.shape), _const_spec((1, GLA_KW))],
        out_specs=out_specs,
        out_shape=out_shape,
        compiler_params=_params(1),
        name="proj",
    )(x2d, nw, w_in_p, wa2_p, ba)


def _memkv_kernel(m_ref, nw_ref, w_ref, mk_ref, mv_ref):
    mb = _rms(m_ref[...], nw_ref[...]).astype(BF16)
    mk_ref[...] = _dot(mb, w_ref[:, :CROSS_W])
    mv_ref[...] = _dot(mb, w_ref[:, CROSS_W:])


def _memory_kv(mem2d, nw, w_kv):
    n, d = mem2d.shape
    tm = ROW_TILE
    assert n % tm == 0
    return pl.pallas_call(
        _memkv_kernel,
        grid=(n // tm,),
        in_specs=[pl.BlockSpec((tm, d), lambda i: (i, 0)), _const_spec((1, d)), _const_spec(w_kv.shape)],
        out_specs=[pl.BlockSpec((tm, CROSS_W), lambda i: (i, 0))] * 2,
        out_shape=[jax.ShapeDtypeStruct((n, CROSS_W), F32)] * 2,
        compiler_params=_params(1),
        name="memkv",
    )(mem2d, nw, w_kv)


def _select_topk(gate, idx, valid, axis):
    g = jnp.where(valid, gate, NEG_INF)
    picked = jnp.zeros(gate.shape, jnp.bool_)
    for _ in range(MOBA_TOPK):
        m = jnp.max(g, axis=axis, keepdims=True)
        first = jnp.min(jnp.where(g == m, idx, float(2 ** 24)), axis=axis, keepdims=True)
        pick = (idx == first) & (m > NEG_INF) & valid
        picked = picked | pick
        g = jnp.where(pick, NEG_INF, g)
    return picked


def _moba_prompt_kernel(q_ref, k_ref, v_ref, ksum_ref, o_ref, ka_sc, s_sc):
    t = pl.program_id(2)
    tq = q_ref.shape[0]
    blk = MOBA_BLOCK
    nblk = ksum_ref.shape[0]
    seq_len = k_ref.shape[0]
    group = s_sc.shape[2] // blk
    n_heads = LANES // HD_MOBA
    bias_off = [((half + 1) % n_heads) * HD_MOBA for half in range(n_heads)]

    @pl.when(t == 0)
    def _():
        key_blk = lax.broadcasted_iota(jnp.int32, (seq_len, LANES), 0) // blk
        key_lane = lax.broadcasted_iota(jnp.int32, (seq_len, LANES), 1)
        k = k_ref[...]
        for half in range(n_heads):
            in_head = (key_lane >= half * HD_MOBA) & (key_lane < (half + 1) * HD_MOBA)
            onehot = jnp.where(key_lane - bias_off[half] == key_blk, 1.0, 0.0).astype(BF16)
            ka_sc[half] = jnp.where(in_head, k, onehot)

    q = q_ref[...] * (HD_MOBA ** -0.5)
    kmean = ksum_ref[...] * (1.0 / blk)
    km_hi = kmean.astype(BF16)
    km_lo = (kmean - km_hi.astype(F32)).astype(BF16)
    lane = lax.broadcasted_iota(jnp.int32, (1, LANES), 1)
    row = lax.broadcasted_iota(jnp.int32, (tq, blk), 0)
    col = lax.broadcasted_iota(jnp.int32, (tq, blk), 1)
    own = pl.ds(pl.multiple_of(t * blk, blk), blk)
    v_own = v_ref[own, :]
    n_groups = (t + group - 1) // group

    def embed(km, off):
        parts = [jnp.zeros((off, LANES), BF16)] if off else []
        parts.append(km)
        if LANES - off - nblk:
            parts.append(jnp.zeros((LANES - off - nblk, LANES), BF16))
        return jnp.concatenate(parts, axis=0)

    outs = []
    for half in range(n_heads):
        off = bias_off[half]
        in_head = (lane >= half * HD_MOBA) & (lane < (half + 1) * HD_MOBA)
        qh = jnp.where(in_head, q, jnp.zeros_like(q))
        gate = _dot_nt(qh, embed(km_hi, off)) + _dot_nt(qh, embed(km_lo, off))
        idx = (lane - off).astype(F32)
        in_range = (idx >= 0.0) & (idx < float(nblk))
        picked = _select_topk(gate, idx, in_range & (idx < t.astype(F32)), axis=1)
        bias = jnp.where(in_range & jnp.logical_not(picked), MASK_BIAS, 0.0)
        q_aug = qh + bias.astype(BF16)
        ka = ka_sc.at[half]

        s_own = _dot_nt(qh, ka[own, :])
        s_own = jnp.where(col <= row, s_own, MASK_BIAS)

        def score_pass(g, mx):
            rows = pl.ds(pl.multiple_of(g * (group * blk), group * blk), group * blk)
            s = _dot_nt(q_aug, ka[rows, :])
            s_sc[g] = s
            for i in range(group):
                mx = jnp.maximum(mx, s[:, i * blk:(i + 1) * blk])
            return mx

        mx = lax.fori_loop(0, n_groups, score_pass, s_own)
        m = jnp.max(mx, axis=-1, keepdims=True)
        p_own = jnp.exp(s_own - m)

        def value_pass(g, carry):
            l_acc, acc = carry
            rows = pl.ds(pl.multiple_of(g * (group * blk), group * blk), group * blk)
            p = jnp.exp(s_sc[g] - m)
            for i in range(group):
                l_acc = l_acc + p[:, i * blk:(i + 1) * blk]
            return l_acc, acc + _dot(p.astype(BF16), v_ref[rows, :])

        l_acc, acc = lax.fori_loop(0, n_groups, value_pass, (p_own, _dot(p_own.astype(BF16), v_own)))
        outs.append((in_head, acc / jnp.sum(l_acc, axis=-1, keepdims=True)))

    o = jnp.zeros((tq, LANES), F32)
    for in_head, oh in outs:
        o = jnp.where(in_head, oh, o)
    o_ref[...] = o.astype(o_ref.dtype)


def _moba_prompt(qm, kb, vb, ksum, n_seq, seq_len):
    assert seq_len % MOBA_BLOCK == 0
    nblk = seq_len // MOBA_BLOCK
    assert nblk <= HD_MOBA
    group = MOBA_GROUP if nblk % MOBA_GROUP == 0 else 1
    tq = MOBA_BLOCK
    n = n_seq * seq_len
    n_pair = MOBA_W // LANES
    seq_block = pl.BlockSpec((seq_len, LANES), lambda b, hp, t: (b, hp))
    return pl.pallas_call(
        _moba_prompt_kernel,
        grid=(n_seq, n_pair, nblk),
        in_specs=[
            pl.BlockSpec((tq, LANES), lambda b, hp, t: (b * nblk + t, hp)),
            seq_block, seq_block,
            pl.BlockSpec((nblk, LANES), lambda b, hp, t: (b, hp)),
        ],
        out_specs=pl.BlockSpec((tq, LANES), lambda b, hp, t: (b * nblk + t, hp)),
        out_shape=jax.ShapeDtypeStruct((n, MOBA_W), BF16),
        scratch_shapes=[pltpu.VMEM((LANES // HD_MOBA, seq_len, LANES), BF16),
                        pltpu.VMEM((nblk // group, tq, group * MOBA_BLOCK), F32)],
        compiler_params=_params(3),
        name="moba_prompt",
    )(qm, kb, vb, ksum)


def _moba_sample_kernel(pt_ref, q_ref, kn_ref, vn_ref, *rest, n_q):
    del pt_ref
    pps = PAGES_PER_STEP
    k_pages, v_pages = rest[:pps], rest[pps:2 * pps]
    o_ref, m_sc, l_sc, gate_sc, acc_sc = rest[2 * pps:]
    s_id = pl.program_id(1)
    n_step = pl.num_programs(1)
    nblk = m_sc.shape[1]
    n_hq = q_ref.shape[1]
    q = q_ref[0]
    pages_per_blk = MOBA_BLOCK // PAGE_SIZE
    page_rows = PAGE_SIZE * H_MOBA
    n_keys = MOBA_BLOCK * H_MOBA

    def same_head(n_cols):
        r = lax.broadcasted_iota(jnp.int32, (n_hq, n_cols), 0)
        c = lax.broadcasted_iota(jnp.int32, (n_hq, n_cols), 1)
        return (r // n_q) == (c % H_MOBA)

    key_mask = same_head(n_keys)
    blk_col = lax.broadcasted_iota(jnp.int32, (n_hq, nblk), 1)
    mean_rows = 2 * SUBLANES
    mean_mask = same_head(mean_rows) & (lax.broadcasted_iota(jnp.int32, (n_hq, mean_rows), 1) < H_MOBA)

    for i in range(pps // pages_per_blk):
        kp = [k_pages[i * pages_per_blk + j][0] for j in range(pages_per_blk)]
        vp = [v_pages[i * pages_per_blk + j][0] for j in range(pages_per_blk)]
        b_idx = s_id * (pps // pages_per_blk) + i
        ksum = kp[0].sum(axis=0)
        for p_ in kp[1:]:
            ksum = ksum + p_.sum(axis=0)
        kb = jnp.concatenate([p_.reshape(page_rows, HD_MOBA) for p_ in kp], axis=0).astype(BF16)
        vb = jnp.concatenate([p_.reshape(page_rows, HD_MOBA) for p_ in vp], axis=0).astype(BF16)
        s = jnp.where(key_mask, _dot_nt(q, kb), NEG_INF)
        m = jnp.max(s, axis=-1, keepdims=True)
        p = jnp.exp(s - m)
        l = jnp.sum(p, axis=-1, keepdims=True)
        acc_sc[b_idx] = _dot(p.astype(BF16), vb)
        kmean = jnp.concatenate([ksum * (1.0 / MOBA_BLOCK), jnp.zeros((mean_rows - H_MOBA, HD_MOBA), F32)], axis=0)
        km_hi = kmean.astype(BF16)
        km_lo = (kmean - km_hi.astype(F32)).astype(BF16)
        g = _dot_nt(q, km_hi) + _dot_nt(q, km_lo)
        gate = jnp.sum(jnp.where(mean_mask, g, 0.0), axis=-1, keepdims=True)
        m_sc[...] = jnp.where(blk_col == b_idx, m, m_sc[...])
        l_sc[...] = jnp.where(blk_col == b_idx, l, l_sc[...])
        gate_sc[...] = jnp.where(blk_col == b_idx, gate, gate_sc[...])

    @pl.when(s_id == n_step - 1)
    def _():
        picked = _select_topk(gate_sc[...], blk_col.astype(F32), blk_col >= 0, axis=1)
        m_all = jnp.where(picked, m_sc[...], NEG_INF)
        kn = kn_ref[0].astype(BF16)
        vn = vn_ref[0].astype(BF16)
        n_new = kn.shape[0]
        key_tok = lax.broadcasted_iota(jnp.int32, (n_hq, n_new), 1) // H_MOBA
        q_tok = lax.broadcasted_iota(jnp.int32, (n_hq, n_new), 0) % n_q
        s_own = jnp.where(same_head(n_new) & (key_tok <= q_tok), _dot_nt(q, kn), NEG_INF)
        m_tot = jnp.maximum(jnp.max(s_own, axis=-1, keepdims=True),
                            jnp.max(m_all, axis=-1, keepdims=True))
        p_own = jnp.exp(s_own - m_tot)
        c = jnp.exp(m_all - m_tot)
        l_tot = jnp.sum(p_own, axis=-1, keepdims=True) + jnp.sum(c * l_sc[...], axis=-1, keepdims=True)
        out = _dot(p_own.astype(BF16), vn)
        for b in range(nblk):
            out = out + c[:, b:b + 1] * acc_sc[b]
        o_ref[0] = out / l_tot


def _moba_sample(q, k_new, v_new, cache_k, cache_v, page_table):
    n_seq, n_q, _ = q.shape
    n_pages = page_table.shape[1]
    past = n_pages * PAGE_SIZE
    assert past % MOBA_BLOCK == 0 and n_pages % PAGES_PER_STEP == 0
    assert n_q <= MOBA_BLOCK and n_q % SUBLANES == 0 and past // MOBA_BLOCK >= MOBA_TOPK
    nblk = past // MOBA_BLOCK
    n_hq = H_MOBA * n_q
    qh = jnp.transpose(q.reshape(n_seq, n_q, H_MOBA, HD_MOBA), (0, 2, 1, 3)).reshape(n_seq, n_hq, HD_MOBA)
    qh = (qh * (HD_MOBA ** -0.5)).astype(BF16)
    kn = k_new.reshape(n_seq, n_q * H_MOBA, HD_MOBA)
    vn = v_new.reshape(n_seq, n_q * H_MOBA, HD_MOBA)

    def page_spec(i):
        return pl.BlockSpec((1, PAGE_SIZE, H_MOBA, HD_MOBA),
                            lambda b, s, pt: (pt[b, s * PAGES_PER_STEP + i], 0, 0, 0))

    seq3 = pl.BlockSpec((1, n_hq, HD_MOBA), lambda b, s, pt: (b, 0, 0))
    grid_spec = pltpu.PrefetchScalarGridSpec(
        num_scalar_prefetch=1,
        grid=(n_seq, n_pages // PAGES_PER_STEP),
        in_specs=[seq3, seq3, seq3] + [page_spec(i) for i in range(PAGES_PER_STEP)] * 2,
        out_specs=seq3,
        scratch_shapes=[
            pltpu.VMEM((n_hq, nblk), F32),
            pltpu.VMEM((n_hq, nblk), F32),
            pltpu.VMEM((n_hq, nblk), F32),
            pltpu.VMEM((nblk, n_hq, HD_MOBA), F32),
        ],
    )
    out = pl.pallas_call(
        functools.partial(_moba_sample_kernel, n_q=n_q),
        grid_spec=grid_spec,
        out_shape=jax.ShapeDtypeStruct((n_seq, n_hq, HD_MOBA), F32),
        compiler_params=_params(2),
        name="moba_sample",
    )(page_table, qh, kn, vn, *([cache_k] * PAGES_PER_STEP), *([cache_v] * PAGES_PER_STEP))
    return jnp.transpose(out.reshape(n_seq, H_MOBA, n_q, HD_MOBA), (0, 2, 1, 3)).reshape(n_seq, n_q, MOBA_W)


def _cumsum_rows(x):
    n = x.shape[0]
    row = lax.broadcasted_iota(jnp.int32, x.shape, 0)
    s = 1
    while s < n:
        x = x + jnp.where(row >= s, pltpu.roll(x, s, axis=0), 0.0)
        s *= 2
    return x


def _gla_chunk(q, k, v, la, st, cast):
    c = q.shape[0]
    sub = min(GLA_SUB, c)
    lane = lax.broadcasted_iota(jnp.int32, (1, GLA_KW), 1)
    head_masks = [(lane >= h * DK_GLA) & (lane < (h + 1) * DK_GLA) for h in range(H_GLA)]
    g = _cumsum_rows(la)
    krow = lax.broadcasted_iota(jnp.int32, (c, GLA_KW), 0)
    a_row = lax.broadcasted_iota(jnp.int32, (H_GLA * sub, c), 0)
    a_col = lax.broadcasted_iota(jnp.int32, (H_GLA * sub, c), 1)

    a_parts = []
    for i in range(c // sub):
        g_ref = jnp.zeros((1, GLA_KW), F32) if i == 0 else g[i * sub - 1:i * sub]
        qt = q[i * sub:(i + 1) * sub] * jnp.exp(g[i * sub:(i + 1) * sub] - g_ref)
        kt = jnp.where(krow < (i + 1) * sub, k * jnp.exp(g_ref - g), 0.0)
        q_stack = jnp.concatenate([jnp.where(hm, qt, 0.0) for hm in head_masks], axis=0)
        a = _dot_nt(cast(q_stack), cast(kt))
        a_parts.append(jnp.where(a_col <= i * sub + a_row % sub, a, 0.0))

    qe = q * jnp.exp(g)
    g_last = g[c - 1:c]
    k_dec = cast(k * jnp.exp(g_last - g))
    st_c = cast(st)
    outs = []
    st_new = st * jnp.exp(g_last)
    for h in range(H_GLA):
        a_h = jnp.concatenate([a[h * sub:(h + 1) * sub] for a in a_parts], axis=0)
        v_h = cast(v[:, h * DV_GLA:(h + 1) * DV_GLA])
        o_intra = _dot(cast(a_h), v_h)
        o_inter = _dot_nt(cast(jnp.where(head_masks[h], qe, 0.0)), st_c)
        outs.append(o_intra + o_inter)
        st_new = st_new + jnp.where(head_masks[h], _dot_tn(v_h, k_dec), 0.0)
    return outs, st_new


def _gla_kernel(q_ref, k_ref, v_ref, la_ref, r_ref, s0_ref, nw_ref, o_ref, sfin_ref, st_sc, *, chunk):
    t = pl.program_id(1)

    @pl.when(t == 0)
    def _():
        st_sc[...] = s0_ref[0]

    cast = (lambda a: a.astype(BF16)) if chunk >= 2 * SUBLANES else (lambda a: a)
    st = st_sc[...]
    for c in range(q_ref.shape[0] // chunk):
        rows = slice(c * chunk, (c + 1) * chunk)
        q = q_ref[rows, :].astype(F32) * (DK_GLA ** -0.5)
        k = k_ref[rows, :].astype(F32)
        outs, st = _gla_chunk(q, k, v_ref[rows, :], la_ref[rows, :], st, cast)
        for h, o in enumerate(outs):
            lanes = slice(h * DV_GLA, (h + 1) * DV_GLA)
            r = r_ref[rows, lanes]
            o = o * lax.rsqrt(jnp.mean(o * o, axis=-1, keepdims=True) + EPS)
            o_ref[rows, lanes] = (o * nw_ref[:, lanes] * (r * _sigmoid(r))).astype(o_ref.dtype)
    st_sc[...] = st

    @pl.when(t == pl.num_programs(1) - 1)
    def _():
        sfin_ref[0] = st


def _gla(qg, kg, vg, la, rg, state_t, nw, n_seq, seq_len, out_dtype):
    chunk = math.gcd(seq_len, GLA_CHUNK)
    tl = min(seq_len, ROW_TILE)
    assert seq_len % tl == 0 and tl % chunk == 0
    nt = seq_len // tl
    n = n_seq * seq_len
    row = lambda w: pl.BlockSpec((tl, w), lambda b, t: (b * nt + t, 0))
    st_spec = pl.BlockSpec((1, DV_GLA, GLA_KW), lambda b, t: (b, 0, 0))
    return pl.pallas_call(
        functools.partial(_gla_kernel, chunk=chunk),
        grid=(n_seq, nt),
        in_specs=[row(GLA_KW), row(GLA_KW), row(GLA_VW), row(GLA_KW), row(GLA_VW), st_spec,
                  _const_spec((1, GLA_VW))],
        out_specs=[row(GLA_VW), st_spec],
        out_shape=[jax.ShapeDtypeStruct((n, GLA_VW), out_dtype),
                   jax.ShapeDtypeStruct((n_seq, DV_GLA, GLA_KW), F32)],
        scratch_shapes=[pltpu.VMEM((DV_GLA, GLA_KW), F32)],
        compiler_params=_params(2),
        name="gla",
    )(qg, kg, vg, la, rg, state_t, nw)


def _state_to_t(s):
    n = s.shape[0]
    return jnp.transpose(s, (0, 3, 1, 2)).reshape(n, DV_GLA, GLA_KW)


def _state_from_t(st):
    n = st.shape[0]
    return jnp.transpose(st.reshape(n, DV_GLA, H_GLA, DK_GLA), (0, 2, 3, 1))


def _cross_kernel(q_ref, mk_ref, mv_ref, o_ref):
    scale = HD_CROSS ** -0.5
    small = q_ref.shape[0] < 2 * SUBLANES
    cast = (lambda a: a) if small else (lambda a: a.astype(BF16))
    for h in range(H_CROSS):
        lanes = slice(h * HD_CROSS, (h + 1) * HD_CROSS)
        s = _dot_nt(cast(q_ref[:, lanes]), cast(mk_ref[:, lanes])) * scale
        m = jnp.max(s, axis=-1, keepdims=True)
        p = jnp.exp(s - m)
        l = jnp.sum(p, axis=-1, keepdims=True)
        o_ref[:, lanes] = (_dot(cast(p), cast(mv_ref[:, lanes])) / l).astype(o_ref.dtype)


def _cross(qc, mk, mv, n_seq, seq_len, n_mem, out_dtype):
    tl = min(seq_len, ROW_TILE)
    nt = seq_len // tl
    row = pl.BlockSpec((tl, CROSS_W), lambda b, t: (b * nt + t, 0))
    mem = pl.BlockSpec((n_mem, CROSS_W), lambda b, t: (b, 0))
    return pl.pallas_call(
        _cross_kernel,
        grid=(n_seq, nt),
        in_specs=[row, mem, mem],
        out_specs=row,
        out_shape=jax.ShapeDtypeStruct((n_seq * seq_len, CROSS_W), out_dtype),
        compiler_params=_params(2),
        name="cross",
    )(qc, mk, mv)


def _merge_kernel(x_ref, om_ref, og_ref, oc_ref, nw_ref, wg_ref, bg_ref, wbm_ref, wbg_ref, wbc_ref,
                  wo_ref, h_ref):
    x = x_ref[...]
    d = x.shape[1]
    xb = _rms(x, nw_ref[...]).astype(BF16)
    merged = jnp.zeros(x.shape, F32)
    for i, (o_ref, wb_ref) in enumerate(((om_ref, wbm_ref), (og_ref, wbg_ref), (oc_ref, wbc_ref))):
        cols = slice(i * d, (i + 1) * d)
        gate = _sigmoid(_dot(xb, wg_ref[:, cols]) + bg_ref[:, cols])
        merged = merged + gate * _dot(o_ref[...].astype(BF16), wb_ref[...])
    h_ref[...] = x + _dot(merged.astype(BF16), wo_ref[...])


def _merge(x2d, om, og, oc, nw, wg, bg, wbm, wbg, wbc, wo):
    n, d = x2d.shape
    tm = ROW_TILE
    row = lambda w: pl.BlockSpec((tm, w), lambda i: (i, 0))
    return pl.pallas_call(
        _merge_kernel,
        grid=(n // tm,),
        in_specs=[row(d), row(MOBA_W), row(GLA_VW), row(CROSS_W), _const_spec((1, d)),
                  _const_spec(wg.shape), _const_spec(bg.shape), _const_spec(wbm.shape),
                  _const_spec(wbg.shape), _const_spec(wbc.shape), _const_spec(wo.shape)],
        out_specs=row(d),
        out_shape=jax.ShapeDtypeStruct((n, d), F32),
        compiler_params=_params(1),
        name="merge",
    )(x2d, om, og, oc, nw, wg, bg, wbm, wbg, wbc, wo)


def _gelu_tanh(x):
    return 0.5 * x * (1.0 + jnp.tanh(0.7978845608028654 * (x + 0.044715 * (x * x * x))))


def _ffn_body(h_ref, nw_ref, wup_ref, wc_ref, bc_ref, wdn_ref, nf_ref, y_ref, prev_rows, store_u):
    h = h_ref[...]
    hb = _rms(h, nw_ref[...]).astype(BF16)
    d_ff = wdn_ref.shape[0]
    n_col_chunks = 2
    fc = d_ff // n_col_chunks
    assert fc % LANES == 0
    acc = jnp.zeros(h.shape, F32)
    for c in range(n_col_chunks):
        cols = slice(c * fc, (c + 1) * fc)
        u = _dot(hb, wup_ref[:, cols])
        gate = _dot(hb, wup_ref[:, d_ff + c * fc:d_ff + (c + 1) * fc])
        u1, u2 = prev_rows(cols, u)
        store_u(cols, u)
        conv = bc_ref[:, cols] + wc_ref[0:1, cols] * u2 + wc_ref[1:2, cols] * u1 + wc_ref[2:3, cols] * u
        act = (_gelu_tanh(conv) * gate).astype(BF16)
        acc = acc + _dot(act, wdn_ref[cols, :])
    y_ref[...] = _rms(h + acc, nf_ref[...])


def _ffn_long_kernel(h_ref, nw_ref, wup_ref, wc_ref, bc_ref, wdn_ref, nf_ref, y_ref, cs_ref, carry_sc):
    tm = h_ref.shape[0]

    @pl.when(pl.program_id(1) == 0)
    def _():
        carry_sc[...] = jnp.zeros(carry_sc.shape, F32)

    row = lax.broadcasted_iota(jnp.int32, (tm, 1), 0)

    def prev_rows(cols, u):
        last = carry_sc[SUBLANES - 1:SUBLANES, cols]
        last2 = carry_sc[SUBLANES - 2:SUBLANES - 1, cols]
        u1 = jnp.where(row == 0, last, pltpu.roll(u, 1, axis=0))
        u2 = jnp.where(row == 0, last2, jnp.where(row == 1, last, pltpu.roll(u, 2, axis=0)))
        return u1, u2

    def store_u(cols, u):
        carry_sc[:, cols] = u[tm - SUBLANES:tm]
        cs_ref[0, :, cols] = u[tm - (CONV_W - 1):tm]

    _ffn_body(h_ref, nw_ref, wup_ref, wc_ref, bc_ref, wdn_ref, nf_ref, y_ref, prev_rows, store_u)


def _ffn_short_kernel(h_ref, p1_ref, p2_ref, nw_ref, wup_ref, wc_ref, bc_ref, wdn_ref, nf_ref,
                      y_ref, u_ref, *, seq_len):
    tm = h_ref.shape[0]
    pos = lax.broadcasted_iota(jnp.int32, (tm, 1), 0) % seq_len

    def prev_rows(cols, u):
        u1 = jnp.where(pos >= 1, pltpu.roll(u, 1, axis=0), p1_ref[:, cols])
        u2 = jnp.where(pos >= 2, pltpu.roll(u, 2, axis=0), p2_ref[:, cols])
        return u1, u2

    def store_u(cols, u):
        u_ref[:, cols] = u

    _ffn_body(h_ref, nw_ref, wup_ref, wc_ref, bc_ref, wdn_ref, nf_ref, y_ref, prev_rows, store_u)


def _ffn_weights_specs(d, wup, wc, bc, wdn):
    return [_const_spec((1, d)), _const_spec(wup.shape), _const_spec(wc.shape), _const_spec(bc.shape),
            _const_spec(wdn.shape), _const_spec((1, d))]


def _ffn_long(h2d, n_seq, seq_len, nw, wup, wc, bc, wdn, nf):
    n, d = h2d.shape
    d_ff = wdn.shape[0]
    tm = ROW_TILE
    assert seq_len % tm == 0
    nt = seq_len // tm
    row = pl.BlockSpec((tm, d), lambda b, t: (b * nt + t, 0))
    return pl.pallas_call(
        _ffn_long_kernel,
        grid=(n_seq, nt),
        in_specs=[row] + _ffn_weights_specs(d, wup, wc, bc, wdn),
        out_specs=[row, pl.BlockSpec((1, CONV_W - 1, d_ff), lambda b, t: (b, 0, 0))],
        out_shape=[jax.ShapeDtypeStruct((n, d), F32),
                   jax.ShapeDtypeStruct((n_seq, CONV_W - 1, d_ff), F32)],
        scratch_shapes=[pltpu.VMEM((SUBLANES, d_ff), F32)],
        compiler_params=_params(2),
        name="ffn_long",
    )(h2d, nw, wup, wc, bc, wdn, nf)


def _ffn_short(h2d, conv_prev, n_seq, seq_len, nw, wup, wc, bc, wdn, nf):
    n, d = h2d.shape
    d_ff = wdn.shape[0]
    tm = ROW_TILE
    assert tm % seq_len == 0 and n % tm == 0 and seq_len >= CONV_W - 1
    zeros = jnp.zeros((n_seq, seq_len, d_ff), F32)
    p1 = zeros.at[:, 0].set(conv_prev[:, 1]).reshape(n, d_ff)
    p2 = zeros.at[:, 0].set(conv_prev[:, 0]).at[:, 1].set(conv_prev[:, 1]).reshape(n, d_ff)
    row = lambda w: pl.BlockSpec((tm, w), lambda i: (i, 0))
    y, u = pl.pallas_call(
        functools.partial(_ffn_short_kernel, seq_len=seq_len),
        grid=(n // tm,),
        in_specs=[row(d), row(d_ff), row(d_ff)] + _ffn_weights_specs(d, wup, wc, bc, wdn),
        out_specs=[row(d), row(d_ff)],
        out_shape=[jax.ShapeDtypeStruct((n, d), F32), jax.ShapeDtypeStruct((n, d_ff), F32)],
        compiler_params=_params(1),
        name="ffn_short",
    )(h2d, p1, p2, nw, wup, wc, bc, wdn, nf)
    return y, u.reshape(n_seq, seq_len, d_ff)[:, seq_len - (CONV_W - 1):]


def _prep_weights(norm_mix, w_in, w_gla_a2, b_gla_a, norm_gla, norm_mem, w_mem_kv, w_br_moba, w_br_gla,
                  w_br_cross, w_gate, b_gate, w_out, norm_ffn, w_up, w_conv, b_conv, w_down, norm_final):
    d = w_in.shape[0]
    o_a = C_QC
    w_in_p = jnp.concatenate(
        [w_in[:, :o_a], w_in[:, o_a + GLA_RANK:], w_in[:, o_a:o_a + GLA_RANK],
         jnp.zeros((d, A_PAD - GLA_RANK), w_in.dtype)], axis=1).astype(BF16)
    wa2_p = jnp.concatenate([w_gla_a2, jnp.zeros((A_PAD - GLA_RANK, GLA_KW), w_gla_a2.dtype)],
                            axis=0).astype(BF16)
    r2 = lambda a: a.reshape(1, -1)
    return dict(
        norm_mix=r2(norm_mix), w_in=w_in_p, wa2=wa2_p, ba=r2(b_gla_a), norm_gla=r2(norm_gla),
        norm_mem=r2(norm_mem), w_mem_kv=w_mem_kv.astype(BF16), wbm=w_br_moba.astype(BF16),
        wbg=w_br_gla.astype(BF16), wbc=w_br_cross.astype(BF16), wg=w_gate.astype(BF16), bg=r2(b_gate),
        wo=w_out.astype(BF16), norm_ffn=r2(norm_ffn), wup=w_up.astype(BF16), wc=w_conv, bc=r2(b_conv),
        wdn=w_down.astype(BF16), norm_final=r2(norm_final))


def kernel(x_prompt, x_sample, cache_moba_k, cache_moba_v, state_gla, state_conv, cache_mem_k, cache_mem_v, page_table, mem_prompt, norm_mix, w_in, w_gla_a2, b_gla_a, norm_gla, norm_mem, w_mem_kv, w_br_moba, w_br_gla, w_br_cross, w_gate, b_gate, w_out, norm_ffn, w_up, w_conv, b_conv, w_down, norm_final):
    depth = w_in.shape[0]
    assert depth == 1
    bp, sp, d = x_prompt.shape
    bs, ss, _ = x_sample.shape
    n_mem = mem_prompt.shape[1]
    w = _prep_weights(norm_mix[0], w_in[0], w_gla_a2[0], b_gla_a[0], norm_gla[0], norm_mem[0], w_mem_kv[0],
                      w_br_moba[0], w_br_gla[0], w_br_cross[0], w_gate[0], b_gate[0], w_out[0], norm_ffn[0],
                      w_up[0], w_conv[0], b_conv[0], w_down[0], norm_final)

    def mix(x2d, o_m, o_g, o_c):
        return _merge(x2d, o_m, o_g, o_c, w["norm_mix"], w["wg"], w["bg"], w["wbm"], w["wbg"], w["wbc"], w["wo"])

    ffn_w = (w["norm_ffn"], w["wup"], w["wc"], w["bc"], w["wdn"], w["norm_final"])

    xp = x_prompt.reshape(bp * sp, d)
    qm, k_p, v_p, qg, kg, vg, rg, qc, la, ksum, kb, vb = _project(
        xp, w["norm_mix"], w["w_in"], w["wa2"], w["ba"], BF16)
    o_m = _moba_prompt(qm, kb, vb, ksum.reshape(-1, MOBA_W), bp, sp)
    mk_p, mv_p = _memory_kv(mem_prompt.reshape(bp * n_mem, d), w["norm_mem"], w["w_mem_kv"])
    o_g, gla_p = _gla(qg, kg, vg, la, rg, jnp.zeros((bp, DV_GLA, GLA_KW), F32), w["norm_gla"], bp, sp, BF16)
    o_c = _cross(qc, mk_p, mv_p, bp, sp, n_mem, BF16)
    h_p = mix(xp, o_m, o_g, o_c)
    y_p, conv_p = _ffn_long(h_p, bp, sp, *ffn_w)

    xs = x_sample.reshape(bs * ss, d)
    qm, k_s, v_s, qg, kg, vg, rg, qc, la, _, _, _ = _project(
        xs, w["norm_mix"], w["w_in"], w["wa2"], w["ba"], F32)
    r3 = lambda a: a.reshape(bs, ss, MOBA_W)
    o_m = _moba_sample(r3(qm), r3(k_s), r3(v_s), cache_moba_k[0], cache_moba_v[0],
                       page_table).reshape(bs * ss, MOBA_W)
    o_g, gla_s = _gla(qg, kg, vg, la, rg, _state_to_t(state_gla[0]), w["norm_gla"], bs, ss, F32)
    o_c = _cross(qc, cache_mem_k[0].reshape(bs * n_mem, CROSS_W), cache_mem_v[0].reshape(bs * n_mem, CROSS_W),
                 bs, ss, n_mem, F32)
    h_s = mix(xs, o_m, o_g, o_c)
    y_s, conv_s = _ffn_short(h_s, state_conv[0], bs, ss, *ffn_w)

    kv5 = lambda a, b, s: a.reshape(1, b, s, H_MOBA, HD_MOBA)
    return (y_p.reshape(bp, sp, d), y_s.reshape(bs, ss, d),
            kv5(k_p, bp, sp), kv5(v_p, bp, sp), kv5(k_s, bs, ss), kv5(v_s, bs, ss),
            _state_from_t(gla_p)[None], _state_from_t(gla_s)[None],
            conv_p[None], conv_s[None],
            mk_p.reshape(1, bp, n_mem, H_CROSS, HD_CROSS), mv_p.reshape(1, bp, n_mem, H_CROSS, HD_CROSS))
```

```python
import functools
import math

import jax
import jax.numpy as jnp
from jax import lax
from jax.experimental import pallas as pl
from jax.experimental.pallas import tpu as pltpu

F32 = jnp.float32
BF16 = jnp.bfloat16
NEG_INF = float("-inf")

EPS = 1e-6
H_MOBA, HD_MOBA = 8, 64
MOBA_W = H_MOBA * HD_MOBA
MOBA_BLOCK = 256
MOBA_TOPK = 3
PAGE_SIZE = 128
H_GLA, DK_GLA, DV_GLA = 4, 64, 128
GLA_KW, GLA_VW = H_GLA * DK_GLA, H_GLA * DV_GLA
GLA_RANK = 16
GLA_TAU = 16.0
GLA_CHUNK = 64
GLA_SUB = 16
H_CROSS, HD_CROSS = 4, 128
CROSS_W = H_CROSS * HD_CROSS
N_BRANCH = 3
CONV_W = 3

LANES = 128
SUBLANES = 8
VMEM_LIMIT_BYTES = 56 * 1024 * 1024

MASK_BIAS = -1e30
MOBA_GROUP = 4
ROW_TILE = 256
PAGES_PER_STEP = 8
A_PAD = LANES

C_QM, C_K, C_V = 0, MOBA_W, 2 * MOBA_W
C_QG = 3 * MOBA_W
C_KG = C_QG + GLA_KW
C_VG = C_KG + GLA_KW
C_RG = C_VG + GLA_VW
C_QC = C_RG + GLA_VW
C_A = C_QC + CROSS_W
IN_COLS_PAD = C_A + A_PAD


def _params(n_axes):
    return pltpu.CompilerParams(
        dimension_semantics=("arbitrary",) * n_axes,
        vmem_limit_bytes=VMEM_LIMIT_BYTES,
    )


def _const_spec(shape):
    nd = len(shape)
    return pl.BlockSpec(shape, lambda *_: (0,) * nd, pipeline_mode=pl.Buffered(1))


def _rms(x, w):
    return x * lax.rsqrt(jnp.mean(x * x, axis=-1, keepdims=True) + EPS) * w


def _sigmoid(x):
    return 1.0 / (1.0 + jnp.exp(-x))


def _dot(a, b):
    return jnp.dot(a, b, preferred_element_type=F32)


def _dot_nt(a, b):
    return lax.dot_general(a, b, (((1,), (1,)), ((), ())), preferred_element_type=F32)


def _dot_tn(a, b):
    return lax.dot_general(a, b, (((0,), (0,)), ((), ())), preferred_element_type=F32)


def _proj_kernel(x_ref, nw_ref, w_ref, wa2_ref, ba_ref,
                 qm_ref, k_ref, v_ref, qg_ref, kg_ref, vg_ref, rg_ref, qc_ref, la_ref, ksum_ref,
                 kb_ref, vb_ref):
    xb = _rms(x_ref[...], nw_ref[...]).astype(BF16)

    def mm(lo, hi):
        return _dot(xb, w_ref[:, lo:hi])

    qm_ref[...] = mm(C_QM, C_K).astype(qm_ref.dtype)
    k = mm(C_K, C_V)
    k_ref[...] = k
    kb_ref[...] = k.astype(BF16)
    for g in range(k.shape[0] // MOBA_BLOCK):
        ksum_ref[g] = jnp.sum(k[g * MOBA_BLOCK:(g + 1) * MOBA_BLOCK], axis=0, keepdims=True)
    v = mm(C_V, C_QG)
    v_ref[...] = v
    vb_ref[...] = v.astype(BF16)
    qg_ref[...] = mm(C_QG, C_KG).astype(qg_ref.dtype)
    kg_ref[...] = mm(C_KG, C_VG).astype(kg_ref.dtype)
    vg_ref[...] = mm(C_VG, C_RG).astype(vg_ref.dtype)
    rg_ref[...] = mm(C_RG, C_QC)
    qc_ref[...] = mm(C_QC, C_A).astype(qc_ref.dtype)
    z = _dot(mm(C_A, IN_COLS_PAD).astype(BF16), wa2_ref[...]) + ba_ref[...]
    la_ref[...] = (jnp.minimum(z, 0.0) - jnp.log1p(jnp.exp(-jnp.abs(z)))) * (1.0 / GLA_TAU)


def _project(x2d, nw, w_in_p, wa2_p, ba, act_dtype):
    n, d = x2d.shape
    assert n % ROW_TILE == 0
    tm = ROW_TILE
    row = lambda w: pl.BlockSpec((tm, w), lambda i: (i, 0))
    widths = (MOBA_W, MOBA_W, MOBA_W, GLA_KW, GLA_KW, GLA_VW, GLA_VW, CROSS_W, GLA_KW)
    dtypes = (act_dtype, F32, F32, act_dtype, act_dtype, act_dtype, F32, act_dtype, F32)
    out_shape = [jax.ShapeDtypeStruct((n, w), dt) for w, dt in zip(widths, dtypes)]
    out_shape.append(jax.ShapeDtypeStruct((n // MOBA_BLOCK, 1, MOBA_W), F32))
    out_shape += [jax.ShapeDtypeStruct((n, MOBA_W), BF16)] * 2
    out_specs = [row(w) for w in widths]
    out_specs.append(pl.BlockSpec((tm // MOBA_BLOCK, 1, MOBA_W), lambda i: (i, 0, 0)))
    out_specs += [row(MOBA_W)] * 2
    return pl.pallas_call(
        _proj_kernel,
        grid=(n // tm,),
        in_specs=[row(d), _const_spec((1, d)), _const_spec(w_in_p.shape),
                  _const_spec(wa2_p.shape), _const_spec((1, GLA_KW))],
        out_specs=out_specs,
        out_shape=out_shape,
        compiler_params=_params(1),
        name="proj",
    )(x2d, nw, w_in_p, wa2_p, ba)


def _memkv_kernel(m_ref, nw_ref, w_ref, mk_ref, mv_ref):
    mb = _rms(m_ref[...], nw_ref[...]).astype(BF16)
    mk_ref[...] = _dot(mb, w_ref[:, :CROSS_W])
    mv_ref[...] = _dot(mb, w_ref[:, CROSS_W:])


def _memory_kv(mem2d, nw, w_kv):
    n, d = mem2d.shape
    tm = ROW_TILE
    assert n % tm == 0
    return pl.pallas_call(
        _memkv_kernel,
        grid=(n // tm,),
        in_specs=[pl.BlockSpec((tm, d), lambda i: (i, 0)), _const_spec((1, d)), _const_spec(w_kv.shape)],
        out_specs=[pl.BlockSpec((tm, CROSS_W), lambda i: (i, 0))] * 2,
        out_shape=[jax.ShapeDtypeStruct((n, CROSS_W), F32)] * 2,
        compiler_params=_params(1),
        name="memkv",
    )(mem2d, nw, w_kv)


def _select_topk(gate, idx, valid, axis):
    g = jnp.where(valid, gate, NEG_INF)
    picked = jnp.zeros(gate.shape, jnp.bool_)
    for _ in range(MOBA_TOPK):
        m = jnp.max(g, axis=axis, keepdims=True)
        first = jnp.min(jnp.where(g == m, idx, float(2 ** 24)), axis=axis, keepdims=True)
        pick = (idx == first) & (m > NEG_INF) & valid
        picked = picked | pick
        g = jnp.where(pick, NEG_INF, g)
    return picked


def _moba_prompt_kernel(q_ref, k_ref, v_ref, ksum_ref, o_ref, ka_sc, s_sc):
    t = pl.program_id(2)
    tq = q_ref.shape[0]
    blk = MOBA_BLOCK
    nblk = ksum_ref.shape[0]
    seq_len = k_ref.shape[0]
    group = s_sc.shape[2] // blk
    n_heads = LANES // HD_MOBA
    bias_off = [((half + 1) % n_heads) * HD_MOBA for half in range(n_heads)]

    @pl.when(t == 0)
    def _():
        key_blk = lax.broadcasted_iota(jnp.int32, (seq_len, LANES), 0) // blk
        key_lane = lax.broadcasted_iota(jnp.int32, (seq_len, LANES), 1)
        k = k_ref[...]
        for half in range(n_heads):
            in_head = (key_lane >= half * HD_MOBA) & (key_lane < (half + 1) * HD_MOBA)
            onehot = jnp.where(key_lane - bias_off[half] == key_blk, 1.0, 0.0).astype(BF16)
            ka_sc[half] = jnp.where(in_head, k, onehot)

    q = q_ref[...] * (HD_MOBA ** -0.5)
    kmean = ksum_ref[...] * (1.0 / blk)
    km_hi = kmean.astype(BF16)
    km_lo = (kmean - km_hi.astype(F32)).astype(BF16)
    lane = lax.broadcasted_iota(jnp.int32, (1, LANES), 1)
    row = lax.broadcasted_iota(jnp.int32, (tq, blk), 0)
    col = lax.broadcasted_iota(jnp.int32, (tq, blk), 1)
    own = pl.ds(pl.multiple_of(t * blk, blk), blk)
    v_own = v_ref[own, :]
    n_groups = (t + group - 1) // group

    def embed(km, off):
        parts = [jnp.zeros((off, LANES), BF16)] if off else []
        parts.append(km)
        if LANES - off - nblk:
            parts.append(jnp.zeros((LANES - off - nblk, LANES), BF16))
        return jnp.concatenate(parts, axis=0)

    outs = []
    for half in range(n_heads):
        off = bias_off[half]
        in_head = (lane >= half * HD_MOBA) & (lane < (half + 1) * HD_MOBA)
        qh = jnp.where(in_head, q, jnp.zeros_like(q))
        gate = _dot_nt(qh, embed(km_hi, off)) + _dot_nt(qh, embed(km_lo, off))
        idx = (lane - off).astype(F32)
        in_range = (idx >= 0.0) & (idx < float(nblk))
        picked = _select_topk(gate, idx, in_range & (idx < t.astype(F32)), axis=1)
        bias = jnp.where(in_range & jnp.logical_not(picked), MASK_BIAS, 0.0)
        q_aug = qh + bias.astype(BF16)
        ka = ka_sc.at[half]

        s_own = _dot_nt(qh, ka[own, :])
        s_own = jnp.where(col <= row, s_own, MASK_BIAS)

        def score_pass(g, mx):
            rows = pl.ds(pl.multiple_of(g * (group * blk), group * blk), group * blk)
            s = _dot_nt(q_aug, ka[rows, :])
            s_sc[g] = s
            for i in range(group):
                mx = jnp.maximum(mx, s[:, i * blk:(i + 1) * blk])
            return mx

        mx = lax.fori_loop(0, n_groups, score_pass, s_own)
        m = jnp.max(mx, axis=-1, keepdims=True)
        p_own = jnp.exp(s_own - m)

        def value_pass(g, carry):
            l_acc, acc = carry
            rows = pl.ds(pl.multiple_of(g * (group * blk), group * blk), group * blk)
            p = jnp.exp(s_sc[g] - m)
            for i in range(group):
                l_acc = l_acc + p[:, i * blk:(i + 1) * blk]
            return l_acc, acc + _dot(p.astype(BF16), v_ref[rows, :])

        l_acc, acc = lax.fori_loop(0, n_groups, value_pass, (p_own, _dot(p_own.astype(BF16), v_own)))
        outs.append((in_head, acc / jnp.sum(l_acc, axis=-1, keepdims=True)))

    o = jnp.zeros((tq, LANES), F32)
    for in_head, oh in outs:
        o = jnp.where(in_head, oh, o)
    o_ref[...] = o.astype(o_ref.dtype)


def _moba_prompt(qm, kb, vb, ksum, n_seq, seq_len):
    assert seq_len % MOBA_BLOCK == 0
    nblk = seq_len // MOBA_BLOCK
    assert nblk <= HD_MOBA
    group = MOBA_GROUP if nblk % MOBA_GROUP == 0 else 1
    tq = MOBA_BLOCK
    n = n_seq * seq_len
    n_pair = MOBA_W // LANES
    seq_block = pl.BlockSpec((seq_len, LANES), lambda b, hp, t: (b, hp))
    return pl.pallas_call(
        _moba_prompt_kernel,
        grid=(n_seq, n_pair, nblk),
        in_specs=[
            pl.BlockSpec((tq, LANES), lambda b, hp, t: (b * nblk + t, hp)),
            seq_block, seq_block,
            pl.BlockSpec((nblk, LANES), lambda b, hp, t: (b, hp)),
        ],
        out_specs=pl.BlockSpec((tq, LANES), lambda b, hp, t: (b * nblk + t, hp)),
        out_shape=jax.ShapeDtypeStruct((n, MOBA_W), BF16),
        scratch_shapes=[pltpu.VMEM((LANES // HD_MOBA, seq_len, LANES), BF16),
                        pltpu.VMEM((nblk // group, tq, group * MOBA_BLOCK), F32)],
        compiler_params=_params(3),
        name="moba_prompt",
    )(qm, kb, vb, ksum)


def _moba_sample_kernel(pt_ref, qbd_ref, kn_ref, vn_ref, *rest, n_q):
    del pt_ref
    pps = PAGES_PER_STEP
    k_pages, v_pages = rest[:pps], rest[pps:2 * pps]
    o_ref, m_sc, l_sc, gate_sc, acc_sc = rest[2 * pps:]
    s_id = pl.program_id(1)
    n_step = pl.num_programs(1)
    nblk = m_sc.shape[1]
    n_hq = qbd_ref.shape[1]
    qbd = qbd_ref[0]
    hq_row = lax.broadcasted_iota(jnp.int32, (n_hq, MOBA_W), 0)
    w_lane = lax.broadcasted_iota(jnp.int32, (n_hq, MOBA_W), 1)
    head_mask = (w_lane // HD_MOBA) == (hq_row // n_q)
    blk_col = lax.broadcasted_iota(jnp.int32, (n_hq, nblk), 1)
    pages_per_blk = MOBA_BLOCK // PAGE_SIZE

    for i in range(pps // pages_per_blk):
        pages = range(i * pages_per_blk, (i + 1) * pages_per_blk)
        kt = jnp.concatenate([k_pages[j][0] for j in pages], axis=1).astype(BF16)
        vt = jnp.concatenate([v_pages[j][0] for j in pages], axis=1).astype(BF16)
        b_idx = s_id * (pps // pages_per_blk) + i
        s = _dot(qbd, kt)
        gate = jnp.sum(s, axis=-1, keepdims=True)
        m = jnp.max(s, axis=-1, keepdims=True)
        p = jnp.exp(s - m)
        l = jnp.sum(p, axis=-1, keepdims=True)
        acc_sc[b_idx] = jnp.where(head_mask, _dot_nt(p.astype(BF16), vt), 0.0)
        m_sc[...] = jnp.where(blk_col == b_idx, m, m_sc[...])
        l_sc[...] = jnp.where(blk_col == b_idx, l, l_sc[...])
        gate_sc[...] = jnp.where(blk_col == b_idx, gate, gate_sc[...])

    @pl.when(s_id == n_step - 1)
    def _():
        picked = _select_topk(gate_sc[...], blk_col.astype(F32), blk_col >= 0, axis=1)
        m_all = jnp.where(picked, m_sc[...], NEG_INF)
        kn = kn_ref[0].astype(BF16)
        vn = vn_ref[0].astype(BF16)
        n_new = kn.shape[0]
        key_i = lax.broadcasted_iota(jnp.int32, (n_hq, n_new), 1)
        q_i = lax.broadcasted_iota(jnp.int32, (n_hq, n_new), 0) % n_q
        s_own = jnp.where(key_i <= q_i, _dot_nt(qbd, kn), NEG_INF)
        m_tot = jnp.maximum(jnp.max(s_own, axis=-1, keepdims=True),
                            jnp.max(m_all, axis=-1, keepdims=True))
        p_own = jnp.exp(s_own - m_tot)
        c = jnp.exp(m_all - m_tot)
        l_tot = jnp.sum(p_own, axis=-1, keepdims=True) + jnp.sum(c * l_sc[...], axis=-1, keepdims=True)
        out = jnp.where(head_mask, _dot(p_own.astype(BF16), vn), 0.0)
        for b in range(nblk):
            out = out + c[:, b:b + 1] * acc_sc[b]
        out = out / l_tot
        o = out[0:n_q]
        for h in range(1, H_MOBA):
            o = o + out[h * n_q:(h + 1) * n_q]
        o_ref[0] = o


def _moba_sample(q, k_new, v_new, cache_k, cache_v, page_table):
    n_seq, n_q, _ = q.shape
    n_phys = cache_k.shape[0]
    n_pages = page_table.shape[1]
    past = n_pages * PAGE_SIZE
    assert past % MOBA_BLOCK == 0 and n_pages % PAGES_PER_STEP == 0
    assert n_q <= MOBA_BLOCK and n_q % SUBLANES == 0
    nblk = past // MOBA_BLOCK
    n_hq = H_MOBA * n_q
    lane_head = jnp.arange(MOBA_W) // HD_MOBA
    qbd = jnp.where(lane_head[None, None, None, :] == jnp.arange(H_MOBA)[None, :, None, None],
                    q[:, None, :, :] * (HD_MOBA ** -0.5), 0.0).reshape(n_seq, n_hq, MOBA_W).astype(BF16)
    n_new = max(2 * SUBLANES, n_q)
    pad = ((0, 0), (0, n_new - n_q), (0, 0))
    kn = jnp.pad(k_new, pad)
    vn = jnp.pad(v_new, pad)
    to_t = lambda c: jnp.transpose(c, (0, 2, 3, 1)).reshape(n_phys, MOBA_W, PAGE_SIZE)

    def page_spec(i):
        return pl.BlockSpec((1, MOBA_W, PAGE_SIZE), lambda b, s, pt: (pt[b, s * PAGES_PER_STEP + i], 0, 0))

    seq3 = lambda r: pl.BlockSpec((1, r, MOBA_W), lambda b, s, pt: (b, 0, 0))
    grid_spec = pltpu.PrefetchScalarGridSpec(
        num_scalar_prefetch=1,
        grid=(n_seq, n_pages // PAGES_PER_STEP),
        in_specs=[seq3(n_hq), seq3(n_new), seq3(n_new)]
        + [page_spec(i) for i in range(PAGES_PER_STEP)] * 2,
        out_specs=seq3(n_q),
        scratch_shapes=[
            pltpu.VMEM((n_hq, nblk), F32),
            pltpu.VMEM((n_hq, nblk), F32),
            pltpu.VMEM((n_hq, nblk), F32),
            pltpu.VMEM((nblk, n_hq, MOBA_W), F32),
        ],
    )
    return pl.pallas_call(
        functools.partial(_moba_sample_kernel, n_q=n_q),
        grid_spec=grid_spec,
        out_shape=jax.ShapeDtypeStruct((n_seq, n_q, MOBA_W), F32),
        compiler_params=_params(2),
        name="moba_sample",
    )(page_table, qbd, kn, vn, *([to_t(cache_k)] * PAGES_PER_STEP), *([to_t(cache_v)] * PAGES_PER_STEP))


def _cumsum_rows(x):
    n = x.shape[0]
    row = lax.broadcasted_iota(jnp.int32, x.shape, 0)
    s = 1
    while s < n:
        x = x + jnp.where(row >= s, pltpu.roll(x, s, axis=0), 0.0)
        s *= 2
    return x


def _gla_chunk(q, k, v, la, st, cast):
    c = q.shape[0]
    sub = min(GLA_SUB, c)
    lane = lax.broadcasted_iota(jnp.int32, (1, GLA_KW), 1)
    head_masks = [(lane >= h * DK_GLA) & (lane < (h + 1) * DK_GLA) for h in range(H_GLA)]
    g = _cumsum_rows(la)
    krow = lax.broadcasted_iota(jnp.int32, (c, GLA_KW), 0)
    a_row = lax.broadcasted_iota(jnp.int32, (H_GLA * sub, c), 0)
    a_col = lax.broadcasted_iota(jnp.int32, (H_GLA * sub, c), 1)

    a_parts = []
    for i in range(c // sub):
        g_ref = jnp.zeros((1, GLA_KW), F32) if i == 0 else g[i * sub - 1:i * sub]
        qt = q[i * sub:(i + 1) * sub] * jnp.exp(g[i * sub:(i + 1) * sub] - g_ref)
        kt = jnp.where(krow < (i + 1) * sub, k * jnp.exp(g_ref - g), 0.0)
        q_stack = jnp.concatenate([jnp.where(hm, qt, 0.0) for hm in head_masks], axis=0)
        a = _dot_nt(cast(q_stack), cast(kt))
        a_parts.append(jnp.where(a_col <= i * sub + a_row % sub, a, 0.0))

    qe = q * jnp.exp(g)
    g_last = g[c - 1:c]
    k_dec = cast(k * jnp.exp(g_last - g))
    st_c = cast(st)
    outs = []
    st_new = st * jnp.exp(g_last)
    for h in range(H_GLA):
        a_h = jnp.concatenate([a[h * sub:(h + 1) * sub] for a in a_parts], axis=0)
        v_h = cast(v[:, h * DV_GLA:(h + 1) * DV_GLA])
        o_intra = _dot(cast(a_h), v_h)
        o_inter = _dot_nt(cast(jnp.where(head_masks[h], qe, 0.0)), st_c)
        outs.append(o_intra + o_inter)
        st_new = st_new + jnp.where(head_masks[h], _dot_tn(v_h, k_dec), 0.0)
    return outs, st_new


def _gla_kernel(q_ref, k_ref, v_ref, la_ref, r_ref, s0_ref, nw_ref, o_ref, sfin_ref, st_sc, *, chunk):
    t = pl.program_id(1)

    @pl.when(t == 0)
    def _():
        st_sc[...] = s0_ref[0]

    cast = (lambda a: a.astype(BF16)) if chunk >= 2 * SUBLANES else (lambda a: a)
    st = st_sc[...]
    for c in range(q_ref.shape[0] // chunk):
        rows = slice(c * chunk, (c + 1) * chunk)
        q = q_ref[rows, :].astype(F32) * (DK_GLA ** -0.5)
        k = k_ref[rows, :].astype(F32)
        outs, st = _gla_chunk(q, k, v_ref[rows, :], la_ref[rows, :], st, cast)
        for h, o in enumerate(outs):
            lanes = slice(h * DV_GLA, (h + 1) * DV_GLA)
            r = r_ref[rows, lanes]
            o = o * lax.rsqrt(jnp.mean(o * o, axis=-1, keepdims=True) + EPS)
            o_ref[rows, lanes] = (o * nw_ref[:, lanes] * (r * _sigmoid(r))).astype(o_ref.dtype)
    st_sc[...] = st

    @pl.when(t == pl.num_programs(1) - 1)
    def _():
        sfin_ref[0] = st


def _gla(qg, kg, vg, la, rg, state_t, nw, n_seq, seq_len, out_dtype):
    chunk = math.gcd(seq_len, GLA_CHUNK)
    tl = min(seq_len, ROW_TILE)
    assert seq_len % tl == 0 and tl % chunk == 0
    nt = seq_len // tl
    n = n_seq * seq_len
    row = lambda w: pl.BlockSpec((tl, w), lambda b, t: (b * nt + t, 0))
    st_spec = pl.BlockSpec((1, DV_GLA, GLA_KW), lambda b, t: (b, 0, 0))
    return pl.pallas_call(
        functools.partial(_gla_kernel, chunk=chunk),
        grid=(n_seq, nt),
        in_specs=[row(GLA_KW), row(GLA_KW), row(GLA_VW), row(GLA_KW), row(GLA_VW), st_spec,
                  _const_spec((1, GLA_VW))],
        out_specs=[row(GLA_VW), st_spec],
        out_shape=[jax.ShapeDtypeStruct((n, GLA_VW), out_dtype),
                   jax.ShapeDtypeStruct((n_seq, DV_GLA, GLA_KW), F32)],
        scratch_shapes=[pltpu.VMEM((DV_GLA, GLA_KW), F32)],
        compiler_params=_params(2),
        name="gla",
    )(qg, kg, vg, la, rg, state_t, nw)


def _state_to_t(s):
    n = s.shape[0]
    return jnp.transpose(s, (0, 3, 1, 2)).reshape(n, DV_GLA, GLA_KW)


def _state_from_t(st):
    n = st.shape[0]
    return jnp.transpose(st.reshape(n, DV_GLA, H_GLA, DK_GLA), (0, 2, 3, 1))


def _cross_kernel(q_ref, mk_ref, mv_ref, o_ref):
    scale = HD_CROSS ** -0.5
    small = q_ref.shape[0] < 2 * SUBLANES
    cast = (lambda a: a) if small else (lambda a: a.astype(BF16))
    for h in range(H_CROSS):
        lanes = slice(h * HD_CROSS, (h + 1) * HD_CROSS)
        s = _dot_nt(cast(q_ref[:, lanes]), cast(mk_ref[:, lanes])) * scale
        m = jnp.max(s, axis=-1, keepdims=True)
        p = jnp.exp(s - m)
        l = jnp.sum(p, axis=-1, keepdims=True)
        o_ref[:, lanes] = (_dot(cast(p), cast(mv_ref[:, lanes])) / l).astype(o_ref.dtype)


def _cross(qc, mk, mv, n_seq, seq_len, n_mem, out_dtype):
    tl = min(seq_len, ROW_TILE)
    nt = seq_len // tl
    row = pl.BlockSpec((tl, CROSS_W), lambda b, t: (b * nt + t, 0))
    mem = pl.BlockSpec((n_mem, CROSS_W), lambda b, t: (b, 0))
    return pl.pallas_call(
        _cross_kernel,
        grid=(n_seq, nt),
        in_specs=[row, mem, mem],
        out_specs=row,
        out_shape=jax.ShapeDtypeStruct((n_seq * seq_len, CROSS_W), out_dtype),
        compiler_params=_params(2),
        name="cross",
    )(qc, mk, mv)


def _merge_kernel(x_ref, om_ref, og_ref, oc_ref, nw_ref, wg_ref, bg_ref, wbm_ref, wbg_ref, wbc_ref,
                  wo_ref, h_ref):
    x = x_ref[...]
    d = x.shape[1]
    xb = _rms(x, nw_ref[...]).astype(BF16)
    merged = jnp.zeros(x.shape, F32)
    for i, (o_ref, wb_ref) in enumerate(((om_ref, wbm_ref), (og_ref, wbg_ref), (oc_ref, wbc_ref))):
        cols = slice(i * d, (i + 1) * d)
        gate = _sigmoid(_dot(xb, wg_ref[:, cols]) + bg_ref[:, cols])
        merged = merged + gate * _dot(o_ref[...].astype(BF16), wb_ref[...])
    h_ref[...] = x + _dot(merged.astype(BF16), wo_ref[...])


def _merge(x2d, om, og, oc, nw, wg, bg, wbm, wbg, wbc, wo):
    n, d = x2d.shape
    tm = ROW_TILE
    row = lambda w: pl.BlockSpec((tm, w), lambda i: (i, 0))
    return pl.pallas_call(
        _merge_kernel,
        grid=(n // tm,),
        in_specs=[row(d), row(MOBA_W), row(GLA_VW), row(CROSS_W), _const_spec((1, d)),
                  _const_spec(wg.shape), _const_spec(bg.shape), _const_spec(wbm.shape),
                  _const_spec(wbg.shape), _const_spec(wbc.shape), _const_spec(wo.shape)],
        out_specs=row(d),
        out_shape=jax.ShapeDtypeStruct((n, d), F32),
        compiler_params=_params(1),
        name="merge",
    )(x2d, om, og, oc, nw, wg, bg, wbm, wbg, wbc, wo)


def _gelu_tanh(x):
    return 0.5 * x * (1.0 + jnp.tanh(0.7978845608028654 * (x + 0.044715 * (x * x * x))))


def _ffn_body(h_ref, nw_ref, wup_ref, wc_ref, bc_ref, wdn_ref, nf_ref, y_ref, prev_rows, store_u):
    h = h_ref[...]
    hb = _rms(h, nw_ref[...]).astype(BF16)
    d_ff = wdn_ref.shape[0]
    n_col_chunks = 2
    fc = d_ff // n_col_chunks
    assert fc % LANES == 0
    acc = jnp.zeros(h.shape, F32)
    for c in range(n_col_chunks):
        cols = slice(c * fc, (c + 1) * fc)
        u = _dot(hb, wup_ref[:, cols])
        gate = _dot(hb, wup_ref[:, d_ff + c * fc:d_ff + (c + 1) * fc])
        u1, u2 = prev_rows(cols, u)
        store_u(cols, u)
        conv = bc_ref[:, cols] + wc_ref[0:1, cols] * u2 + wc_ref[1:2, cols] * u1 + wc_ref[2:3, cols] * u
        act = (_gelu_tanh(conv) * gate).astype(BF16)
        acc = acc + _dot(act, wdn_ref[cols, :])
    y_ref[...] = _rms(h + acc, nf_ref[...])


def _ffn_long_kernel(h_ref, nw_ref, wup_ref, wc_ref, bc_ref, wdn_ref, nf_ref, y_ref, cs_ref, carry_sc):
    tm = h_ref.shape[0]

    @pl.when(pl.program_id(1) == 0)
    def _():
        carry_sc[...] = jnp.zeros(carry_sc.shape, F32)

    row = lax.broadcasted_iota(jnp.int32, (tm, 1), 0)

    def prev_rows(cols, u):
        last = carry_sc[SUBLANES - 1:SUBLANES, cols]
        last2 = carry_sc[SUBLANES - 2:SUBLANES - 1, cols]
        u1 = jnp.where(row == 0, last, pltpu.roll(u, 1, axis=0))
        u2 = jnp.where(row == 0, last2, jnp.where(row == 1, last, pltpu.roll(u, 2, axis=0)))
        return u1, u2

    def store_u(cols, u):
        carry_sc[:, cols] = u[tm - SUBLANES:tm]
        cs_ref[0, :, cols] = u[tm - (CONV_W - 1):tm]

    _ffn_body(h_ref, nw_ref, wup_ref, wc_ref, bc_ref, wdn_ref, nf_ref, y_ref, prev_rows, store_u)


def _ffn_short_kernel(h_ref, p1_ref, p2_ref, nw_ref, wup_ref, wc_ref, bc_ref, wdn_ref, nf_ref,
                      y_ref, u_ref, *, seq_len):
    tm = h_ref.shape[0]
    pos = lax.broadcasted_iota(jnp.int32, (tm, 1), 0) % seq_len

    def prev_rows(cols, u):
        u1 = jnp.where(pos >= 1, pltpu.roll(u, 1, axis=0), p1_ref[:, cols])
        u2 = jnp.where(pos >= 2, pltpu.roll(u, 2, axis=0), p2_ref[:, cols])
        return u1, u2

    def store_u(cols, u):
        u_ref[:, cols] = u

    _ffn_body(h_ref, nw_ref, wup_ref, wc_ref, bc_ref, wdn_ref, nf_ref, y_ref, prev_rows, store_u)


def _ffn_weights_specs(d, wup, wc, bc, wdn):
    return [_const_spec((1, d)), _const_spec(wup.shape), _const_spec(wc.shape), _const_spec(bc.shape),
            _const_spec(wdn.shape), _const_spec((1, d))]


def _ffn_long(h2d, n_seq, seq_len, nw, wup, wc, bc, wdn, nf):
    n, d = h2d.shape
    d_ff = wdn.shape[0]
    tm = ROW_TILE
    assert seq_len % tm == 0
    nt = seq_len // tm
    row = pl.BlockSpec((tm, d), lambda b, t: (b * nt + t, 0))
    return pl.pallas_call(
        _ffn_long_kernel,
        grid=(n_seq, nt),
        in_specs=[row] + _ffn_weights_specs(d, wup, wc, bc, wdn),
        out_specs=[row, pl.BlockSpec((1, CONV_W - 1, d_ff), lambda b, t: (b, 0, 0))],
        out_shape=[jax.ShapeDtypeStruct((n, d), F32),
                   jax.ShapeDtypeStruct((n_seq, CONV_W - 1, d_ff), F32)],
        scratch_shapes=[pltpu.VMEM((SUBLANES, d_ff), F32)],
        compiler_params=_params(2),
        name="ffn_long",
    )(h2d, nw, wup, wc, bc, wdn, nf)


def _ffn_short(h2d, conv_prev, n_seq, seq_len, nw, wup, wc, bc, wdn, nf):
    n, d = h2d.shape
    d_ff = wdn.shape[0]
    tm = ROW_TILE
    assert tm % seq_len == 0 and n % tm == 0 and seq_len >= CONV_W - 1
    zeros = jnp.zeros((n_seq, seq_len, d_ff), F32)
    p1 = zeros.at[:, 0].set(conv_prev[:, 1]).reshape(n, d_ff)
    p2 = zeros.at[:, 0].set(conv_prev[:, 0]).at[:, 1].set(conv_prev[:, 1]).reshape(n, d_ff)
    row = lambda w: pl.BlockSpec((tm, w), lambda i: (i, 0))
    y, u = pl.pallas_call(
        functools.partial(_ffn_short_kernel, seq_len=seq_len),
        grid=(n // tm,),
        in_specs=[row(d), row(d_ff), row(d_ff)] + _ffn_weights_specs(d, wup, wc, bc, wdn),
        out_specs=[row(d), row(d_ff)],
        out_shape=[jax.ShapeDtypeStruct((n, d), F32), jax.ShapeDtypeStruct((n, d_ff), F32)],
        compiler_params=_params(1),
        name="ffn_short",
    )(h2d, p1, p2, nw, wup, wc, bc, wdn, nf)
    return y, u.reshape(n_seq, seq_len, d_ff)[:, seq_len - (CONV_W - 1):]


def _prep_weights(norm_mix, w_in, w_gla_a2, b_gla_a, norm_gla, norm_mem, w_mem_kv, w_br_moba, w_br_gla,
                  w_br_cross, w_gate, b_gate, w_out, norm_ffn, w_up, w_conv, b_conv, w_down, norm_final):
    d = w_in.shape[0]
    o_a = C_QC
    w_in_p = jnp.concatenate(
        [w_in[:, :o_a], w_in[:, o_a + GLA_RANK:], w_in[:, o_a:o_a + GLA_RANK],
         jnp.zeros((d, A_PAD - GLA_RANK), w_in.dtype)], axis=1).astype(BF16)
    wa2_p = jnp.concatenate([w_gla_a2, jnp.zeros((A_PAD - GLA_RANK, GLA_KW), w_gla_a2.dtype)],
                            axis=0).astype(BF16)
    r2 = lambda a: a.reshape(1, -1)
    return dict(
        norm_mix=r2(norm_mix), w_in=w_in_p, wa2=wa2_p, ba=r2(b_gla_a), norm_gla=r2(norm_gla),
        norm_mem=r2(norm_mem), w_mem_kv=w_mem_kv.astype(BF16), wbm=w_br_moba.astype(BF16),
        wbg=w_br_gla.astype(BF16), wbc=w_br_cross.astype(BF16), wg=w_gate.astype(BF16), bg=r2(b_gate),
        wo=w_out.astype(BF16), norm_ffn=r2(norm_ffn), wup=w_up.astype(BF16), wc=w_conv, bc=r2(b_conv),
        wdn=w_down.astype(BF16), norm_final=r2(norm_final))


def kernel(x_prompt, x_sample, cache_moba_k, cache_moba_v, state_gla, state_conv, cache_mem_k, cache_mem_v, page_table, mem_prompt, norm_mix, w_in, w_gla_a2, b_gla_a, norm_gla, norm_mem, w_mem_kv, w_br_moba, w_br_gla, w_br_cross, w_gate, b_gate, w_out, norm_ffn, w_up, w_conv, b_conv, w_down, norm_final):
    depth = w_in.shape[0]
    assert depth == 1
    bp, sp, d = x_prompt.shape
    bs, ss, _ = x_sample.shape
    n_mem = mem_prompt.shape[1]
    w = _prep_weights(norm_mix[0], w_in[0], w_gla_a2[0], b_gla_a[0], norm_gla[0], norm_mem[0], w_mem_kv[0],
                      w_br_moba[0], w_br_gla[0], w_br_cross[0], w_gate[0], b_gate[0], w_out[0], norm_ffn[0],
                      w_up[0], w_conv[0], b_conv[0], w_down[0], norm_final)

    def mix(x2d, o_m, o_g, o_c):
        return _merge(x2d, o_m, o_g, o_c, w["norm_mix"], w["wg"], w["bg"], w["wbm"], w["wbg"], w["wbc"], w["wo"])

    ffn_w = (w["norm_ffn"], w["wup"], w["wc"], w["bc"], w["wdn"], w["norm_final"])

    xp = x_prompt.reshape(bp * sp, d)
    qm, k_p, v_p, qg, kg, vg, rg, qc, la, ksum, kb, vb = _project(
        xp, w["norm_mix"], w["w_in"], w["wa2"], w["ba"], BF16)
    o_m = _moba_prompt(qm, kb, vb, ksum.reshape(-1, MOBA_W), bp, sp)
    mk_p, mv_p = _memory_kv(mem_prompt.reshape(bp * n_mem, d), w["norm_mem"], w["w_mem_kv"])
    o_g, gla_p = _gla(qg, kg, vg, la, rg, jnp.zeros((bp, DV_GLA, GLA_KW), F32), w["norm_gla"], bp, sp, BF16)
    o_c = _cross(qc, mk_p, mv_p, bp, sp, n_mem, BF16)
    h_p = mix(xp, o_m, o_g, o_c)
    y_p, conv_p = _ffn_long(h_p, bp, sp, *ffn_w)

    xs = x_sample.reshape(bs * ss, d)
    qm, k_s, v_s, qg, kg, vg, rg, qc, la, _, _, _ = _project(
        xs, w["norm_mix"], w["w_in"], w["wa2"], w["ba"], F32)
    r3 = lambda a: a.reshape(bs, ss, MOBA_W)
    o_m = _moba_sample(r3(qm), r3(k_s), r3(v_s), cache_moba_k[0], cache_moba_v[0],
                       page_table).reshape(bs * ss, MOBA_W)
    o_g, gla_s = _gla(qg, kg, vg, la, rg, _state_to_t(state_gla[0]), w["norm_gla"], bs, ss, F32)
    o_c = _cross(qc, cache_mem_k[0].reshape(bs * n_mem, CROSS_W), cache_mem_v[0].reshape(bs * n_mem, CROSS_W),
                 bs, ss, n_mem, F32)
    h_s = mix(xs, o_m, o_g, o_c)
    y_s, conv_s = _ffn_short(h_s, state_conv[0], bs, ss, *ffn_w)

    kv5 = lambda a, b, s: a.reshape(1, b, s, H_MOBA, HD_MOBA)
    return (y_p.reshape(bp, sp, d), y_s.reshape(bs, ss, d),
            kv5(k_p, bp, sp), kv5(v_p, bp, sp), kv5(k_s, bs, ss), kv5(v_s, bs, ss),
            _state_from_t(gla_p)[None], _state_from_t(gla_s)[None],
            conv_p[None], conv_s[None],
            mk_p.reshape(1, bp, n_mem, H_CROSS, HD_CROSS), mv_p.reshape(1, bp, n_mem, H_CROSS, HD_CROSS))
```

```python
import functools
import math

import jax
import jax.numpy as jnp
from jax import lax
from jax.experimental import pallas as pl
from jax.experimental.pallas import tpu as pltpu

F32 = jnp.float32
BF16 = jnp.bfloat16
NEG_INF = float("-inf")

EPS = 1e-6
H_MOBA, HD_MOBA = 8, 64
MOBA_W = H_MOBA * HD_MOBA
MOBA_BLOCK = 256
MOBA_TOPK = 3
PAGE_SIZE = 128
H_GLA, DK_GLA, DV_GLA = 4, 64, 128
GLA_KW, GLA_VW = H_GLA * DK_GLA, H_GLA * DV_GLA
GLA_RANK = 16
GLA_TAU = 16.0
GLA_CHUNK = 64
GLA_SUB = 16
H_CROSS, HD_CROSS = 4, 128
CROSS_W = H_CROSS * HD_CROSS
N_BRANCH = 3
CONV_W = 3

LANES = 128
SUBLANES = 8
VMEM_LIMIT_BYTES = 56 * 1024 * 1024

MASK_BIAS = -1e30
MOBA_GROUP = 4
MOBA_Q_SCALE = HD_MOBA ** -0.5 * math.log2(math.e)
ROW_TILE = 256
PAGES_PER_STEP = 8
A_PAD = LANES

C_QM, C_K, C_V = 0, MOBA_W, 2 * MOBA_W
C_QG = 3 * MOBA_W
C_KG = C_QG + GLA_KW
C_VG = C_KG + GLA_KW
C_RG = C_VG + GLA_VW
C_QC = C_RG + GLA_VW
C_A = C_QC + CROSS_W
IN_COLS_PAD = C_A + A_PAD


def _params(n_axes):
    return pltpu.CompilerParams(
        dimension_semantics=("arbitrary",) * n_axes,
        vmem_limit_bytes=VMEM_LIMIT_BYTES,
    )


def _const_spec(shape):
    nd = len(shape)
    return pl.BlockSpec(shape, lambda *_: (0,) * nd, pipeline_mode=pl.Buffered(1))


def _rms(x, w):
    return x * lax.rsqrt(jnp.mean(x * x, axis=-1, keepdims=True) + EPS) * w


def _sigmoid(x):
    return 1.0 / (1.0 + jnp.exp(-x))


def _dot(a, b):
    return jnp.dot(a, b, preferred_element_type=F32)


def _dot_nt(a, b):
    return lax.dot_general(a, b, (((1,), (1,)), ((), ())), preferred_element_type=F32)


def _dot_tn(a, b):
    return lax.dot_general(a, b, (((0,), (0,)), ((), ())), preferred_element_type=F32)


def _proj_kernel(x_ref, nw_ref, w_ref, wa2_ref, ba_ref,
                 qm_ref, k_ref, v_ref, qg_ref, kg_ref, vg_ref, rg_ref, qc_ref, la_ref, ksum_ref,
                 kb_ref, vb_ref, *, qm_scale):
    xb = _rms(x_ref[...], nw_ref[...]).astype(BF16)

    def mm(lo, hi):
        return _dot(xb, w_ref[:, lo:hi])

    qm_ref[...] = (mm(C_QM, C_K) * qm_scale).astype(qm_ref.dtype)
    k = mm(C_K, C_V)
    k_ref[...] = k
    kb_ref[...] = k.astype(BF16)
    for g in range(k.shape[0] // MOBA_BLOCK):
        ksum_ref[g] = jnp.sum(k[g * MOBA_BLOCK:(g + 1) * MOBA_BLOCK], axis=0, keepdims=True)
    v = mm(C_V, C_QG)
    v_ref[...] = v
    vb_ref[...] = v.astype(BF16)
    qg_ref[...] = mm(C_QG, C_KG).astype(qg_ref.dtype)
    kg_ref[...] = mm(C_KG, C_VG).astype(kg_ref.dtype)
    vg_ref[...] = mm(C_VG, C_RG).astype(vg_ref.dtype)
    rg_ref[...] = mm(C_RG, C_QC)
    qc_ref[...] = mm(C_QC, C_A).astype(qc_ref.dtype)
    z = _dot(mm(C_A, IN_COLS_PAD).astype(BF16), wa2_ref[...]) + ba_ref[...]
    la_ref[...] = (jnp.minimum(z, 0.0) - jnp.log1p(jnp.exp(-jnp.abs(z)))) * (1.0 / GLA_TAU)


def _project(x2d, nw, w_in_p, wa2_p, ba, act_dtype, qm_scale):
    n, d = x2d.shape
    assert n % ROW_TILE == 0
    tm = ROW_TILE
    row = lambda w: pl.BlockSpec((tm, w), lambda i: (i, 0))
    widths = (MOBA_W, MOBA_W, MOBA_W, GLA_KW, GLA_KW, GLA_VW, GLA_VW, CROSS_W, GLA_KW)
    dtypes = (act_dtype, F32, F32, act_dtype, act_dtype, act_dtype, F32, act_dtype, F32)
    out_shape = [jax.ShapeDtypeStruct((n, w), dt) for w, dt in zip(widths, dtypes)]
    out_shape.append(jax.ShapeDtypeStruct((n // MOBA_BLOCK, 1, MOBA_W), F32))
    out_shape += [jax.ShapeDtypeStruct((n, MOBA_W), BF16)] * 2
    out_specs = [row(w) for w in widths]
    out_specs.append(pl.BlockSpec((tm // MOBA_BLOCK, 1, MOBA_W), lambda i: (i, 0, 0)))
    out_specs += [row(MOBA_W)] * 2
    return pl.pallas_call(
        functools.partial(_proj_kernel, qm_scale=qm_scale),
        grid=(n // tm,),
        in_specs=[row(d), _const_spec((1, d)), _const_spec(w_in_p.shape),
                  _const_spec(wa2_p.shape), _const_spec((1, GLA_KW))],
        out_specs=out_specs,
        out_shape=out_shape,
        compiler_params=_params(1),
        name="proj",
    )(x2d, nw, w_in_p, wa2_p, ba)


def _memkv_kernel(m_ref, nw_ref, w_ref, mk_ref, mv_ref):
    mb = _rms(m_ref[...], nw_ref[...]).astype(BF16)
    mk_ref[...] = _dot(mb, w_ref[:, :CROSS_W])
    mv_ref[...] = _dot(mb, w_ref[:, CROSS_W:])


def _memory_kv(mem2d, nw, w_kv):
    n, d = mem2d.shape
    tm = ROW_TILE
    assert n % tm == 0
    return pl.pallas_call(
        _memkv_kernel,
        grid=(n // tm,),
        in_specs=[pl.BlockSpec((tm, d), lambda i: (i, 0)), _const_spec((1, d)), _const_spec(w_kv.shape)],
        out_specs=[pl.BlockSpec((tm, CROSS_W), lambda i: (i, 0))] * 2,
        out_shape=[jax.ShapeDtypeStruct((n, CROSS_W), F32)] * 2,
        compiler_params=_params(1),
        name="memkv",
    )(mem2d, nw, w_kv)


def _select_topk(gate, idx, valid, axis):
    g = jnp.where(valid, gate, NEG_INF)
    picked = jnp.zeros(gate.shape, jnp.bool_)
    for _ in range(MOBA_TOPK):
        m = jnp.max(g, axis=axis, keepdims=True)
        first = jnp.min(jnp.where(g == m, idx, float(2 ** 24)), axis=axis, keepdims=True)
        pick = (idx == first) & (m > NEG_INF) & valid
        picked = picked | pick
        g = jnp.where(pick, NEG_INF, g)
    return picked


def _moba_prompt_kernel(q_ref, k_ref, v_ref, ksum_ref, o_ref, ka_sc, va_sc, s_sc):
    t = pl.program_id(2)
    tq = q_ref.shape[0]
    blk = MOBA_BLOCK
    nblk = ksum_ref.shape[0]
    seq_len = k_ref.shape[0]
    group = s_sc.shape[3] // blk
    n_heads = LANES // HD_MOBA
    spare_off = [((half + 1) % n_heads) * HD_MOBA for half in range(n_heads)]

    @pl.when(t == 0)
    def _():
        key_blk = lax.broadcasted_iota(jnp.int32, (seq_len, LANES), 0) // blk
        key_lane = lax.broadcasted_iota(jnp.int32, (seq_len, LANES), 1)
        k = k_ref[...]
        v = v_ref[...]
        for half in range(n_heads):
            in_head = (key_lane >= half * HD_MOBA) & (key_lane < (half + 1) * HD_MOBA)
            onehot = jnp.where(key_lane - spare_off[half] == key_blk, 1.0, 0.0).astype(BF16)
            ka_sc[half] = jnp.where(in_head, k, onehot)
            va_sc[half] = jnp.where(in_head, v, jnp.ones_like(v))

    q = q_ref[...]
    kmean = ksum_ref[...] * (1.0 / blk)
    km_hi = kmean.astype(BF16)
    km_lo = (kmean - km_hi.astype(F32)).astype(BF16)
    lane = lax.broadcasted_iota(jnp.int32, (1, LANES), 1)
    row = lax.broadcasted_iota(jnp.int32, (tq, blk), 0)
    col = lax.broadcasted_iota(jnp.int32, (tq, blk), 1)
    blk_id = lax.broadcasted_iota(jnp.int32, (nblk, tq), 0).astype(F32)
    place_row = lax.broadcasted_iota(jnp.int32, (nblk, LANES), 0)
    place_lane = lax.broadcasted_iota(jnp.int32, (nblk, LANES), 1)
    own = pl.ds(pl.multiple_of(t * blk, blk), blk)
    n_groups = (t + group - 1) // group
    heads = range(n_heads)

    in_head, q_aug, s_own = [], [], []
    for half in heads:
        off = spare_off[half]
        in_head.append((lane >= half * HD_MOBA) & (lane < (half + 1) * HD_MOBA))
        qh = jnp.where(in_head[half], q, jnp.zeros_like(q))
        gate_t = _dot_nt(km_hi, qh) + _dot_nt(km_lo, qh)
        picked_t = _select_topk(gate_t, blk_id, blk_id < t.astype(F32), axis=0)
        place = jnp.where(place_lane - off == place_row, 1.0, 0.0).astype(BF16)
        picked = _dot_tn(jnp.where(picked_t, 1.0, 0.0).astype(BF16), place)
        in_range = (lane >= off) & (lane < off + nblk)
        bias = jnp.where(in_range & (picked < 0.5), MASK_BIAS, 0.0)
        q_aug.append(qh + bias.astype(BF16))
        s_own.append(jnp.where(col <= row, _dot_nt(qh, ka_sc[half, own, :]), MASK_BIAS))

    def group_rows(g):
        return pl.ds(pl.multiple_of(g * (group * blk), group * blk), group * blk)

    def score_pass(g, mxs):
        out = []
        for half in heads:
            s = _dot_nt(q_aug[half], ka_sc[half, group_rows(g), :])
            s_sc[half, g] = s
            mx = mxs[half]
            for i in range(group):
                mx = jnp.maximum(mx, s[:, i * blk:(i + 1) * blk])
            out.append(mx)
        return tuple(out)

    mxs = lax.fori_loop(0, n_groups, score_pass, tuple(s_own))
    m = [jnp.max(mx, axis=-1, keepdims=True) for mx in mxs]

    def value_pass(g, accs):
        out = []
        for half in heads:
            p = jnp.exp2(s_sc[half, g] - m[half])
            out.append(accs[half] + _dot(p.astype(BF16), va_sc[half, group_rows(g), :]))
        return tuple(out)

    acc0 = tuple(_dot(jnp.exp2(s_own[half] - m[half]).astype(BF16), va_sc[half, own, :]) for half in heads)
    accs = lax.fori_loop(0, n_groups, value_pass, acc0)

    o = jnp.zeros((tq, LANES), F32)
    for half in heads:
        row_sum = accs[half][:, spare_off[half]:spare_off[half] + 1]
        o = jnp.where(in_head[half], accs[half] / row_sum, o)
    o_ref[...] = o.astype(o_ref.dtype)


def _moba_prompt(qm, kb, vb, ksum, n_seq, seq_len):
    assert seq_len % MOBA_BLOCK == 0
    nblk = seq_len // MOBA_BLOCK
    assert nblk <= HD_MOBA
    group = MOBA_GROUP if nblk % MOBA_GROUP == 0 else 1
    tq = MOBA_BLOCK
    n = n_seq * seq_len
    n_pair = MOBA_W // LANES
    n_heads = LANES // HD_MOBA
    seq_block = pl.BlockSpec((seq_len, LANES), lambda b, hp, t: (b, hp))
    return pl.pallas_call(
        _moba_prompt_kernel,
        grid=(n_seq, n_pair, nblk),
        in_specs=[
            pl.BlockSpec((tq, LANES), lambda b, hp, t: (b * nblk + t, hp)),
            seq_block, seq_block,
            pl.BlockSpec((nblk, LANES), lambda b, hp, t: (b, hp)),
        ],
        out_specs=pl.BlockSpec((tq, LANES), lambda b, hp, t: (b * nblk + t, hp)),
        out_shape=jax.ShapeDtypeStruct((n, MOBA_W), BF16),
        scratch_shapes=[pltpu.VMEM((n_heads, seq_len, LANES), BF16),
                        pltpu.VMEM((n_heads, seq_len, LANES), BF16),
                        pltpu.VMEM((n_heads, nblk // group, tq, group * MOBA_BLOCK), F32)],
        compiler_params=_params(3),
        name="moba_prompt",
    )(qm, kb, vb, ksum)


def _moba_sample_kernel(pt_ref, qbd_ref, kn_ref, vn_ref, *rest, n_q):
    del pt_ref
    pps = PAGES_PER_STEP
    k_pages, v_pages = rest[:pps], rest[pps:2 * pps]
    o_ref, m_sc, l_sc, gate_sc, acc_sc = rest[2 * pps:]
    s_id = pl.program_id(1)
    n_step = pl.num_programs(1)
    nblk = m_sc.shape[1]
    n_hq = qbd_ref.shape[1]
    qbd = qbd_ref[0]
    hq_row = lax.broadcasted_iota(jnp.int32, (n_hq, MOBA_W), 0)
    w_lane = lax.broadcasted_iota(jnp.int32, (n_hq, MOBA_W), 1)
    head_mask = (w_lane // HD_MOBA) == (hq_row // n_q)
    blk_col = lax.broadcasted_iota(jnp.int32, (n_hq, nblk), 1)
    pages_per_blk = MOBA_BLOCK // PAGE_SIZE

    for i in range(pps // pages_per_blk):
        pages = range(i * pages_per_blk, (i + 1) * pages_per_blk)
        kt = jnp.concatenate([k_pages[j][0] for j in pages], axis=1).astype(BF16)
        vt = jnp.concatenate([v_pages[j][0] for j in pages], axis=1).astype(BF16)
        b_idx = s_id * (pps // pages_per_blk) + i
        s = _dot(qbd, kt)
        gate = jnp.sum(s, axis=-1, keepdims=True)
        m = jnp.max(s, axis=-1, keepdims=True)
        p = jnp.exp(s - m)
        l = jnp.sum(p, axis=-1, keepdims=True)
        acc_sc[b_idx] = jnp.where(head_mask, _dot_nt(p.astype(BF16), vt), 0.0)
        m_sc[...] = jnp.where(blk_col == b_idx, m, m_sc[...])
        l_sc[...] = jnp.where(blk_col == b_idx, l, l_sc[...])
        gate_sc[...] = jnp.where(blk_col == b_idx, gate, gate_sc[...])

    @pl.when(s_id == n_step - 1)
    def _():
        picked = _select_topk(gate_sc[...], blk_col.astype(F32), blk_col >= 0, axis=1)
        m_all = jnp.where(picked, m_sc[...], NEG_INF)
        kn = kn_ref[0].astype(BF16)
        vn = vn_ref[0].astype(BF16)
        n_new = kn.shape[0]
        key_i = lax.broadcasted_iota(jnp.int32, (n_hq, n_new), 1)
        q_i = lax.broadcasted_iota(jnp.int32, (n_hq, n_new), 0) % n_q
        s_own = jnp.where(key_i <= q_i, _dot_nt(qbd, kn), NEG_INF)
        m_tot = jnp.maximum(jnp.max(s_own, axis=-1, keepdims=True),
                            jnp.max(m_all, axis=-1, keepdims=True))
        p_own = jnp.exp(s_own - m_tot)
        c = jnp.exp(m_all - m_tot)
        l_tot = jnp.sum(p_own, axis=-1, keepdims=True) + jnp.sum(c * l_sc[...], axis=-1, keepdims=True)
        out = jnp.where(head_mask, _dot(p_own.astype(BF16), vn), 0.0)
        for b in range(nblk):
            out = out + c[:, b:b + 1] * acc_sc[b]
        out = out / l_tot
        o = out[0:n_q]
        for h in range(1, H_MOBA):
            o = o + out[h * n_q:(h + 1) * n_q]
        o_ref[0] = o


def _moba_sample(q, k_new, v_new, cache_k, cache_v, page_table):
    n_seq, n_q, _ = q.shape
    n_phys = cache_k.shape[0]
    n_pages = page_table.shape[1]
    past = n_pages * PAGE_SIZE
    assert past % MOBA_BLOCK == 0 and n_pages % PAGES_PER_STEP == 0
    assert n_q <= MOBA_BLOCK and n_q % SUBLANES == 0
    nblk = past // MOBA_BLOCK
    n_hq = H_MOBA * n_q
    lane_head = jnp.arange(MOBA_W) // HD_MOBA
    qbd = jnp.where(lane_head[None, None, None, :] == jnp.arange(H_MOBA)[None, :, None, None],
                    q[:, None, :, :] * (HD_MOBA ** -0.5), 0.0).reshape(n_seq, n_hq, MOBA_W).astype(BF16)
    n_new = max(2 * SUBLANES, n_q)
    pad = ((0, 0), (0, n_new - n_q), (0, 0))
    kn = jnp.pad(k_new, pad)
    vn = jnp.pad(v_new, pad)
    to_t = lambda c: jnp.transpose(c, (0, 2, 3, 1)).reshape(n_phys, MOBA_W, PAGE_SIZE)

    def page_spec(i):
        return pl.BlockSpec((1, MOBA_W, PAGE_SIZE), lambda b, s, pt: (pt[b, s * PAGES_PER_STEP + i], 0, 0))

    seq3 = lambda r: pl.BlockSpec((1, r, MOBA_W), lambda b, s, pt: (b, 0, 0))
    grid_spec = pltpu.PrefetchScalarGridSpec(
        num_scalar_prefetch=1,
        grid=(n_seq, n_pages // PAGES_PER_STEP),
        in_specs=[seq3(n_hq), seq3(n_new), seq3(n_new)]
        + [page_spec(i) for i in range(PAGES_PER_STEP)] * 2,
        out_specs=seq3(n_q),
        scratch_shapes=[
            pltpu.VMEM((n_hq, nblk), F32),
            pltpu.VMEM((n_hq, nblk), F32),
            pltpu.VMEM((n_hq, nblk), F32),
            pltpu.VMEM((nblk, n_hq, MOBA_W), F32),
        ],
    )
    return pl.pallas_call(
        functools.partial(_moba_sample_kernel, n_q=n_q),
        grid_spec=grid_spec,
        out_shape=jax.ShapeDtypeStruct((n_seq, n_q, MOBA_W), F32),
        compiler_params=_params(2),
        name="moba_sample",
    )(page_table, qbd, kn, vn, *([to_t(cache_k)] * PAGES_PER_STEP), *([to_t(cache_v)] * PAGES_PER_STEP))


def _cumsum_rows(x):
    n = x.shape[0]
    row = lax.broadcasted_iota(jnp.int32, x.shape, 0)
    s = 1
    while s < n:
        x = x + jnp.where(row >= s, pltpu.roll(x, s, axis=0), 0.0)
        s *= 2
    return x


def _gla_chunk(q, k, v, la, st, cast):
    c = q.shape[0]
    sub = min(GLA_SUB, c)
    lane = lax.broadcasted_iota(jnp.int32, (1, GLA_KW), 1)
    head_masks = [(lane >= h * DK_GLA) & (lane < (h + 1) * DK_GLA) for h in range(H_GLA)]
    g = _cumsum_rows(la)
    krow = lax.broadcasted_iota(jnp.int32, (c, GLA_KW), 0)
    a_row = lax.broadcasted_iota(jnp.int32, (H_GLA * sub, c), 0)
    a_col = lax.broadcasted_iota(jnp.int32, (H_GLA * sub, c), 1)

    a_parts = []
    for i in range(c // sub):
        g_ref = jnp.zeros((1, GLA_KW), F32) if i == 0 else g[i * sub - 1:i * sub]
        qt = q[i * sub:(i + 1) * sub] * jnp.exp(g[i * sub:(i + 1) * sub] - g_ref)
        kt = jnp.where(krow < (i + 1) * sub, k * jnp.exp(g_ref - g), 0.0)
        q_stack = jnp.concatenate([jnp.where(hm, qt, 0.0) for hm in head_masks], axis=0)
        a = _dot_nt(cast(q_stack), cast(kt))
        a_parts.append(jnp.where(a_col <= i * sub + a_row % sub, a, 0.0))

    qe = q * jnp.exp(g)
    g_last = g[c - 1:c]
    k_dec = cast(k * jnp.exp(g_last - g))
    st_c = cast(st)
    outs = []
    st_new = st * jnp.exp(g_last)
    for h in range(H_GLA):
        a_h = jnp.concatenate([a[h * sub:(h + 1) * sub] for a in a_parts], axis=0)
        v_h = cast(v[:, h * DV_GLA:(h + 1) * DV_GLA])
        o_intra = _dot(cast(a_h), v_h)
        o_inter = _dot_nt(cast(jnp.where(head_masks[h], qe, 0.0)), st_c)
        outs.append(o_intra + o_inter)
        st_new = st_new + jnp.where(head_masks[h], _dot_tn(v_h, k_dec), 0.0)
    return outs, st_new


def _gla_kernel(q_ref, k_ref, v_ref, la_ref, r_ref, s0_ref, nw_ref, o_ref, sfin_ref, st_sc, *, chunk):
    t = pl.program_id(1)

    @pl.when(t == 0)
    def _():
        st_sc[...] = s0_ref[0]

    cast = (lambda a: a.astype(BF16)) if chunk >= 2 * SUBLANES else (lambda a: a)
    st = st_sc[...]
    for c in range(q_ref.shape[0] // chunk):
        rows = slice(c * chunk, (c + 1) * chunk)
        q = q_ref[rows, :].astype(F32) * (DK_GLA ** -0.5)
        k = k_ref[rows, :].astype(F32)
        outs, st = _gla_chunk(q, k, v_ref[rows, :], la_ref[rows, :], st, cast)
        for h, o in enumerate(outs):
            lanes = slice(h * DV_GLA, (h + 1) * DV_GLA)
            r = r_ref[rows, lanes]
            o = o * lax.rsqrt(jnp.mean(o * o, axis=-1, keepdims=True) + EPS)
            o_ref[rows, lanes] = (o * nw_ref[:, lanes] * (r * _sigmoid(r))).astype(o_ref.dtype)
    st_sc[...] = st

    @pl.when(t == pl.num_programs(1) - 1)
    def _():
        sfin_ref[0] = st


def _gla(qg, kg, vg, la, rg, state_t, nw, n_seq, seq_len, out_dtype):
    chunk = math.gcd(seq_len, GLA_CHUNK)
    tl = min(seq_len, ROW_TILE)
    assert seq_len % tl == 0 and tl % chunk == 0
    nt = seq_len // tl
    n = n_seq * seq_len
    row = lambda w: pl.BlockSpec((tl, w), lambda b, t: (b * nt + t, 0))
    st_spec = pl.BlockSpec((1, DV_GLA, GLA_KW), lambda b, t: (b, 0, 0))
    return pl.pallas_call(
        functools.partial(_gla_kernel, chunk=chunk),
        grid=(n_seq, nt),
        in_specs=[row(GLA_KW), row(GLA_KW), row(GLA_VW), row(GLA_KW), row(GLA_VW), st_spec,
                  _const_spec((1, GLA_VW))],
        out_specs=[row(GLA_VW), st_spec],
        out_shape=[jax.ShapeDtypeStruct((n, GLA_VW), out_dtype),
                   jax.ShapeDtypeStruct((n_seq, DV_GLA, GLA_KW), F32)],
        scratch_shapes=[pltpu.VMEM((DV_GLA, GLA_KW), F32)],
        compiler_params=_params(2),
        name="gla",
    )(qg, kg, vg, la, rg, state_t, nw)


def _state_to_t(s):
    n = s.shape[0]
    return jnp.transpose(s, (0, 3, 1, 2)).reshape(n, DV_GLA, GLA_KW)


def _state_from_t(st):
    n = st.shape[0]
    return jnp.transpose(st.reshape(n, DV_GLA, H_GLA, DK_GLA), (0, 2, 3, 1))


def _cross_kernel(q_ref, mk_ref, mv_ref, o_ref):
    scale = HD_CROSS ** -0.5
    small = q_ref.shape[0] < 2 * SUBLANES
    cast = (lambda a: a) if small else (lambda a: a.astype(BF16))
    for h in range(H_CROSS):
        lanes = slice(h * HD_CROSS, (h + 1) * HD_CROSS)
        s = _dot_nt(cast(q_ref[:, lanes]), cast(mk_ref[:, lanes])) * scale
        m = jnp.max(s, axis=-1, keepdims=True)
        p = jnp.exp(s - m)
        l = jnp.sum(p, axis=-1, keepdims=True)
        o_ref[:, lanes] = (_dot(cast(p), cast(mv_ref[:, lanes])) / l).astype(o_ref.dtype)


def _cross(qc, mk, mv, n_seq, seq_len, n_mem, out_dtype):
    tl = min(seq_len, ROW_TILE)
    nt = seq_len // tl
    row = pl.BlockSpec((tl, CROSS_W), lambda b, t: (b * nt + t, 0))
    mem = pl.BlockSpec((n_mem, CROSS_W), lambda b, t: (b, 0))
    return pl.pallas_call(
        _cross_kernel,
        grid=(n_seq, nt),
        in_specs=[row, mem, mem],
        out_specs=row,
        out_shape=jax.ShapeDtypeStruct((n_seq * seq_len, CROSS_W), out_dtype),
        compiler_params=_params(2),
        name="cross",
    )(qc, mk, mv)


def _merge_kernel(x_ref, om_ref, og_ref, oc_ref, nw_ref, wg_ref, bg_ref, wbm_ref, wbg_ref, wbc_ref,
                  wo_ref, h_ref):
    x = x_ref[...]
    d = x.shape[1]
    xb = _rms(x, nw_ref[...]).astype(BF16)
    merged = jnp.zeros(x.shape, F32)
    for i, (o_ref, wb_ref) in enumerate(((om_ref, wbm_ref), (og_ref, wbg_ref), (oc_ref, wbc_ref))):
        cols = slice(i * d, (i + 1) * d)
        gate = _sigmoid(_dot(xb, wg_ref[:, cols]) + bg_ref[:, cols])
        merged = merged + gate * _dot(o_ref[...].astype(BF16), wb_ref[...])
    h_ref[...] = x + _dot(merged.astype(BF16), wo_ref[...])


def _merge(x2d, om, og, oc, nw, wg, bg, wbm, wbg, wbc, wo):
    n, d = x2d.shape
    tm = ROW_TILE
    row = lambda w: pl.BlockSpec((tm, w), lambda i: (i, 0))
    return pl.pallas_call(
        _merge_kernel,
        grid=(n // tm,),
        in_specs=[row(d), row(MOBA_W), row(GLA_VW), row(CROSS_W), _const_spec((1, d)),
                  _const_spec(wg.shape), _const_spec(bg.shape), _const_spec(wbm.shape),
                  _const_spec(wbg.shape), _const_spec(wbc.shape), _const_spec(wo.shape)],
        out_specs=row(d),
        out_shape=jax.ShapeDtypeStruct((n, d), F32),
        compiler_params=_params(1),
        name="merge",
    )(x2d, om, og, oc, nw, wg, bg, wbm, wbg, wbc, wo)


def _gelu_tanh(x):
    return 0.5 * x * (1.0 + jnp.tanh(0.7978845608028654 * (x + 0.044715 * (x * x * x))))


def _ffn_body(h_ref, nw_ref, wup_ref, wc_ref, bc_ref, wdn_ref, nf_ref, y_ref, prev_rows, store_u):
    h = h_ref[...]
    hb = _rms(h, nw_ref[...]).astype(BF16)
    d_ff = wdn_ref.shape[0]
    n_col_chunks = 2
    fc = d_ff // n_col_chunks
    assert fc % LANES == 0
    acc = jnp.zeros(h.shape, F32)
    for c in range(n_col_chunks):
        cols = slice(c * fc, (c + 1) * fc)
        u = _dot(hb, wup_ref[:, cols])
        gate = _dot(hb, wup_ref[:, d_ff + c * fc:d_ff + (c + 1) * fc])
        u1, u2 = prev_rows(cols, u)
        store_u(cols, u)
        conv = bc_ref[:, cols] + wc_ref[0:1, cols] * u2 + wc_ref[1:2, cols] * u1 + wc_ref[2:3, cols] * u
        act = (_gelu_tanh(conv) * gate).astype(BF16)
        acc = acc + _dot(act, wdn_ref[cols, :])
    y_ref[...] = _rms(h + acc, nf_ref[...])


def _ffn_long_kernel(h_ref, nw_ref, wup_ref, wc_ref, bc_ref, wdn_ref, nf_ref, y_ref, cs_ref, carry_sc):
    tm = h_ref.shape[0]

    @pl.when(pl.program_id(1) == 0)
    def _():
        carry_sc[...] = jnp.zeros(carry_sc.shape, F32)

    row = lax.broadcasted_iota(jnp.int32, (tm, 1), 0)

    def prev_rows(cols, u):
        last = carry_sc[SUBLANES - 1:SUBLANES, cols]
        last2 = carry_sc[SUBLANES - 2:SUBLANES - 1, cols]
        u1 = jnp.where(row == 0, last, pltpu.roll(u, 1, axis=0))
        u2 = jnp.where(row == 0, last2, jnp.where(row == 1, last, pltpu.roll(u, 2, axis=0)))
        return u1, u2

    def store_u(cols, u):
        carry_sc[:, cols] = u[tm - SUBLANES:tm]
        cs_ref[0, :, cols] = u[tm - (CONV_W - 1):tm]

    _ffn_body(h_ref, nw_ref, wup_ref, wc_ref, bc_ref, wdn_ref, nf_ref, y_ref, prev_rows, store_u)


def _ffn_short_kernel(h_ref, p1_ref, p2_ref, nw_ref, wup_ref, wc_ref, bc_ref, wdn_ref, nf_ref,
                      y_ref, u_ref, *, seq_len):
    tm = h_ref.shape[0]
    pos = lax.broadcasted_iota(jnp.int32, (tm, 1), 0) % seq_len

    def prev_rows(cols, u):
        u1 = jnp.where(pos >= 1, pltpu.roll(u, 1, axis=0), p1_ref[:, cols])
        u2 = jnp.where(pos >= 2, pltpu.roll(u, 2, axis=0), p2_ref[:, cols])
        return u1, u2

    def store_u(cols, u):
        u_ref[:, cols] = u

    _ffn_body(h_ref, nw_ref, wup_ref, wc_ref, bc_ref, wdn_ref, nf_ref, y_ref, prev_rows, store_u)


def _ffn_weights_specs(d, wup, wc, bc, wdn):
    return [_const_spec((1, d)), _const_spec(wup.shape), _const_spec(wc.shape), _const_spec(bc.shape),
            _const_spec(wdn.shape), _const_spec((1, d))]


def _ffn_long(h2d, n_seq, seq_len, nw, wup, wc, bc, wdn, nf):
    n, d = h2d.shape
    d_ff = wdn.shape[0]
    tm = ROW_TILE
    assert seq_len % tm == 0
    nt = seq_len // tm
    row = pl.BlockSpec((tm, d), lambda b, t: (b * nt + t, 0))
    return pl.pallas_call(
        _ffn_long_kernel,
        grid=(n_seq, nt),
        in_specs=[row] + _ffn_weights_specs(d, wup, wc, bc, wdn),
        out_specs=[row, pl.BlockSpec((1, CONV_W - 1, d_ff), lambda b, t: (b, 0, 0))],
        out_shape=[jax.ShapeDtypeStruct((n, d), F32),
                   jax.ShapeDtypeStruct((n_seq, CONV_W - 1, d_ff), F32)],
        scratch_shapes=[pltpu.VMEM((SUBLANES, d_ff), F32)],
        compiler_params=_params(2),
        name="ffn_long",
    )(h2d, nw, wup, wc, bc, wdn, nf)


def _ffn_short(h2d, conv_prev, n_seq, seq_len, nw, wup, wc, bc, wdn, nf):
    n, d = h2d.shape
    d_ff = wdn.shape[0]
    tm = ROW_TILE
    assert tm % seq_len == 0 and n % tm == 0 and seq_len >= CONV_W - 1
    zeros = jnp.zeros((n_seq, seq_len, d_ff), F32)
    p1 = zeros.at[:, 0].set(conv_prev[:, 1]).reshape(n, d_ff)
    p2 = zeros.at[:, 0].set(conv_prev[:, 0]).at[:, 1].set(conv_prev[:, 1]).reshape(n, d_ff)
    row = lambda w: pl.BlockSpec((tm, w), lambda i: (i, 0))
    y, u = pl.pallas_call(
        functools.partial(_ffn_short_kernel, seq_len=seq_len),
        grid=(n // tm,),
        in_specs=[row(d), row(d_ff), row(d_ff)] + _ffn_weights_specs(d, wup, wc, bc, wdn),
        out_specs=[row(d), row(d_ff)],
        out_shape=[jax.ShapeDtypeStruct((n, d), F32), jax.ShapeDtypeStruct((n, d_ff), F32)],
        compiler_params=_params(1),
        name="ffn_short",
    )(h2d, p1, p2, nw, wup, wc, bc, wdn, nf)
    return y, u.reshape(n_seq, seq_len, d_ff)[:, seq_len - (CONV_W - 1):]


def _prep_weights(norm_mix, w_in, w_gla_a2, b_gla_a, norm_gla, norm_mem, w_mem_kv, w_br_moba, w_br_gla,
                  w_br_cross, w_gate, b_gate, w_out, norm_ffn, w_up, w_conv, b_conv, w_down, norm_final):
    d = w_in.shape[0]
    o_a = C_QC
    w_in_p = jnp.concatenate(
        [w_in[:, :o_a], w_in[:, o_a + GLA_RANK:], w_in[:, o_a:o_a + GLA_RANK],
         jnp.zeros((d, A_PAD - GLA_RANK), w_in.dtype)], axis=1).astype(BF16)
    wa2_p = jnp.concatenate([w_gla_a2, jnp.zeros((A_PAD - GLA_RANK, GLA_KW), w_gla_a2.dtype)],
                            axis=0).astype(BF16)
    r2 = lambda a: a.reshape(1, -1)
    return dict(
        norm_mix=r2(norm_mix), w_in=w_in_p, wa2=wa2_p, ba=r2(b_gla_a), norm_gla=r2(norm_gla),
        norm_mem=r2(norm_mem), w_mem_kv=w_mem_kv.astype(BF16), wbm=w_br_moba.astype(BF16),
        wbg=w_br_gla.astype(BF16), wbc=w_br_cross.astype(BF16), wg=w_gate.astype(BF16), bg=r2(b_gate),
        wo=w_out.astype(BF16), norm_ffn=r2(norm_ffn), wup=w_up.astype(BF16), wc=w_conv, bc=r2(b_conv),
        wdn=w_down.astype(BF16), norm_final=r2(norm_final))


def kernel(x_prompt, x_sample, cache_moba_k, cache_moba_v, state_gla, state_conv, cache_mem_k, cache_mem_v, page_table, mem_prompt, norm_mix, w_in, w_gla_a2, b_gla_a, norm_gla, norm_mem, w_mem_kv, w_br_moba, w_br_gla, w_br_cross, w_gate, b_gate, w_out, norm_ffn, w_up, w_conv, b_conv, w_down, norm_final):
    depth = w_in.shape[0]
    assert depth == 1
    bp, sp, d = x_prompt.shape
    bs, ss, _ = x_sample.shape
    n_mem = mem_prompt.shape[1]
    w = _prep_weights(norm_mix[0], w_in[0], w_gla_a2[0], b_gla_a[0], norm_gla[0], norm_mem[0], w_mem_kv[0],
                      w_br_moba[0], w_br_gla[0], w_br_cross[0], w_gate[0], b_gate[0], w_out[0], norm_ffn[0],
                      w_up[0], w_conv[0], b_conv[0], w_down[0], norm_final)

    def mix(x2d, o_m, o_g, o_c):
        return _merge(x2d, o_m, o_g, o_c, w["norm_mix"], w["wg"], w["bg"], w["wbm"], w["wbg"], w["wbc"], w["wo"])

    ffn_w = (w["norm_ffn"], w["wup"], w["wc"], w["bc"], w["wdn"], w["norm_final"])

    xp = x_prompt.reshape(bp * sp, d)
    qm, k_p, v_p, qg, kg, vg, rg, qc, la, ksum, kb, vb = _project(
        xp, w["norm_mix"], w["w_in"], w["wa2"], w["ba"], BF16, MOBA_Q_SCALE)
    o_m = _moba_prompt(qm, kb, vb, ksum.reshape(-1, MOBA_W), bp, sp)
    mk_p, mv_p = _memory_kv(mem_prompt.reshape(bp * n_mem, d), w["norm_mem"], w["w_mem_kv"])
    o_g, gla_p = _gla(qg, kg, vg, la, rg, jnp.zeros((bp, DV_GLA, GLA_KW), F32), w["norm_gla"], bp, sp, BF16)
    o_c = _cross(qc, mk_p, mv_p, bp, sp, n_mem, BF16)
    h_p = mix(xp, o_m, o_g, o_c)
    y_p, conv_p = _ffn_long(h_p, bp, sp, *ffn_w)

    xs = x_sample.reshape(bs * ss, d)
    qm, k_s, v_s, qg, kg, vg, rg, qc, la, _, _, _ = _project(
        xs, w["norm_mix"], w["w_in"], w["wa2"], w["ba"], F32, 1.0)
    r3 = lambda a: a.reshape(bs, ss, MOBA_W)
    o_m = _moba_sample(r3(qm), r3(k_s), r3(v_s), cache_moba_k[0], cache_moba_v[0],
                       page_table).reshape(bs * ss, MOBA_W)
    o_g, gla_s = _gla(qg, kg, vg, la, rg, _state_to_t(state_gla[0]), w["norm_gla"], bs, ss, F32)
    o_c = _cross(qc, cache_mem_k[0].reshape(bs * n_mem, CROSS_W), cache_mem_v[0].reshape(bs * n_mem, CROSS_W),
                 bs, ss, n_mem, F32)
    h_s = mix(xs, o_m, o_g, o_c)
    y_s, conv_s = _ffn_short(h_s, state_conv[0], bs, ss, *ffn_w)

    kv5 = lambda a, b, s: a.reshape(1, b, s, H_MOBA, HD_MOBA)
    return (y_p.reshape(bp, sp, d), y_s.reshape(bs, ss, d),
            kv5(k_p, bp, sp), kv5(v_p, bp, sp), kv5(k_s, bs, ss), kv5(v_s, bs, ss),
            _state_from_t(gla_p)[None], _state_from_t(gla_s)[None],
            conv_p[None], conv_s[None],
            mk_p.reshape(1, bp, n_mem, H_CROSS, HD_CROSS), mv_p.reshape(1, bp, n_mem, H_CROSS, HD_CROSS))
```

```python
import functools
import math

import jax
import jax.numpy as jnp
from jax import lax
from jax.experimental import pallas as pl
from jax.experimental.pallas import tpu as pltpu

F32 = jnp.float32
BF16 = jnp.bfloat16
NEG_INF = float("-inf")

EPS = 1e-6
H_MOBA, HD_MOBA = 8, 64
MOBA_W = H_MOBA * HD_MOBA
MOBA_BLOCK = 256
MOBA_TOPK = 3
PAGE_SIZE = 128
H_GLA, DK_GLA, DV_GLA = 4, 64, 128
GLA_KW, GLA_VW = H_GLA * DK_GLA, H_GLA * DV_GLA
GLA_RANK = 16
GLA_TAU = 16.0
GLA_CHUNK = 64
GLA_SUB = 16
H_CROSS, HD_CROSS = 4, 128
CROSS_W = H_CROSS * HD_CROSS
N_BRANCH = 3
CONV_W = 3

LANES = 128
SUBLANES = 8
VMEM_LIMIT_BYTES = 56 * 1024 * 1024

MASK_BIAS = -1e30
MOBA_GROUP = 4
MOBA_Q_SCALE = HD_MOBA ** -0.5 * math.log2(math.e)
ROW_TILE = 512
SEQ_TILE = 256
PAGES_PER_STEP = 16
A_PAD = LANES

C_QM, C_K, C_V = 0, MOBA_W, 2 * MOBA_W
C_QG = 3 * MOBA_W
C_KG = C_QG + GLA_KW
C_VG = C_KG + GLA_KW
C_RG = C_VG + GLA_VW
C_QC = C_RG + GLA_VW
C_A = C_QC + CROSS_W
IN_COLS_PAD = C_A + A_PAD


def _params(n_axes):
    return pltpu.CompilerParams(
        dimension_semantics=("arbitrary",) * n_axes,
        vmem_limit_bytes=VMEM_LIMIT_BYTES,
    )


def _const_spec(shape):
    nd = len(shape)
    return pl.BlockSpec(shape, lambda *_: (0,) * nd, pipeline_mode=pl.Buffered(1))


def _rms(x, w):
    return x * lax.rsqrt(jnp.mean(x * x, axis=-1, keepdims=True) + EPS) * w


def _sigmoid(x):
    return 1.0 / (1.0 + jnp.exp(-x))


def _dot(a, b):
    return jnp.dot(a, b, preferred_element_type=F32)


def _dot_nt(a, b):
    return lax.dot_general(a, b, (((1,), (1,)), ((), ())), preferred_element_type=F32)


def _dot_tn(a, b):
    return lax.dot_general(a, b, (((0,), (0,)), ((), ())), preferred_element_type=F32)


def _proj_kernel(x_ref, nw_ref, w_ref, wa2_ref, ba_ref,
                 qm_ref, k_ref, v_ref, qg_ref, kg_ref, vg_ref, rg_ref, qc_ref, la_ref, ksum_ref,
                 kb_ref, vb_ref, *, qm_scale, kv_transposed):
    xb = _rms(x_ref[...], nw_ref[...]).astype(BF16)

    def mm(lo, hi):
        return _dot(xb, w_ref[:, lo:hi])

    def store_kv(ref, val):
        if kv_transposed:
            ref[0] = val.T
        else:
            ref[...] = val

    qm_ref[...] = (mm(C_QM, C_K) * qm_scale).astype(qm_ref.dtype)
    k = mm(C_K, C_V)
    store_kv(k_ref, k)
    kb_ref[...] = k.astype(BF16)
    for g in range(k.shape[0] // MOBA_BLOCK):
        ksum_ref[g] = jnp.sum(k[g * MOBA_BLOCK:(g + 1) * MOBA_BLOCK], axis=0, keepdims=True)
    v = mm(C_V, C_QG)
    store_kv(v_ref, v)
    vb_ref[...] = v.astype(BF16)
    qg_ref[...] = mm(C_QG, C_KG).astype(qg_ref.dtype)
    kg_ref[...] = mm(C_KG, C_VG).astype(kg_ref.dtype)
    vg_ref[...] = mm(C_VG, C_RG).astype(vg_ref.dtype)
    rg_ref[...] = mm(C_RG, C_QC)
    qc_ref[...] = mm(C_QC, C_A).astype(qc_ref.dtype)
    z = _dot(mm(C_A, IN_COLS_PAD).astype(BF16), wa2_ref[...]) + ba_ref[...]
    la_ref[...] = (jnp.minimum(z, 0.0) - jnp.log1p(jnp.exp(-jnp.abs(z)))) * (1.0 / GLA_TAU)


def _project(x2d, nw, w_in_p, wa2_p, ba, act_dtype, qm_scale, kv_seq_len=None):
    n, d = x2d.shape
    tm = min(ROW_TILE, n)
    assert n % tm == 0 and tm % MOBA_BLOCK == 0
    row = lambda w: pl.BlockSpec((tm, w), lambda i: (i, 0))
    widths = (MOBA_W, MOBA_W, MOBA_W, GLA_KW, GLA_KW, GLA_VW, GLA_VW, CROSS_W, GLA_KW)
    dtypes = (act_dtype, F32, F32, act_dtype, act_dtype, act_dtype, F32, act_dtype, F32)
    out_shape = [jax.ShapeDtypeStruct((n, w), dt) for w, dt in zip(widths, dtypes)]
    out_shape.append(jax.ShapeDtypeStruct((n // MOBA_BLOCK, 1, MOBA_W), F32))
    out_shape += [jax.ShapeDtypeStruct((n, MOBA_W), BF16)] * 2
    out_specs = [row(w) for w in widths]
    out_specs.append(pl.BlockSpec((tm // MOBA_BLOCK, 1, MOBA_W), lambda i: (i, 0, 0)))
    out_specs += [row(MOBA_W)] * 2
    if kv_seq_len is not None:
        assert kv_seq_len % tm == 0
        nt = kv_seq_len // tm
        for i in (1, 2):
            out_shape[i] = jax.ShapeDtypeStruct((n // kv_seq_len, MOBA_W, kv_seq_len), F32)
            out_specs[i] = pl.BlockSpec((1, MOBA_W, tm), lambda i: (i // nt, 0, i % nt))
    return pl.pallas_call(
        functools.partial(_proj_kernel, qm_scale=qm_scale, kv_transposed=kv_seq_len is not None),
        grid=(n // tm,),
        in_specs=[row(d), _const_spec((1, d)), _const_spec(w_in_p.shape),
                  _const_spec(wa2_p.shape), _const_spec((1, GLA_KW))],
        out_specs=out_specs,
        out_shape=out_shape,
        compiler_params=_params(1),
        name="proj",
    )(x2d, nw, w_in_p, wa2_p, ba)


def _memkv_kernel(m_ref, nw_ref, w_ref, mk_ref, mv_ref):
    mb = _rms(m_ref[...], nw_ref[...]).astype(BF16)
    mk_ref[...] = _dot(mb, w_ref[:, :CROSS_W])
    mv_ref[...] = _dot(mb, w_ref[:, CROSS_W:])


def _memory_kv(mem2d, nw, w_kv):
    n, d = mem2d.shape
    tm = min(ROW_TILE, n)
    assert n % tm == 0
    return pl.pallas_call(
        _memkv_kernel,
        grid=(n // tm,),
        in_specs=[pl.BlockSpec((tm, d), lambda i: (i, 0)), _const_spec((1, d)), _const_spec(w_kv.shape)],
        out_specs=[pl.BlockSpec((tm, CROSS_W), lambda i: (i, 0))] * 2,
        out_shape=[jax.ShapeDtypeStruct((n, CROSS_W), F32)] * 2,
        compiler_params=_params(1),
        name="memkv",
    )(mem2d, nw, w_kv)


def _select_topk(gate, idx, valid, axis):
    g = jnp.where(valid, gate, NEG_INF)
    picked = jnp.zeros(gate.shape, jnp.bool_)
    for _ in range(MOBA_TOPK):
        m = jnp.max(g, axis=axis, keepdims=True)
        first = jnp.min(jnp.where(g == m, idx, float(2 ** 24)), axis=axis, keepdims=True)
        pick = (idx == first) & (m > NEG_INF) & valid
        picked = picked | pick
        g = jnp.where(pick, NEG_INF, g)
    return picked


def _moba_prompt_kernel(q_ref, k_ref, v_ref, ksum_ref, o_ref, ka_sc, va_sc, s_sc):
    t = pl.program_id(2)
    tq = q_ref.shape[0]
    blk = MOBA_BLOCK
    nblk = ksum_ref.shape[0]
    seq_len = k_ref.shape[0]
    group = s_sc.shape[3] // blk
    n_heads = LANES // HD_MOBA
    spare_off = [((half + 1) % n_heads) * HD_MOBA for half in range(n_heads)]

    @pl.when(t == 0)
    def _():
        key_blk = lax.broadcasted_iota(jnp.int32, (seq_len, LANES), 0) // blk
        key_lane = lax.broadcasted_iota(jnp.int32, (seq_len, LANES), 1)
        k = k_ref[...]
        v = v_ref[...]
        for half in range(n_heads):
            in_head = (key_lane >= half * HD_MOBA) & (key_lane < (half + 1) * HD_MOBA)
            onehot = jnp.where(key_lane - spare_off[half] == key_blk, 1.0, 0.0).astype(BF16)
            ka_sc[half] = jnp.where(in_head, k, onehot)
            va_sc[half] = jnp.where(in_head, v, jnp.ones_like(v))

    q = q_ref[...]
    kmean = ksum_ref[...] * (1.0 / blk)
    km_hi = kmean.astype(BF16)
    km_lo = (kmean - km_hi.astype(F32)).astype(BF16)
    lane = lax.broadcasted_iota(jnp.int32, (1, LANES), 1)
    row = lax.broadcasted_iota(jnp.int32, (tq, blk), 0)
    col = lax.broadcasted_iota(jnp.int32, (tq, blk), 1)
    blk_id = lax.broadcasted_iota(jnp.int32, (nblk, tq), 0).astype(F32)
    place_row = lax.broadcasted_iota(jnp.int32, (nblk, LANES), 0)
    place_lane = lax.broadcasted_iota(jnp.int32, (nblk, LANES), 1)
    own = pl.ds(pl.multiple_of(t * blk, blk), blk)
    n_groups = (t + group - 1) // group
    heads = range(n_heads)

    in_head, q_aug, s_own = [], [], []
    for half in heads:
        off = spare_off[half]
        in_head.append((lane >= half * HD_MOBA) & (lane < (half + 1) * HD_MOBA))
        qh = jnp.where(in_head[half], q, jnp.zeros_like(q))
        gate_t = _dot_nt(km_hi, qh) + _dot_nt(km_lo, qh)
        picked_t = _select_topk(gate_t, blk_id, blk_id < t.astype(F32), axis=0)
        place = jnp.where(place_lane - off == place_row, 1.0, 0.0).astype(BF16)
        picked = _dot_tn(jnp.where(picked_t, 1.0, 0.0).astype(BF16), place)
        in_range = (lane >= off) & (lane < off + nblk)
        bias = jnp.where(in_range & (picked < 0.5), MASK_BIAS, 0.0)
        q_aug.append(qh + bias.astype(BF16))
        s_own.append(jnp.where(col <= row, _dot_nt(qh, ka_sc[half, own, :]), MASK_BIAS))

    def group_rows(g):
        return pl.ds(pl.multiple_of(g * (group * blk), group * blk), group * blk)

    def score_pass(g, mxs):
        out = []
        for half in heads:
            s = _dot_nt(q_aug[half], ka_sc[half, group_rows(g), :])
            s_sc[half, g] = s
            mx = mxs[half]
            for i in range(group):
                mx = jnp.maximum(mx, s[:, i * blk:(i + 1) * blk])
            out.append(mx)
        return tuple(out)

    mxs = lax.fori_loop(0, n_groups, score_pass, tuple(s_own))
    m = [jnp.max(mx, axis=-1, keepdims=True) for mx in mxs]

    def value_pass(g, accs):
        out = []
        for half in heads:
            p = jnp.exp2(s_sc[half, g] - m[half])
            out.append(accs[half] + _dot(p.astype(BF16), va_sc[half, group_rows(g), :]))
        return tuple(out)

    acc0 = tuple(_dot(jnp.exp2(s_own[half] - m[half]).astype(BF16), va_sc[half, own, :]) for half in heads)
    accs = lax.fori_loop(0, n_groups, value_pass, acc0)

    o = jnp.zeros((tq, LANES), F32)
    for half in heads:
        row_sum = accs[half][:, spare_off[half]:spare_off[half] + 1]
        o = jnp.where(in_head[half], accs[half] / row_sum, o)
    o_ref[...] = o.astype(o_ref.dtype)


def _moba_prompt(qm, kb, vb, ksum, n_seq, seq_len):
    assert seq_len % MOBA_BLOCK == 0
    nblk = seq_len // MOBA_BLOCK
    assert nblk <= HD_MOBA
    group = MOBA_GROUP if nblk % MOBA_GROUP == 0 else 1
    tq = MOBA_BLOCK
    n = n_seq * seq_len
    n_pair = MOBA_W // LANES
    n_heads = LANES // HD_MOBA
    seq_block = pl.BlockSpec((seq_len, LANES), lambda b, hp, t: (b, hp))
    return pl.pallas_call(
        _moba_prompt_kernel,
        grid=(n_seq, n_pair, nblk),
        in_specs=[
            pl.BlockSpec((tq, LANES), lambda b, hp, t: (b * nblk + t, hp)),
            seq_block, seq_block,
            pl.BlockSpec((nblk, LANES), lambda b, hp, t: (b, hp)),
        ],
        out_specs=pl.BlockSpec((tq, LANES), lambda b, hp, t: (b * nblk + t, hp)),
        out_shape=jax.ShapeDtypeStruct((n, MOBA_W), BF16),
        scratch_shapes=[pltpu.VMEM((n_heads, seq_len, LANES), BF16),
                        pltpu.VMEM((n_heads, seq_len, LANES), BF16),
                        pltpu.VMEM((n_heads, nblk // group, tq, group * MOBA_BLOCK), F32)],
        compiler_params=_params(3),
        name="moba_prompt",
    )(qm, kb, vb, ksum)


def _moba_sample_kernel(pt_ref, qbd_ref, kn_ref, vn_ref, *rest, n_q):
    del pt_ref
    pps = PAGES_PER_STEP
    k_pages, v_pages = rest[:pps], rest[pps:2 * pps]
    o_ref, m_sc, l_sc, gate_sc, acc_sc = rest[2 * pps:]
    s_id = pl.program_id(1)
    n_step = pl.num_programs(1)
    nblk = m_sc.shape[1]
    n_hq = qbd_ref.shape[1]
    qbd = qbd_ref[0]
    hq_row = lax.broadcasted_iota(jnp.int32, (n_hq, MOBA_W), 0)
    w_lane = lax.broadcasted_iota(jnp.int32, (n_hq, MOBA_W), 1)
    head_mask = (w_lane // HD_MOBA) == (hq_row // n_q)
    blk_col = lax.broadcasted_iota(jnp.int32, (n_hq, nblk), 1)
    pages_per_blk = MOBA_BLOCK // PAGE_SIZE

    for i in range(pps // pages_per_blk):
        pages = range(i * pages_per_blk, (i + 1) * pages_per_blk)
        kt = jnp.concatenate([k_pages[j][0] for j in pages], axis=1).astype(BF16)
        vt = jnp.concatenate([v_pages[j][0] for j in pages], axis=1).astype(BF16)
        b_idx = s_id * (pps // pages_per_blk) + i
        s = _dot(qbd, kt)
        gate = jnp.sum(s, axis=-1, keepdims=True)
        m = jnp.max(s, axis=-1, keepdims=True)
        p = jnp.exp(s - m)
        l = jnp.sum(p, axis=-1, keepdims=True)
        acc_sc[b_idx] = jnp.where(head_mask, _dot_nt(p.astype(BF16), vt), 0.0)
        m_sc[...] = jnp.where(blk_col == b_idx, m, m_sc[...])
        l_sc[...] = jnp.where(blk_col == b_idx, l, l_sc[...])
        gate_sc[...] = jnp.where(blk_col == b_idx, gate, gate_sc[...])

    @pl.when(s_id == n_step - 1)
    def _():
        picked = _select_topk(gate_sc[...], blk_col.astype(F32), blk_col >= 0, axis=1)
        m_all = jnp.where(picked, m_sc[...], NEG_INF)
        kn = kn_ref[0].astype(BF16)
        vn = vn_ref[0].astype(BF16)
        n_new = kn.shape[0]
        key_i = lax.broadcasted_iota(jnp.int32, (n_hq, n_new), 1)
        q_i = lax.broadcasted_iota(jnp.int32, (n_hq, n_new), 0) % n_q
        s_own = jnp.where(key_i <= q_i, _dot_nt(qbd, kn), NEG_INF)
        m_tot = jnp.maximum(jnp.max(s_own, axis=-1, keepdims=True),
                            jnp.max(m_all, axis=-1, keepdims=True))
        p_own = jnp.exp(s_own - m_tot)
        c = jnp.exp(m_all - m_tot)
        l_tot = jnp.sum(p_own, axis=-1, keepdims=True) + jnp.sum(c * l_sc[...], axis=-1, keepdims=True)
        out = jnp.where(head_mask, _dot(p_own.astype(BF16), vn), 0.0)
        for b in range(nblk):
            out = out + c[:, b:b + 1] * acc_sc[b]
        out = out / l_tot
        o = out[0:n_q]
        for h in range(1, H_MOBA):
            o = o + out[h * n_q:(h + 1) * n_q]
        o_ref[0] = o


def _moba_sample(q, k_new, v_new, cache_k, cache_v, page_table):
    n_seq, n_q, _ = q.shape
    n_phys = cache_k.shape[0]
    n_pages = page_table.shape[1]
    past = n_pages * PAGE_SIZE
    assert past % MOBA_BLOCK == 0 and n_pages % PAGES_PER_STEP == 0
    assert n_q <= MOBA_BLOCK and n_q % SUBLANES == 0
    nblk = past // MOBA_BLOCK
    n_hq = H_MOBA * n_q
    lane_head = jnp.arange(MOBA_W) // HD_MOBA
    qbd = jnp.where(lane_head[None, None, None, :] == jnp.arange(H_MOBA)[None, :, None, None],
                    q[:, None, :, :] * (HD_MOBA ** -0.5), 0.0).reshape(n_seq, n_hq, MOBA_W).astype(BF16)
    n_new = max(2 * SUBLANES, n_q)
    pad = ((0, 0), (0, n_new - n_q), (0, 0))
    kn = jnp.pad(k_new, pad)
    vn = jnp.pad(v_new, pad)
    to_t = lambda c: jnp.transpose(c, (0, 2, 3, 1)).reshape(n_phys, MOBA_W, PAGE_SIZE)

    def page_spec(i):
        return pl.BlockSpec((1, MOBA_W, PAGE_SIZE), lambda b, s, pt: (pt[b, s * PAGES_PER_STEP + i], 0, 0))

    seq3 = lambda r: pl.BlockSpec((1, r, MOBA_W), lambda b, s, pt: (b, 0, 0))
    grid_spec = pltpu.PrefetchScalarGridSpec(
        num_scalar_prefetch=1,
        grid=(n_seq, n_pages // PAGES_PER_STEP),
        in_specs=[seq3(n_hq), seq3(n_new), seq3(n_new)]
        + [page_spec(i) for i in range(PAGES_PER_STEP)] * 2,
        out_specs=seq3(n_q),
        scratch_shapes=[
            pltpu.VMEM((n_hq, nblk), F32),
            pltpu.VMEM((n_hq, nblk), F32),
            pltpu.VMEM((n_hq, nblk), F32),
            pltpu.VMEM((nblk, n_hq, MOBA_W), F32),
        ],
    )
    return pl.pallas_call(
        functools.partial(_moba_sample_kernel, n_q=n_q),
        grid_spec=grid_spec,
        out_shape=jax.ShapeDtypeStruct((n_seq, n_q, MOBA_W), F32),
        compiler_params=_params(2),
        name="moba_sample",
    )(page_table, qbd, kn, vn, *([to_t(cache_k)] * PAGES_PER_STEP), *([to_t(cache_v)] * PAGES_PER_STEP))


def _cumsum_rows(x):
    n = x.shape[0]
    row = lax.broadcasted_iota(jnp.int32, x.shape, 0)
    s = 1
    while s < n:
        x = x + jnp.where(row >= s, pltpu.roll(x, s, axis=0), 0.0)
        s *= 2
    return x


def _gla_chunk(q, k, v, la, st, cast):
    c = q.shape[0]
    sub = min(GLA_SUB, c)
    lane = lax.broadcasted_iota(jnp.int32, (1, GLA_KW), 1)
    head_masks = [(lane >= h * DK_GLA) & (lane < (h + 1) * DK_GLA) for h in range(H_GLA)]
    g = _cumsum_rows(la)
    krow = lax.broadcasted_iota(jnp.int32, (c, GLA_KW), 0)
    a_row = lax.broadcasted_iota(jnp.int32, (H_GLA * sub, c), 0)
    a_col = lax.broadcasted_iota(jnp.int32, (H_GLA * sub, c), 1)

    a_parts = []
    for i in range(c // sub):
        g_ref = jnp.zeros((1, GLA_KW), F32) if i == 0 else g[i * sub - 1:i * sub]
        qt = q[i * sub:(i + 1) * sub] * jnp.exp(g[i * sub:(i + 1) * sub] - g_ref)
        kt = jnp.where(krow < (i + 1) * sub, k * jnp.exp(g_ref - g), 0.0)
        q_stack = jnp.concatenate([jnp.where(hm, qt, 0.0) for hm in head_masks], axis=0)
        a = _dot_nt(cast(q_stack), cast(kt))
        a_parts.append(jnp.where(a_col <= i * sub + a_row % sub, a, 0.0))

    qe = q * jnp.exp(g)
    g_last = g[c - 1:c]
    k_dec = cast(k * jnp.exp(g_last - g))
    st_c = cast(st)
    outs = []
    st_new = st * jnp.exp(g_last)
    for h in range(H_GLA):
        a_h = jnp.concatenate([a[h * sub:(h + 1) * sub] for a in a_parts], axis=0)
        v_h = cast(v[:, h * DV_GLA:(h + 1) * DV_GLA])
        o_intra = _dot(cast(a_h), v_h)
        o_inter = _dot_nt(cast(jnp.where(head_masks[h], qe, 0.0)), st_c)
        outs.append(o_intra + o_inter)
        st_new = st_new + jnp.where(head_masks[h], _dot_tn(v_h, k_dec), 0.0)
    return outs, st_new


def _gla_kernel(q_ref, k_ref, v_ref, la_ref, r_ref, s0_ref, nw_ref, o_ref, sfin_ref, st_sc, *, chunk):
    t = pl.program_id(1)

    @pl.when(t == 0)
    def _():
        st_sc[...] = s0_ref[0]

    cast = (lambda a: a.astype(BF16)) if chunk >= 2 * SUBLANES else (lambda a: a)
    st = st_sc[...]
    for c in range(q_ref.shape[0] // chunk):
        rows = slice(c * chunk, (c + 1) * chunk)
        q = q_ref[rows, :].astype(F32) * (DK_GLA ** -0.5)
        k = k_ref[rows, :].astype(F32)
        outs, st = _gla_chunk(q, k, v_ref[rows, :], la_ref[rows, :], st, cast)
        for h, o in enumerate(outs):
            lanes = slice(h * DV_GLA, (h + 1) * DV_GLA)
            r = r_ref[rows, lanes]
            o = o * lax.rsqrt(jnp.mean(o * o, axis=-1, keepdims=True) + EPS)
            o_ref[rows, lanes] = (o * nw_ref[:, lanes] * (r * _sigmoid(r))).astype(o_ref.dtype)
    st_sc[...] = st

    @pl.when(t == pl.num_programs(1) - 1)
    def _():
        sfin_ref[0] = st


def _gla(qg, kg, vg, la, rg, state_t, nw, n_seq, seq_len, out_dtype):
    chunk = math.gcd(seq_len, GLA_CHUNK)
    tl = min(seq_len, SEQ_TILE)
    assert seq_len % tl == 0 and tl % chunk == 0
    nt = seq_len // tl
    n = n_seq * seq_len
    row = lambda w: pl.BlockSpec((tl, w), lambda b, t: (b * nt + t, 0))
    st_spec = pl.BlockSpec((1, DV_GLA, GLA_KW), lambda b, t: (b, 0, 0))
    return pl.pallas_call(
        functools.partial(_gla_kernel, chunk=chunk),
        grid=(n_seq, nt),
        in_specs=[row(GLA_KW), row(GLA_KW), row(GLA_VW), row(GLA_KW), row(GLA_VW), st_spec,
                  _const_spec((1, GLA_VW))],
        out_specs=[row(GLA_VW), st_spec],
        out_shape=[jax.ShapeDtypeStruct((n, GLA_VW), out_dtype),
                   jax.ShapeDtypeStruct((n_seq, DV_GLA, GLA_KW), F32)],
        scratch_shapes=[pltpu.VMEM((DV_GLA, GLA_KW), F32)],
        compiler_params=_params(2),
        name="gla",
    )(qg, kg, vg, la, rg, state_t, nw)


def _state_to_t(s):
    n = s.shape[0]
    return jnp.transpose(s, (0, 3, 1, 2)).reshape(n, DV_GLA, GLA_KW)


def _state_from_t(st):
    n = st.shape[0]
    return jnp.transpose(st.reshape(n, DV_GLA, H_GLA, DK_GLA), (0, 2, 3, 1))


def _cross_kernel(q_ref, mk_ref, mv_ref, o_ref):
    scale = HD_CROSS ** -0.5
    small = q_ref.shape[0] < 2 * SUBLANES
    cast = (lambda a: a) if small else (lambda a: a.astype(BF16))
    for h in range(H_CROSS):
        lanes = slice(h * HD_CROSS, (h + 1) * HD_CROSS)
        s = _dot_nt(cast(q_ref[:, lanes]), cast(mk_ref[:, lanes])) * scale
        m = jnp.max(s, axis=-1, keepdims=True)
        p = jnp.exp(s - m)
        l = jnp.sum(p, axis=-1, keepdims=True)
        o_ref[:, lanes] = (_dot(cast(p), cast(mv_ref[:, lanes])) / l).astype(o_ref.dtype)


def _cross(qc, mk, mv, n_seq, seq_len, n_mem, out_dtype):
    tl = min(seq_len, SEQ_TILE)
    nt = seq_len // tl
    row = pl.BlockSpec((tl, CROSS_W), lambda b, t: (b * nt + t, 0))
    mem = pl.BlockSpec((n_mem, CROSS_W), lambda b, t: (b, 0))
    return pl.pallas_call(
        _cross_kernel,
        grid=(n_seq, nt),
        in_specs=[row, mem, mem],
        out_specs=row,
        out_shape=jax.ShapeDtypeStruct((n_seq * seq_len, CROSS_W), out_dtype),
        compiler_params=_params(2),
        name="cross",
    )(qc, mk, mv)


def _merge_kernel(x_ref, om_ref, og_ref, oc_ref, nw_ref, wg_ref, bg_ref, wbm_ref, wbg_ref, wbc_ref,
                  wo_ref, h_ref):
    x = x_ref[...]
    d = x.shape[1]
    xb = _rms(x, nw_ref[...]).astype(BF16)
    merged = jnp.zeros(x.shape, F32)
    for i, (o_ref, wb_ref) in enumerate(((om_ref, wbm_ref), (og_ref, wbg_ref), (oc_ref, wbc_ref))):
        cols = slice(i * d, (i + 1) * d)
        gate = _sigmoid(_dot(xb, wg_ref[:, cols]) + bg_ref[:, cols])
        merged = merged + gate * _dot(o_ref[...].astype(BF16), wb_ref[...])
    h_ref[...] = x + _dot(merged.astype(BF16), wo_ref[...])


def _merge(x2d, om, og, oc, nw, wg, bg, wbm, wbg, wbc, wo):
    n, d = x2d.shape
    tm = min(ROW_TILE, n)
    row = lambda w: pl.BlockSpec((tm, w), lambda i: (i, 0))
    return pl.pallas_call(
        _merge_kernel,
        grid=(n // tm,),
        in_specs=[row(d), row(MOBA_W), row(GLA_VW), row(CROSS_W), _const_spec((1, d)),
                  _const_spec(wg.shape), _const_spec(bg.shape), _const_spec(wbm.shape),
                  _const_spec(wbg.shape), _const_spec(wbc.shape), _const_spec(wo.shape)],
        out_specs=row(d),
        out_shape=jax.ShapeDtypeStruct((n, d), F32),
        compiler_params=_params(1),
        name="merge",
    )(x2d, om, og, oc, nw, wg, bg, wbm, wbg, wbc, wo)


def _gelu_tanh(x):
    return 0.5 * x * (1.0 + jnp.tanh(0.7978845608028654 * (x + 0.044715 * (x * x * x))))


def _ffn_body(h_ref, nw_ref, wup_ref, wc_ref, bc_ref, wdn_ref, nf_ref, y_ref, prev_rows, store_u):
    h = h_ref[...]
    hb = _rms(h, nw_ref[...]).astype(BF16)
    d_ff = wdn_ref.shape[0]
    n_col_chunks = 2
    fc = d_ff // n_col_chunks
    assert fc % LANES == 0
    acc = jnp.zeros(h.shape, F32)
    for c in range(n_col_chunks):
        cols = slice(c * fc, (c + 1) * fc)
        u = _dot(hb, wup_ref[:, cols])
        gate = _dot(hb, wup_ref[:, d_ff + c * fc:d_ff + (c + 1) * fc])
        u1, u2 = prev_rows(cols, u)
        store_u(cols, u)
        conv = bc_ref[:, cols] + wc_ref[0:1, cols] * u2 + wc_ref[1:2, cols] * u1 + wc_ref[2:3, cols] * u
        act = (_gelu_tanh(conv) * gate).astype(BF16)
        acc = acc + _dot(act, wdn_ref[cols, :])
    y_ref[...] = _rms(h + acc, nf_ref[...])


def _ffn_long_kernel(h_ref, nw_ref, wup_ref, wc_ref, bc_ref, wdn_ref, nf_ref, y_ref, cs_ref, carry_sc):
    tm = h_ref.shape[0]

    @pl.when(pl.program_id(1) == 0)
    def _():
        carry_sc[...] = jnp.zeros(carry_sc.shape, F32)

    row = lax.broadcasted_iota(jnp.int32, (tm, 1), 0)

    def prev_rows(cols, u):
        last = carry_sc[SUBLANES - 1:SUBLANES, cols]
        last2 = carry_sc[SUBLANES - 2:SUBLANES - 1, cols]
        u1 = jnp.where(row == 0, last, pltpu.roll(u, 1, axis=0))
        u2 = jnp.where(row == 0, last2, jnp.where(row == 1, last, pltpu.roll(u, 2, axis=0)))
        return u1, u2

    def store_u(cols, u):
        carry_sc[:, cols] = u[tm - SUBLANES:tm]
        cs_ref[0, :, cols] = u[tm - (CONV_W - 1):tm]

    _ffn_body(h_ref, nw_ref, wup_ref, wc_ref, bc_ref, wdn_ref, nf_ref, y_ref, prev_rows, store_u)


def _ffn_short_kernel(h_ref, p1_ref, p2_ref, nw_ref, wup_ref, wc_ref, bc_ref, wdn_ref, nf_ref,
                      y_ref, u_ref, *, seq_len):
    tm = h_ref.shape[0]
    pos = lax.broadcasted_iota(jnp.int32, (tm, 1), 0) % seq_len

    def prev_rows(cols, u):
        u1 = jnp.where(pos >= 1, pltpu.roll(u, 1, axis=0), p1_ref[:, cols])
        u2 = jnp.where(pos >= 2, pltpu.roll(u, 2, axis=0), p2_ref[:, cols])
        return u1, u2

    def store_u(cols, u):
        u_ref[:, cols] = u

    _ffn_body(h_ref, nw_ref, wup_ref, wc_ref, bc_ref, wdn_ref, nf_ref, y_ref, prev_rows, store_u)


def _ffn_weights_specs(d, wup, wc, bc, wdn):
    return [_const_spec((1, d)), _const_spec(wup.shape), _const_spec(wc.shape), _const_spec(bc.shape),
            _const_spec(wdn.shape), _const_spec((1, d))]


def _ffn_long(h2d, n_seq, seq_len, nw, wup, wc, bc, wdn, nf):
    n, d = h2d.shape
    d_ff = wdn.shape[0]
    tm = min(ROW_TILE, n)
    assert seq_len % tm == 0
    nt = seq_len // tm
    row = pl.BlockSpec((tm, d), lambda b, t: (b * nt + t, 0))
    return pl.pallas_call(
        _ffn_long_kernel,
        grid=(n_seq, nt),
        in_specs=[row] + _ffn_weights_specs(d, wup, wc, bc, wdn),
        out_specs=[row, pl.BlockSpec((1, CONV_W - 1, d_ff), lambda b, t: (b, 0, 0))],
        out_shape=[jax.ShapeDtypeStruct((n, d), F32),
                   jax.ShapeDtypeStruct((n_seq, CONV_W - 1, d_ff), F32)],
        scratch_shapes=[pltpu.VMEM((SUBLANES, d_ff), F32)],
        compiler_params=_params(2),
        name="ffn_long",
    )(h2d, nw, wup, wc, bc, wdn, nf)


def _ffn_short(h2d, conv_prev, n_seq, seq_len, nw, wup, wc, bc, wdn, nf):
    n, d = h2d.shape
    d_ff = wdn.shape[0]
    tm = min(ROW_TILE, n)
    assert tm % seq_len == 0 and n % tm == 0 and seq_len >= CONV_W - 1
    zeros = jnp.zeros((n_seq, seq_len, d_ff), F32)
    p1 = zeros.at[:, 0].set(conv_prev[:, 1]).reshape(n, d_ff)
    p2 = zeros.at[:, 0].set(conv_prev[:, 0]).at[:, 1].set(conv_prev[:, 1]).reshape(n, d_ff)
    row = lambda w: pl.BlockSpec((tm, w), lambda i: (i, 0))
    y, u = pl.pallas_call(
        functools.partial(_ffn_short_kernel, seq_len=seq_len),
        grid=(n // tm,),
        in_specs=[row(d), row(d_ff), row(d_ff)] + _ffn_weights_specs(d, wup, wc, bc, wdn),
        out_specs=[row(d), row(d_ff)],
        out_shape=[jax.ShapeDtypeStruct((n, d), F32), jax.ShapeDtypeStruct((n, d_ff), F32)],
        compiler_params=_params(1),
        name="ffn_short",
    )(h2d, p1, p2, nw, wup, wc, bc, wdn, nf)
    return y, u.reshape(n_seq, seq_len, d_ff)[:, seq_len - (CONV_W - 1):]


def _prep_weights(norm_mix, w_in, w_gla_a2, b_gla_a, norm_gla, norm_mem, w_mem_kv, w_br_moba, w_br_gla,
                  w_br_cross, w_gate, b_gate, w_out, norm_ffn, w_up, w_conv, b_conv, w_down, norm_final):
    d = w_in.shape[0]
    o_a = C_QC
    w_in_p = jnp.concatenate(
        [w_in[:, :o_a], w_in[:, o_a + GLA_RANK:], w_in[:, o_a:o_a + GLA_RANK],
         jnp.zeros((d, A_PAD - GLA_RANK), w_in.dtype)], axis=1).astype(BF16)
    wa2_p = jnp.concatenate([w_gla_a2, jnp.zeros((A_PAD - GLA_RANK, GLA_KW), w_gla_a2.dtype)],
                            axis=0).astype(BF16)
    r2 = lambda a: a.reshape(1, -1)
    return dict(
        norm_mix=r2(norm_mix), w_in=w_in_p, wa2=wa2_p, ba=r2(b_gla_a), norm_gla=r2(norm_gla),
        norm_mem=r2(norm_mem), w_mem_kv=w_mem_kv.astype(BF16), wbm=w_br_moba.astype(BF16),
        wbg=w_br_gla.astype(BF16), wbc=w_br_cross.astype(BF16), wg=w_gate.astype(BF16), bg=r2(b_gate),
        wo=w_out.astype(BF16), norm_ffn=r2(norm_ffn), wup=w_up.astype(BF16), wc=w_conv, bc=r2(b_conv),
        wdn=w_down.astype(BF16), norm_final=r2(norm_final))


def kernel(x_prompt, x_sample, cache_moba_k, cache_moba_v, state_gla, state_conv, cache_mem_k, cache_mem_v, page_table, mem_prompt, norm_mix, w_in, w_gla_a2, b_gla_a, norm_gla, norm_mem, w_mem_kv, w_br_moba, w_br_gla, w_br_cross, w_gate, b_gate, w_out, norm_ffn, w_up, w_conv, b_conv, w_down, norm_final):
    depth = w_in.shape[0]
    assert depth == 1
    bp, sp, d = x_prompt.shape
    bs, ss, _ = x_sample.shape
    n_mem = mem_prompt.shape[1]
    w = _prep_weights(norm_mix[0], w_in[0], w_gla_a2[0], b_gla_a[0], norm_gla[0], norm_mem[0], w_mem_kv[0],
                      w_br_moba[0], w_br_gla[0], w_br_cross[0], w_gate[0], b_gate[0], w_out[0], norm_ffn[0],
                      w_up[0], w_conv[0], b_conv[0], w_down[0], norm_final)

    def mix(x2d, o_m, o_g, o_c):
        return _merge(x2d, o_m, o_g, o_c, w["norm_mix"], w["wg"], w["bg"], w["wbm"], w["wbg"], w["wbc"], w["wo"])

    ffn_w = (w["norm_ffn"], w["wup"], w["wc"], w["bc"], w["wdn"], w["norm_final"])

    xp = x_prompt.reshape(bp * sp, d)
    qm, kt_p, vt_p, qg, kg, vg, rg, qc, la, ksum, kb, vb = _project(
        xp, w["norm_mix"], w["w_in"], w["wa2"], w["ba"], BF16, MOBA_Q_SCALE, kv_seq_len=sp)
    o_m = _moba_prompt(qm, kb, vb, ksum.reshape(-1, MOBA_W), bp, sp)
    mk_p, mv_p = _memory_kv(mem_prompt.reshape(bp * n_mem, d), w["norm_mem"], w["w_mem_kv"])
    o_g, gla_p = _gla(qg, kg, vg, la, rg, jnp.zeros((bp, DV_GLA, GLA_KW), F32), w["norm_gla"], bp, sp, BF16)
    o_c = _cross(qc, mk_p, mv_p, bp, sp, n_mem, BF16)
    h_p = mix(xp, o_m, o_g, o_c)
    y_p, conv_p = _ffn_long(h_p, bp, sp, *ffn_w)

    xs = x_sample.reshape(bs * ss, d)
    qm, k_s, v_s, qg, kg, vg, rg, qc, la, _, _, _ = _project(
        xs, w["norm_mix"], w["w_in"], w["wa2"], w["ba"], F32, 1.0)
    r3 = lambda a: a.reshape(bs, ss, MOBA_W)
    o_m = _moba_sample(r3(qm), r3(k_s), r3(v_s), cache_moba_k[0], cache_moba_v[0],
                       page_table).reshape(bs * ss, MOBA_W)
    o_g, gla_s = _gla(qg, kg, vg, la, rg, _state_to_t(state_gla[0]), w["norm_gla"], bs, ss, F32)
    o_c = _cross(qc, cache_mem_k[0].reshape(bs * n_mem, CROSS_W), cache_mem_v[0].reshape(bs * n_mem, CROSS_W),
                 bs, ss, n_mem, F32)
    h_s = mix(xs, o_m, o_g, o_c)
    y_s, conv_s = _ffn_short(h_s, state_conv[0], bs, ss, *ffn_w)

    kv5 = lambda a, b, s: a.reshape(1, b, s, H_MOBA, HD_MOBA)
    kv5_t = lambda a: jnp.transpose(a.reshape(bp, H_MOBA, HD_MOBA, sp), (0, 3, 1, 2))[None]
    return (y_p.reshape(bp, sp, d), y_s.reshape(bs, ss, d),
            kv5_t(kt_p), kv5_t(vt_p), kv5(k_s, bs, ss), kv5(v_s, bs, ss),
            _state_from_t(gla_p)[None], _state_from_t(gla_s)[None],
            conv_p[None], conv_s[None],
            mk_p.reshape(1, bp, n_mem, H_CROSS, HD_CROSS), mv_p.reshape(1, bp, n_mem, H_CROSS, HD_CROSS))
```

```python
import functools
import math

import jax
import jax.numpy as jnp
from jax import lax
from jax.experimental import pallas as pl
from jax.experimental.pallas import tpu as pltpu

F32 = jnp.float32
BF16 = jnp.bfloat16
NEG_INF = float("-inf")

EPS = 1e-6
H_MOBA, HD_MOBA = 8, 64
MOBA_W = H_MOBA * HD_MOBA
MOBA_BLOCK = 256
MOBA_TOPK = 3
PAGE_SIZE = 128
H_GLA, DK_GLA, DV_GLA = 4, 64, 128
GLA_KW, GLA_VW = H_GLA * DK_GLA, H_GLA * DV_GLA
GLA_RANK = 16
GLA_TAU = 16.0
GLA_CHUNK = 64
GLA_SUB = 16
H_CROSS, HD_CROSS = 4, 128
CROSS_W = H_CROSS * HD_CROSS
N_BRANCH = 3
CONV_W = 3

LANES = 128
SUBLANES = 8
VMEM_LIMIT_BYTES = 56 * 1024 * 1024

MASK_BIAS = -1e30
MOBA_GROUP = 8
MOBA_Q_SCALE = HD_MOBA ** -0.5 * math.log2(math.e)
ROW_TILE = 512
SEQ_TILE = 256
PAGES_PER_STEP = 16
A_PAD = LANES

C_QM, C_K, C_V = 0, MOBA_W, 2 * MOBA_W
C_QG = 3 * MOBA_W
C_KG = C_QG + GLA_KW
C_VG = C_KG + GLA_KW
C_RG = C_VG + GLA_VW
C_QC = C_RG + GLA_VW
C_A = C_QC + CROSS_W
IN_COLS_PAD = C_A + A_PAD


def _params(n_axes):
    return pltpu.CompilerParams(
        dimension_semantics=("arbitrary",) * n_axes,
        vmem_limit_bytes=VMEM_LIMIT_BYTES,
    )


def _const_spec(shape):
    nd = len(shape)
    return pl.BlockSpec(shape, lambda *_: (0,) * nd, pipeline_mode=pl.Buffered(1))


def _rms(x, w):
    return x * lax.rsqrt(jnp.mean(x * x, axis=-1, keepdims=True) + EPS) * w


def _sigmoid(x):
    return 1.0 / (1.0 + jnp.exp(-x))


def _dot(a, b):
    return jnp.dot(a, b, preferred_element_type=F32)


def _dot_nt(a, b):
    return lax.dot_general(a, b, (((1,), (1,)), ((), ())), preferred_element_type=F32)


def _dot_tn(a, b):
    return lax.dot_general(a, b, (((0,), (0,)), ((), ())), preferred_element_type=F32)


def _proj_kernel(x_ref, nw_ref, w_ref, wa2_ref, ba_ref,
                 qm_ref, k_ref, v_ref, qg_ref, kg_ref, vg_ref, rg_ref, qc_ref, la_ref, ksum_ref,
                 kb_ref, vb_ref, *, qm_scale, kv_transposed):
    xb = _rms(x_ref[...], nw_ref[...]).astype(BF16)

    def mm(lo, hi):
        return _dot(xb, w_ref[:, lo:hi])

    def store_kv(ref, val):
        if kv_transposed:
            ref[0] = val.T
        else:
            ref[...] = val

    qm_ref[...] = (mm(C_QM, C_K) * qm_scale).astype(qm_ref.dtype)
    k = mm(C_K, C_V)
    store_kv(k_ref, k)
    kb_ref[...] = k.astype(BF16)
    for g in range(k.shape[0] // MOBA_BLOCK):
        ksum_ref[g] = jnp.sum(k[g * MOBA_BLOCK:(g + 1) * MOBA_BLOCK], axis=0, keepdims=True)
    v = mm(C_V, C_QG)
    store_kv(v_ref, v)
    vb_ref[...] = v.astype(BF16)
    qg_ref[...] = mm(C_QG, C_KG).astype(qg_ref.dtype)
    kg_ref[...] = mm(C_KG, C_VG).astype(kg_ref.dtype)
    vg_ref[...] = mm(C_VG, C_RG).astype(vg_ref.dtype)
    rg_ref[...] = mm(C_RG, C_QC)
    qc_ref[...] = mm(C_QC, C_A).astype(qc_ref.dtype)
    z = _dot(mm(C_A, IN_COLS_PAD).astype(BF16), wa2_ref[...]) + ba_ref[...]
    la_ref[...] = (jnp.minimum(z, 0.0) - jnp.log1p(jnp.exp(-jnp.abs(z)))) * (1.0 / GLA_TAU)


def _project(x2d, nw, w_in_p, wa2_p, ba, act_dtype, qm_scale, kv_seq_len=None):
    n, d = x2d.shape
    tm = min(ROW_TILE, n)
    assert n % tm == 0 and tm % MOBA_BLOCK == 0
    row = lambda w: pl.BlockSpec((tm, w), lambda i: (i, 0))
    widths = (MOBA_W, MOBA_W, MOBA_W, GLA_KW, GLA_KW, GLA_VW, GLA_VW, CROSS_W, GLA_KW)
    dtypes = (act_dtype, F32, F32, act_dtype, act_dtype, act_dtype, F32, act_dtype, F32)
    out_shape = [jax.ShapeDtypeStruct((n, w), dt) for w, dt in zip(widths, dtypes)]
    out_shape.append(jax.ShapeDtypeStruct((n // MOBA_BLOCK, 1, MOBA_W), F32))
    out_shape += [jax.ShapeDtypeStruct((n, MOBA_W), BF16)] * 2
    out_specs = [row(w) for w in widths]
    out_specs.append(pl.BlockSpec((tm // MOBA_BLOCK, 1, MOBA_W), lambda i: (i, 0, 0)))
    out_specs += [row(MOBA_W)] * 2
    if kv_seq_len is not None:
        assert kv_seq_len % tm == 0
        nt = kv_seq_len // tm
        for i in (1, 2):
            out_shape[i] = jax.ShapeDtypeStruct((n // kv_seq_len, MOBA_W, kv_seq_len), F32)
            out_specs[i] = pl.BlockSpec((1, MOBA_W, tm), lambda i: (i // nt, 0, i % nt))
    return pl.pallas_call(
        functools.partial(_proj_kernel, qm_scale=qm_scale, kv_transposed=kv_seq_len is not None),
        grid=(n // tm,),
        in_specs=[row(d), _const_spec((1, d)), _const_spec(w_in_p.shape),
                  _const_spec(wa2_p.shape), _const_spec((1, GLA_KW))],
        out_specs=out_specs,
        out_shape=out_shape,
        compiler_params=_params(1),
        name="proj",
    )(x2d, nw, w_in_p, wa2_p, ba)


def _memkv_kernel(m_ref, nw_ref, w_ref, mk_ref, mv_ref):
    mb = _rms(m_ref[...], nw_ref[...]).astype(BF16)
    mk_ref[...] = _dot(mb, w_ref[:, :CROSS_W])
    mv_ref[...] = _dot(mb, w_ref[:, CROSS_W:])


def _memory_kv(mem2d, nw, w_kv):
    n, d = mem2d.shape
    tm = min(ROW_TILE, n)
    assert n % tm == 0
    return pl.pallas_call(
        _memkv_kernel,
        grid=(n // tm,),
        in_specs=[pl.BlockSpec((tm, d), lambda i: (i, 0)), _const_spec((1, d)), _const_spec(w_kv.shape)],
        out_specs=[pl.BlockSpec((tm, CROSS_W), lambda i: (i, 0))] * 2,
        out_shape=[jax.ShapeDtypeStruct((n, CROSS_W), F32)] * 2,
        compiler_params=_params(1),
        name="memkv",
    )(mem2d, nw, w_kv)


def _select_topk(gate, idx, valid, axis):
    g = jnp.where(valid, gate, NEG_INF)
    picked = jnp.zeros(gate.shape, jnp.bool_)
    for _ in range(MOBA_TOPK):
        m = jnp.max(g, axis=axis, keepdims=True)
        first = jnp.min(jnp.where(g == m, idx, float(2 ** 24)), axis=axis, keepdims=True)
        pick = (idx == first) & (m > NEG_INF) & valid
        picked = picked | pick
        g = jnp.where(pick, NEG_INF, g)
    return picked


def _moba_prompt_kernel(q_ref, k_ref, v_ref, ksum_ref, o_ref, ka_sc, va_sc, s_sc):
    t = pl.program_id(2)
    tq = q_ref.shape[0]
    blk = MOBA_BLOCK
    nblk = ksum_ref.shape[0]
    seq_len = k_ref.shape[0]
    group = s_sc.shape[3] // blk
    n_heads = LANES // HD_MOBA
    spare_off = [((half + 1) % n_heads) * HD_MOBA for half in range(n_heads)]

    @pl.when(t == 0)
    def _():
        key_blk = lax.broadcasted_iota(jnp.int32, (seq_len, LANES), 0) // blk
        key_lane = lax.broadcasted_iota(jnp.int32, (seq_len, LANES), 1)
        k = k_ref[...]
        v = v_ref[...]
        for half in range(n_heads):
            in_head = (key_lane >= half * HD_MOBA) & (key_lane < (half + 1) * HD_MOBA)
            onehot = jnp.where(key_lane - spare_off[half] == key_blk, 1.0, 0.0).astype(BF16)
            ka_sc[half] = jnp.where(in_head, k, onehot)
            va_sc[half] = jnp.where(in_head, v, jnp.ones_like(v))

    q = q_ref[...]
    kmean = ksum_ref[...] * (1.0 / blk)
    km_hi = kmean.astype(BF16)
    km_lo = (kmean - km_hi.astype(F32)).astype(BF16)
    lane = lax.broadcasted_iota(jnp.int32, (1, LANES), 1)
    row = lax.broadcasted_iota(jnp.int32, (tq, blk), 0)
    col = lax.broadcasted_iota(jnp.int32, (tq, blk), 1)
    blk_id = lax.broadcasted_iota(jnp.int32, (nblk, tq), 0).astype(F32)
    place_row = lax.broadcasted_iota(jnp.int32, (nblk, LANES), 0)
    place_lane = lax.broadcasted_iota(jnp.int32, (nblk, LANES), 1)
    own = pl.ds(pl.multiple_of(t * blk, blk), blk)
    n_groups = (t + group - 1) // group
    heads = range(n_heads)

    in_head, q_aug, s_own = [], [], []
    for half in heads:
        off = spare_off[half]
        in_head.append((lane >= half * HD_MOBA) & (lane < (half + 1) * HD_MOBA))
        qh = jnp.where(in_head[half], q, jnp.zeros_like(q))
        gate_t = _dot_nt(km_hi, qh) + _dot_nt(km_lo, qh)
        picked_t = _select_topk(gate_t, blk_id, blk_id < t.astype(F32), axis=0)
        place = jnp.where(place_lane - off == place_row, 1.0, 0.0).astype(BF16)
        picked = _dot_tn(jnp.where(picked_t, 1.0, 0.0).astype(BF16), place)
        in_range = (lane >= off) & (lane < off + nblk)
        bias = jnp.where(in_range & (picked < 0.5), MASK_BIAS, 0.0)
        q_aug.append(qh + bias.astype(BF16))
        s_own.append(jnp.where(col <= row, _dot_nt(qh, ka_sc[half, own, :]), MASK_BIAS))

    def group_rows(g):
        return pl.ds(pl.multiple_of(g * (group * blk), group * blk), group * blk)

    def score_pass(g, mxs):
        out = []
        for half in heads:
            s = _dot_nt(q_aug[half], ka_sc[half, group_rows(g), :])
            s_sc[half, g] = s
            mx = mxs[half]
            for i in range(group):
                mx = jnp.maximum(mx, s[:, i * blk:(i + 1) * blk])
            out.append(mx)
        return tuple(out)

    mxs = lax.fori_loop(0, n_groups, score_pass, tuple(s_own))
    m = [jnp.max(mx, axis=-1, keepdims=True) for mx in mxs]

    def value_pass(g, accs):
        out = []
        for half in heads:
            p = jnp.exp2(s_sc[half, g] - m[half])
            out.append(accs[half] + _dot(p.astype(BF16), va_sc[half, group_rows(g), :]))
        return tuple(out)

    acc0 = tuple(_dot(jnp.exp2(s_own[half] - m[half]).astype(BF16), va_sc[half, own, :]) for half in heads)
    accs = lax.fori_loop(0, n_groups, value_pass, acc0)

    o = jnp.zeros((tq, LANES), F32)
    for half in heads:
        row_sum = accs[half][:, spare_off[half]:spare_off[half] + 1]
        o = jnp.where(in_head[half], accs[half] / row_sum, o)
    o_ref[...] = o.astype(o_ref.dtype)


def _moba_prompt(qm, kb, vb, ksum, n_seq, seq_len):
    assert seq_len % MOBA_BLOCK == 0
    nblk = seq_len // MOBA_BLOCK
    assert nblk <= HD_MOBA
    group = MOBA_GROUP if nblk % MOBA_GROUP == 0 else 1
    tq = MOBA_BLOCK
    n = n_seq * seq_len
    n_pair = MOBA_W // LANES
    n_heads = LANES // HD_MOBA
    seq_block = pl.BlockSpec((seq_len, LANES), lambda b, hp, t: (b, hp))
    return pl.pallas_call(
        _moba_prompt_kernel,
        grid=(n_seq, n_pair, nblk),
        in_specs=[
            pl.BlockSpec((tq, LANES), lambda b, hp, t: (b * nblk + t, hp)),
            seq_block, seq_block,
            pl.BlockSpec((nblk, LANES), lambda b, hp, t: (b, hp)),
        ],
        out_specs=pl.BlockSpec((tq, LANES), lambda b, hp, t: (b * nblk + t, hp)),
        out_shape=jax.ShapeDtypeStruct((n, MOBA_W), BF16),
        scratch_shapes=[pltpu.VMEM((n_heads, seq_len, LANES), BF16),
                        pltpu.VMEM((n_heads, seq_len, LANES), BF16),
                        pltpu.VMEM((n_heads, nblk // group, tq, group * MOBA_BLOCK), F32)],
        compiler_params=_params(3),
        name="moba_prompt",
    )(qm, kb, vb, ksum)


def _moba_sample_kernel(pt_ref, qbd_ref, kn_ref, vnt_ref, *rest, n_q):
    del pt_ref
    pps = PAGES_PER_STEP
    k_pages, v_pages = rest[:pps], rest[pps:2 * pps]
    o_ref, m_sc, l_sc, gate_sc, acc_sc = rest[2 * pps:]
    s_id = pl.program_id(1)
    n_step = pl.num_programs(1)
    nblk = m_sc.shape[1]
    n_hq = qbd_ref.shape[1]
    qbd = qbd_ref[0]
    col_head = lax.broadcasted_iota(jnp.int32, (HD_MOBA, n_hq), 1) // n_q
    blk_col = lax.broadcasted_iota(jnp.int32, (n_hq, nblk), 1)
    pages_per_blk = MOBA_BLOCK // PAGE_SIZE

    def own_head(pvt):
        out = pvt[0:HD_MOBA]
        for h in range(1, H_MOBA):
            out = jnp.where(col_head == h, pvt[h * HD_MOBA:(h + 1) * HD_MOBA], out)
        return out

    for i in range(pps // pages_per_blk):
        pages = range(i * pages_per_blk, (i + 1) * pages_per_blk)
        kt = jnp.concatenate([k_pages[j][0] for j in pages], axis=1).astype(BF16)
        vt = jnp.concatenate([v_pages[j][0] for j in pages], axis=1).astype(BF16)
        b_idx = s_id * (pps // pages_per_blk) + i
        s = _dot(qbd, kt)
        gate = jnp.sum(s, axis=-1, keepdims=True)
        m = jnp.max(s, axis=-1, keepdims=True)
        p = jnp.exp(s - m)
        l = jnp.sum(p, axis=-1, keepdims=True)
        acc_sc[b_idx] = own_head(_dot_nt(vt, p.astype(BF16)))
        m_sc[...] = jnp.where(blk_col == b_idx, m, m_sc[...])
        l_sc[...] = jnp.where(blk_col == b_idx, l, l_sc[...])
        gate_sc[...] = jnp.where(blk_col == b_idx, gate, gate_sc[...])

    @pl.when(s_id == n_step - 1)
    def _():
        picked = _select_topk(gate_sc[...], blk_col.astype(F32), blk_col >= 0, axis=1)
        m_all = jnp.where(picked, m_sc[...], NEG_INF)
        kn = kn_ref[0].astype(BF16)
        vnt = vnt_ref[0].astype(BF16)
        n_new = kn.shape[0]
        key_i = lax.broadcasted_iota(jnp.int32, (n_hq, n_new), 1)
        q_i = lax.broadcasted_iota(jnp.int32, (n_hq, n_new), 0) % n_q
        s_own = jnp.where(key_i <= q_i, _dot_nt(qbd, kn), NEG_INF)
        m_tot = jnp.maximum(jnp.max(s_own, axis=-1, keepdims=True),
                            jnp.max(m_all, axis=-1, keepdims=True))
        p_own = jnp.exp(s_own - m_tot)
        c = jnp.exp(m_all - m_tot)
        l_tot = jnp.sum(p_own, axis=-1, keepdims=True) + jnp.sum(c * l_sc[...], axis=-1, keepdims=True)
        inv_l = 1.0 / l_tot
        c_t = jnp.transpose(c * inv_l)
        out = own_head(_dot_nt(vnt, (p_own * inv_l).astype(BF16)))
        for b in range(nblk):
            out = out + acc_sc[b] * c_t[b:b + 1, :]
        o_ref[0] = out


def _moba_sample(q, k_new, v_new, cache_k, cache_v, page_table):
    n_seq, n_q, _ = q.shape
    n_phys = cache_k.shape[0]
    n_pages = page_table.shape[1]
    past = n_pages * PAGE_SIZE
    assert past % MOBA_BLOCK == 0 and n_pages % PAGES_PER_STEP == 0
    assert n_q <= MOBA_BLOCK and n_q % SUBLANES == 0
    nblk = past // MOBA_BLOCK
    n_hq = H_MOBA * n_q
    lane_head = jnp.arange(MOBA_W) // HD_MOBA
    qbd = jnp.where(lane_head[None, None, None, :] == jnp.arange(H_MOBA)[None, :, None, None],
                    q[:, None, :, :] * (HD_MOBA ** -0.5), 0.0).reshape(n_seq, n_hq, MOBA_W).astype(BF16)
    n_new = max(2 * SUBLANES, n_q)
    pad = ((0, 0), (0, n_new - n_q), (0, 0))
    kn = jnp.pad(k_new, pad)
    vnt = jnp.transpose(jnp.pad(v_new, pad), (0, 2, 1))
    to_t = lambda c: jnp.transpose(c, (0, 2, 3, 1)).reshape(n_phys, MOBA_W, PAGE_SIZE)

    def page_spec(i):
        return pl.BlockSpec((1, MOBA_W, PAGE_SIZE), lambda b, s, pt: (pt[b, s * PAGES_PER_STEP + i], 0, 0))

    seq3 = lambda r, c: pl.BlockSpec((1, r, c), lambda b, s, pt: (b, 0, 0))
    grid_spec = pltpu.PrefetchScalarGridSpec(
        num_scalar_prefetch=1,
        grid=(n_seq, n_pages // PAGES_PER_STEP),
        in_specs=[seq3(n_hq, MOBA_W), seq3(n_new, MOBA_W), seq3(MOBA_W, n_new)]
        + [page_spec(i) for i in range(PAGES_PER_STEP)] * 2,
        out_specs=seq3(HD_MOBA, n_hq),
        scratch_shapes=[
            pltpu.VMEM((n_hq, nblk), F32),
            pltpu.VMEM((n_hq, nblk), F32),
            pltpu.VMEM((n_hq, nblk), F32),
            pltpu.VMEM((nblk, HD_MOBA, n_hq), F32),
        ],
    )
    out = pl.pallas_call(
        functools.partial(_moba_sample_kernel, n_q=n_q),
        grid_spec=grid_spec,
        out_shape=jax.ShapeDtypeStruct((n_seq, HD_MOBA, n_hq), F32),
        compiler_params=_params(2),
        name="moba_sample",
    )(page_table, qbd, kn, vnt, *([to_t(cache_k)] * PAGES_PER_STEP), *([to_t(cache_v)] * PAGES_PER_STEP))
    return jnp.transpose(out.reshape(n_seq, HD_MOBA, H_MOBA, n_q), (0, 3, 2, 1)).reshape(n_seq, n_q, MOBA_W)


def _cumsum_rows(x):
    n = x.shape[0]
    row = lax.broadcasted_iota(jnp.int32, x.shape, 0)
    s = 1
    while s < n:
        x = x + jnp.where(row >= s, pltpu.roll(x, s, axis=0), 0.0)
        s *= 2
    return x


def _gla_chunk(q, k, v, la, st, cast):
    c = q.shape[0]
    sub = min(GLA_SUB, c)
    lane = lax.broadcasted_iota(jnp.int32, (1, GLA_KW), 1)
    head_masks = [(lane >= h * DK_GLA) & (lane < (h + 1) * DK_GLA) for h in range(H_GLA)]
    g = _cumsum_rows(la)
    krow = lax.broadcasted_iota(jnp.int32, (c, GLA_KW), 0)
    a_row = lax.broadcasted_iota(jnp.int32, (H_GLA * sub, c), 0)
    a_col = lax.broadcasted_iota(jnp.int32, (H_GLA * sub, c), 1)

    a_parts = []
    for i in range(c // sub):
        g_ref = jnp.zeros((1, GLA_KW), F32) if i == 0 else g[i * sub - 1:i * sub]
        qt = q[i * sub:(i + 1) * sub] * jnp.exp(g[i * sub:(i + 1) * sub] - g_ref)
        kt = jnp.where(krow < (i + 1) * sub, k * jnp.exp(g_ref - g), 0.0)
        q_stack = jnp.concatenate([jnp.where(hm, qt, 0.0) for hm in head_masks], axis=0)
        a = _dot_nt(cast(q_stack), cast(kt))
        a_parts.append(jnp.where(a_col <= i * sub + a_row % sub, a, 0.0))

    qe = q * jnp.exp(g)
    g_last = g[c - 1:c]
    k_dec = cast(k * jnp.exp(g_last - g))
    st_c = cast(st)
    outs = []
    st_new = st * jnp.exp(g_last)
    for h in range(H_GLA):
        a_h = jnp.concatenate([a[h * sub:(h + 1) * sub] for a in a_parts], axis=0)
        v_h = cast(v[:, h * DV_GLA:(h + 1) * DV_GLA])
        o_intra = _dot(cast(a_h), v_h)
        o_inter = _dot_nt(cast(jnp.where(head_masks[h], qe, 0.0)), st_c)
        outs.append(o_intra + o_inter)
        st_new = st_new + jnp.where(head_masks[h], _dot_tn(v_h, k_dec), 0.0)
    return outs, st_new


def _gla_kernel(q_ref, k_ref, v_ref, la_ref, r_ref, s0_ref, nw_ref, o_ref, sfin_ref, st_sc, *, chunk):
    t = pl.program_id(1)

    @pl.when(t == 0)
    def _():
        st_sc[...] = s0_ref[0]

    cast = (lambda a: a.astype(BF16)) if chunk >= 2 * SUBLANES else (lambda a: a)
    st = st_sc[...]
    for c in range(q_ref.shape[0] // chunk):
        rows = slice(c * chunk, (c + 1) * chunk)
        q = q_ref[rows, :].astype(F32) * (DK_GLA ** -0.5)
        k = k_ref[rows, :].astype(F32)
        outs, st = _gla_chunk(q, k, v_ref[rows, :], la_ref[rows, :], st, cast)
        for h, o in enumerate(outs):
            lanes = slice(h * DV_GLA, (h + 1) * DV_GLA)
            r = r_ref[rows, lanes]
            o = o * lax.rsqrt(jnp.mean(o * o, axis=-1, keepdims=True) + EPS)
            o_ref[rows, lanes] = (o * nw_ref[:, lanes] * (r * _sigmoid(r))).astype(o_ref.dtype)
    st_sc[...] = st

    @pl.when(t == pl.num_programs(1) - 1)
    def _():
        sfin_ref[0] = st


def _gla(qg, kg, vg, la, rg, state_t, nw, n_seq, seq_len, out_dtype):
    chunk = math.gcd(seq_len, GLA_CHUNK)
    tl = min(seq_len, SEQ_TILE)
    assert seq_len % tl == 0 and tl % chunk == 0
    nt = seq_len // tl
    n = n_seq * seq_len
    row = lambda w: pl.BlockSpec((tl, w), lambda b, t: (b * nt + t, 0))
    st_spec = pl.BlockSpec((1, DV_GLA, GLA_KW), lambda b, t: (b, 0, 0))
    return pl.pallas_call(
        functools.partial(_gla_kernel, chunk=chunk),
        grid=(n_seq, nt),
        in_specs=[row(GLA_KW), row(GLA_KW), row(GLA_VW), row(GLA_KW), row(GLA_VW), st_spec,
                  _const_spec((1, GLA_VW))],
        out_specs=[row(GLA_VW), st_spec],
        out_shape=[jax.ShapeDtypeStruct((n, GLA_VW), out_dtype),
                   jax.ShapeDtypeStruct((n_seq, DV_GLA, GLA_KW), F32)],
        scratch_shapes=[pltpu.VMEM((DV_GLA, GLA_KW), F32)],
        compiler_params=_params(2),
        name="gla",
    )(qg, kg, vg, la, rg, state_t, nw)


def _state_to_t(s):
    n = s.shape[0]
    return jnp.transpose(s, (0, 3, 1, 2)).reshape(n, DV_GLA, GLA_KW)


def _state_from_t(st):
    n = st.shape[0]
    return jnp.transpose(st.reshape(n, DV_GLA, H_GLA, DK_GLA), (0, 2, 3, 1))


def _cross_kernel(q_ref, mk_ref, mv_ref, o_ref):
    scale = HD_CROSS ** -0.5
    small = q_ref.shape[0] < 2 * SUBLANES
    cast = (lambda a: a) if small else (lambda a: a.astype(BF16))
    for h in range(H_CROSS):
        lanes = slice(h * HD_CROSS, (h + 1) * HD_CROSS)
        s = _dot_nt(cast(q_ref[:, lanes]), cast(mk_ref[:, lanes])) * scale
        m = jnp.max(s, axis=-1, keepdims=True)
        p = jnp.exp(s - m)
        l = jnp.sum(p, axis=-1, keepdims=True)
        o_ref[:, lanes] = (_dot(cast(p), cast(mv_ref[:, lanes])) / l).astype(o_ref.dtype)


def _cross(qc, mk, mv, n_seq, seq_len, n_mem, out_dtype):
    tl = min(seq_len, SEQ_TILE)
    nt = seq_len // tl
    row = pl.BlockSpec((tl, CROSS_W), lambda b, t: (b * nt + t, 0))
    mem = pl.BlockSpec((n_mem, CROSS_W), lambda b, t: (b, 0))
    return pl.pallas_call(
        _cross_kernel,
        grid=(n_seq, nt),
        in_specs=[row, mem, mem],
        out_specs=row,
        out_shape=jax.ShapeDtypeStruct((n_seq * seq_len, CROSS_W), out_dtype),
        compiler_params=_params(2),
        name="cross",
    )(qc, mk, mv)


def _merge_kernel(x_ref, om_ref, og_ref, oc_ref, nw_ref, wg_ref, bg_ref, wbm_ref, wbg_ref, wbc_ref,
                  wo_ref, h_ref):
    x = x_ref[...]
    d = x.shape[1]
    xb = _rms(x, nw_ref[...]).astype(BF16)
    merged = jnp.zeros(x.shape, F32)
    for i, (o_ref, wb_ref) in enumerate(((om_ref, wbm_ref), (og_ref, wbg_ref), (oc_ref, wbc_ref))):
        cols = slice(i * d, (i + 1) * d)
        gate = _sigmoid(_dot(xb, wg_ref[:, cols]) + bg_ref[:, cols])
        merged = merged + gate * _dot(o_ref[...].astype(BF16), wb_ref[...])
    h_ref[...] = x + _dot(merged.astype(BF16), wo_ref[...])


def _merge(x2d, om, og, oc, nw, wg, bg, wbm, wbg, wbc, wo):
    n, d = x2d.shape
    tm = min(ROW_TILE, n)
    row = lambda w: pl.BlockSpec((tm, w), lambda i: (i, 0))
    return pl.pallas_call(
        _merge_kernel,
        grid=(n // tm,),
        in_specs=[row(d), row(MOBA_W), row(GLA_VW), row(CROSS_W), _const_spec((1, d)),
                  _const_spec(wg.shape), _const_spec(bg.shape), _const_spec(wbm.shape),
                  _const_spec(wbg.shape), _const_spec(wbc.shape), _const_spec(wo.shape)],
        out_specs=row(d),
        out_shape=jax.ShapeDtypeStruct((n, d), F32),
        compiler_params=_params(1),
        name="merge",
    )(x2d, om, og, oc, nw, wg, bg, wbm, wbg, wbc, wo)


def _gelu_tanh(x):
    return 0.5 * x * (1.0 + jnp.tanh(0.7978845608028654 * (x + 0.044715 * (x * x * x))))


def _ffn_body(h_ref, nw_ref, wup_ref, wc_ref, bc_ref, wdn_ref, nf_ref, y_ref, prev_rows, store_u):
    h = h_ref[...]
    hb = _rms(h, nw_ref[...]).astype(BF16)
    d_ff = wdn_ref.shape[0]
    n_col_chunks = 2
    fc = d_ff // n_col_chunks
    assert fc % LANES == 0
    acc = jnp.zeros(h.shape, F32)
    for c in range(n_col_chunks):
        cols = slice(c * fc, (c + 1) * fc)
        u = _dot(hb, wup_ref[:, cols])
        gate = _dot(hb, wup_ref[:, d_ff + c * fc:d_ff + (c + 1) * fc])
        u1, u2 = prev_rows(cols, u)
        store_u(cols, u)
        conv = bc_ref[:, cols] + wc_ref[0:1, cols] * u2 + wc_ref[1:2, cols] * u1 + wc_ref[2:3, cols] * u
        act = (_gelu_tanh(conv) * gate).astype(BF16)
        acc = acc + _dot(act, wdn_ref[cols, :])
    y_ref[...] = _rms(h + acc, nf_ref[...])


def _ffn_long_kernel(h_ref, nw_ref, wup_ref, wc_ref, bc_ref, wdn_ref, nf_ref, y_ref, cs_ref, carry_sc):
    tm = h_ref.shape[0]

    @pl.when(pl.program_id(1) == 0)
    def _():
        carry_sc[...] = jnp.zeros(carry_sc.shape, F32)

    row = lax.broadcasted_iota(jnp.int32, (tm, 1), 0)

    def prev_rows(cols, u):
        last = carry_sc[SUBLANES - 1:SUBLANES, cols]
        last2 = carry_sc[SUBLANES - 2:SUBLANES - 1, cols]
        u1 = jnp.where(row == 0, last, pltpu.roll(u, 1, axis=0))
        u2 = jnp.where(row == 0, last2, jnp.where(row == 1, last, pltpu.roll(u, 2, axis=0)))
        return u1, u2

    def store_u(cols, u):
        carry_sc[:, cols] = u[tm - SUBLANES:tm]
        cs_ref[0, :, cols] = u[tm - (CONV_W - 1):tm]

    _ffn_body(h_ref, nw_ref, wup_ref, wc_ref, bc_ref, wdn_ref, nf_ref, y_ref, prev_rows, store_u)


def _ffn_short_kernel(h_ref, p1_ref, p2_ref, nw_ref, wup_ref, wc_ref, bc_ref, wdn_ref, nf_ref,
                      y_ref, u_ref, *, seq_len):
    tm = h_ref.shape[0]
    pos = lax.broadcasted_iota(jnp.int32, (tm, 1), 0) % seq_len

    def prev_rows(cols, u):
        u1 = jnp.where(pos >= 1, pltpu.roll(u, 1, axis=0), p1_ref[:, cols])
        u2 = jnp.where(pos >= 2, pltpu.roll(u, 2, axis=0), p2_ref[:, cols])
        return u1, u2

    def store_u(cols, u):
        u_ref[:, cols] = u

    _ffn_body(h_ref, nw_ref, wup_ref, wc_ref, bc_ref, wdn_ref, nf_ref, y_ref, prev_rows, store_u)


def _ffn_weights_specs(d, wup, wc, bc, wdn):
    return [_const_spec((1, d)), _const_spec(wup.shape), _const_spec(wc.shape), _const_spec(bc.shape),
            _const_spec(wdn.shape), _const_spec((1, d))]


def _ffn_long(h2d, n_seq, seq_len, nw, wup, wc, bc, wdn, nf):
    n, d = h2d.shape
    d_ff = wdn.shape[0]
    tm = min(ROW_TILE, n)
    assert seq_len % tm == 0
    nt = seq_len // tm
    row = pl.BlockSpec((tm, d), lambda b, t: (b * nt + t, 0))
    return pl.pallas_call(
        _ffn_long_kernel,
        grid=(n_seq, nt),
        in_specs=[row] + _ffn_weights_specs(d, wup, wc, bc, wdn),
        out_specs=[row, pl.BlockSpec((1, CONV_W - 1, d_ff), lambda b, t: (b, 0, 0))],
        out_shape=[jax.ShapeDtypeStruct((n, d), F32),
                   jax.ShapeDtypeStruct((n_seq, CONV_W - 1, d_ff), F32)],
        scratch_shapes=[pltpu.VMEM((SUBLANES, d_ff), F32)],
        compiler_params=_params(2),
        name="ffn_long",
    )(h2d, nw, wup, wc, bc, wdn, nf)


def _ffn_short(h2d, conv_prev, n_seq, seq_len, nw, wup, wc, bc, wdn, nf):
    n, d = h2d.shape
    d_ff = wdn.shape[0]
    tm = min(ROW_TILE, n)
    assert tm % seq_len == 0 and n % tm == 0 and seq_len >= CONV_W - 1
    zeros = jnp.zeros((n_seq, seq_len, d_ff), F32)
    p1 = zeros.at[:, 0].set(conv_prev[:, 1]).reshape(n, d_ff)
    p2 = zeros.at[:, 0].set(conv_prev[:, 0]).at[:, 1].set(conv_prev[:, 1]).reshape(n, d_ff)
    row = lambda w: pl.BlockSpec((tm, w), lambda i: (i, 0))
    y, u = pl.pallas_call(
        functools.partial(_ffn_short_kernel, seq_len=seq_len),
        grid=(n // tm,),
        in_specs=[row(d), row(d_ff), row(d_ff)] + _ffn_weights_specs(d, wup, wc, bc, wdn),
        out_specs=[row(d), row(d_ff)],
        out_shape=[jax.ShapeDtypeStruct((n, d), F32), jax.ShapeDtypeStruct((n, d_ff), F32)],
        compiler_params=_params(1),
        name="ffn_short",
    )(h2d, p1, p2, nw, wup, wc, bc, wdn, nf)
    return y, u.reshape(n_seq, seq_len, d_ff)[:, seq_len - (CONV_W - 1):]


def _prep_weights(norm_mix, w_in, w_gla_a2, b_gla_a, norm_gla, norm_mem, w_mem_kv, w_br_moba, w_br_gla,
                  w_br_cross, w_gate, b_gate, w_out, norm_ffn, w_up, w_conv, b_conv, w_down, norm_final):
    d = w_in.shape[0]
    o_a = C_QC
    w_in_p = jnp.concatenate(
        [w_in[:, :o_a], w_in[:, o_a + GLA_RANK:], w_in[:, o_a:o_a + GLA_RANK],
         jnp.zeros((d, A_PAD - GLA_RANK), w_in.dtype)], axis=1).astype(BF16)
    wa2_p = jnp.concatenate([w_gla_a2, jnp.zeros((A_PAD - GLA_RANK, GLA_KW), w_gla_a2.dtype)],
                            axis=0).astype(BF16)
    r2 = lambda a: a.reshape(1, -1)
    return dict(
        norm_mix=r2(norm_mix), w_in=w_in_p, wa2=wa2_p, ba=r2(b_gla_a), norm_gla=r2(norm_gla),
        norm_mem=r2(norm_mem), w_mem_kv=w_mem_kv.astype(BF16), wbm=w_br_moba.astype(BF16),
        wbg=w_br_gla.astype(BF16), wbc=w_br_cross.astype(BF16), wg=w_gate.astype(BF16), bg=r2(b_gate),
        wo=w_out.astype(BF16), norm_ffn=r2(norm_ffn), wup=w_up.astype(BF16), wc=w_conv, bc=r2(b_conv),
        wdn=w_down.astype(BF16), norm_final=r2(norm_final))


def kernel(x_prompt, x_sample, cache_moba_k, cache_moba_v, state_gla, state_conv, cache_mem_k, cache_mem_v, page_table, mem_prompt, norm_mix, w_in, w_gla_a2, b_gla_a, norm_gla, norm_mem, w_mem_kv, w_br_moba, w_br_gla, w_br_cross, w_gate, b_gate, w_out, norm_ffn, w_up, w_conv, b_conv, w_down, norm_final):
    depth = w_in.shape[0]
    assert depth == 1
    bp, sp, d = x_prompt.shape
    bs, ss, _ = x_sample.shape
    n_mem = mem_prompt.shape[1]
    w = _prep_weights(norm_mix[0], w_in[0], w_gla_a2[0], b_gla_a[0], norm_gla[0], norm_mem[0], w_mem_kv[0],
                      w_br_moba[0], w_br_gla[0], w_br_cross[0], w_gate[0], b_gate[0], w_out[0], norm_ffn[0],
                      w_up[0], w_conv[0], b_conv[0], w_down[0], norm_final)

    def mix(x2d, o_m, o_g, o_c):
        return _merge(x2d, o_m, o_g, o_c, w["norm_mix"], w["wg"], w["bg"], w["wbm"], w["wbg"], w["wbc"], w["wo"])

    ffn_w = (w["norm_ffn"], w["wup"], w["wc"], w["bc"], w["wdn"], w["norm_final"])

    xp = x_prompt.reshape(bp * sp, d)
    qm, kt_p, vt_p, qg, kg, vg, rg, qc, la, ksum, kb, vb = _project(
        xp, w["norm_mix"], w["w_in"], w["wa2"], w["ba"], BF16, MOBA_Q_SCALE, kv_seq_len=sp)
    o_m = _moba_prompt(qm, kb, vb, ksum.reshape(-1, MOBA_W), bp, sp)
    mk_p, mv_p = _memory_kv(mem_prompt.reshape(bp * n_mem, d), w["norm_mem"], w["w_mem_kv"])
    o_g, gla_p = _gla(qg, kg, vg, la, rg, jnp.zeros((bp, DV_GLA, GLA_KW), F32), w["norm_gla"], bp, sp, BF16)
    o_c = _cross(qc, mk_p, mv_p, bp, sp, n_mem, BF16)
    h_p = mix(xp, o_m, o_g, o_c)
    y_p, conv_p = _ffn_long(h_p, bp, sp, *ffn_w)

    xs = x_sample.reshape(bs * ss, d)
    qm, k_s, v_s, qg, kg, vg, rg, qc, la, _, _, _ = _project(
        xs, w["norm_mix"], w["w_in"], w["wa2"], w["ba"], F32, 1.0)
    r3 = lambda a: a.reshape(bs, ss, MOBA_W)
    o_m = _moba_sample(r3(qm), r3(k_s), r3(v_s), cache_moba_k[0], cache_moba_v[0],
                       page_table).reshape(bs * ss, MOBA_W)
    o_g, gla_s = _gla(qg, kg, vg, la, rg, _state_to_t(state_gla[0]), w["norm_gla"], bs, ss, F32)
    o_c = _cross(qc, cache_mem_k[0].reshape(bs * n_mem, CROSS_W), cache_mem_v[0].reshape(bs * n_mem, CROSS_W),
                 bs, ss, n_mem, F32)
    h_s = mix(xs, o_m, o_g, o_c)
    y_s, conv_s = _ffn_short(h_s, state_conv[0], bs, ss, *ffn_w)

    kv5 = lambda a, b, s: a.reshape(1, b, s, H_MOBA, HD_MOBA)
    kv5_t = lambda a: jnp.transpose(a.reshape(bp, H_MOBA, HD_MOBA, sp), (0, 3, 1, 2))[None]
    return (y_p.reshape(bp, sp, d), y_s.reshape(bs, ss, d),
            kv5_t(kt_p), kv5_t(vt_p), kv5(k_s, bs, ss), kv5(v_s, bs, ss),
            _state_from_t(gla_p)[None], _state_from_t(gla_s)[None],
            conv_p[None], conv_s[None],
            mk_p.reshape(1, bp, n_mem, H_CROSS, HD_CROSS), mv_p.reshape(1, bp, n_mem, H_CROSS, HD_CROSS))
```

```python
import functools
import math

import jax
import jax.numpy as jnp
from jax import lax
from jax.experimental import pallas as pl
from jax.experimental.pallas import tpu as pltpu

F32 = jnp.float32
BF16 = jnp.bfloat16
NEG_INF = float("-inf")

EPS = 1e-6
H_MOBA, HD_MOBA = 8, 64
MOBA_W = H_MOBA * HD_MOBA
MOBA_BLOCK = 256
MOBA_TOPK = 3
PAGE_SIZE = 128
H_GLA, DK_GLA, DV_GLA = 4, 64, 128
GLA_KW, GLA_VW = H_GLA * DK_GLA, H_GLA * DV_GLA
GLA_RANK = 16
GLA_TAU = 16.0
GLA_CHUNK = 64
GLA_SUB = 16
H_CROSS, HD_CROSS = 4, 128
CROSS_W = H_CROSS * HD_CROSS
N_BRANCH = 3
CONV_W = 3

LANES = 128
SUBLANES = 8
VMEM_LIMIT_BYTES = 56 * 1024 * 1024

MASK_BIAS = -1e30
MOBA_GROUP = 8
MOBA_Q_SCALE = HD_MOBA ** -0.5 * math.log2(math.e)
ROW_TILE = 512
SEQ_TILE = 256
GLA_STEP_CHUNKS = 8
CROSS_STEP_ROWS = 512
CROSS_MAX_GROUP = 8
PAGES_PER_STEP = 16
A_PAD = LANES

C_QM, C_K, C_V = 0, MOBA_W, 2 * MOBA_W
C_QG = 3 * MOBA_W
C_KG = C_QG + GLA_KW
C_VG = C_KG + GLA_KW
C_RG = C_VG + GLA_VW
C_QC = C_RG + GLA_VW
C_A = C_QC + CROSS_W
IN_COLS_PAD = C_A + A_PAD


def _params(n_axes):
    return pltpu.CompilerParams(
        dimension_semantics=("arbitrary",) * n_axes,
        vmem_limit_bytes=VMEM_LIMIT_BYTES,
    )


def _const_spec(shape):
    nd = len(shape)
    return pl.BlockSpec(shape, lambda *_: (0,) * nd, pipeline_mode=pl.Buffered(1))


def _rms(x, w):
    return x * lax.rsqrt(jnp.mean(x * x, axis=-1, keepdims=True) + EPS) * w


def _sigmoid(x):
    return 1.0 / (1.0 + jnp.exp(-x))


def _dot(a, b):
    return jnp.dot(a, b, preferred_element_type=F32)


def _dot_nt(a, b):
    return lax.dot_general(a, b, (((1,), (1,)), ((), ())), preferred_element_type=F32)


def _dot_tn(a, b):
    return lax.dot_general(a, b, (((0,), (0,)), ((), ())), preferred_element_type=F32)


def _proj_kernel(x_ref, nw_ref, w_ref, wa2_ref, ba_ref,
                 qm_ref, k_ref, v_ref, qg_ref, kg_ref, vg_ref, rg_ref, qc_ref, la_ref, ksum_ref,
                 kb_ref, vb_ref, *, qm_scale, kv_transposed):
    xb = _rms(x_ref[...], nw_ref[...]).astype(BF16)

    def mm(lo, hi):
        return _dot(xb, w_ref[:, lo:hi])

    def store_kv(ref, val):
        if kv_transposed:
            ref[0] = val.T
        else:
            ref[...] = val

    qm_ref[...] = (mm(C_QM, C_K) * qm_scale).astype(qm_ref.dtype)
    k = mm(C_K, C_V)
    store_kv(k_ref, k)
    kb_ref[...] = k.astype(BF16)
    for g in range(k.shape[0] // MOBA_BLOCK):
        ksum_ref[g] = jnp.sum(k[g * MOBA_BLOCK:(g + 1) * MOBA_BLOCK], axis=0, keepdims=True)
    v = mm(C_V, C_QG)
    store_kv(v_ref, v)
    vb_ref[...] = v.astype(BF16)
    qg_ref[...] = mm(C_QG, C_KG).astype(qg_ref.dtype)
    kg_ref[...] = mm(C_KG, C_VG).astype(kg_ref.dtype)
    vg_ref[...] = mm(C_VG, C_RG).astype(vg_ref.dtype)
    rg_ref[...] = mm(C_RG, C_QC)
    qc_ref[...] = mm(C_QC, C_A).astype(qc_ref.dtype)
    z = _dot(mm(C_A, IN_COLS_PAD).astype(BF16), wa2_ref[...]) + ba_ref[...]
    la_ref[...] = (jnp.minimum(z, 0.0) - jnp.log1p(jnp.exp(-jnp.abs(z)))) * (1.0 / GLA_TAU)


def _project(x2d, nw, w_in_p, wa2_p, ba, act_dtype, qm_scale, kv_seq_len=None):
    n, d = x2d.shape
    tm = min(ROW_TILE, n)
    assert n % tm == 0 and tm % MOBA_BLOCK == 0
    row = lambda w: pl.BlockSpec((tm, w), lambda i: (i, 0))
    widths = (MOBA_W, MOBA_W, MOBA_W, GLA_KW, GLA_KW, GLA_VW, GLA_VW, CROSS_W, GLA_KW)
    dtypes = (act_dtype, F32, F32, act_dtype, act_dtype, act_dtype, F32, act_dtype, F32)
    out_shape = [jax.ShapeDtypeStruct((n, w), dt) for w, dt in zip(widths, dtypes)]
    out_shape.append(jax.ShapeDtypeStruct((n // MOBA_BLOCK, 1, MOBA_W), F32))
    out_shape += [jax.ShapeDtypeStruct((n, MOBA_W), BF16)] * 2
    out_specs = [row(w) for w in widths]
    out_specs.append(pl.BlockSpec((tm // MOBA_BLOCK, 1, MOBA_W), lambda i: (i, 0, 0)))
    out_specs += [row(MOBA_W)] * 2
    if kv_seq_len is not None:
        assert kv_seq_len % tm == 0
        nt = kv_seq_len // tm
        for i in (1, 2):
            out_shape[i] = jax.ShapeDtypeStruct((n // kv_seq_len, MOBA_W, kv_seq_len), F32)
            out_specs[i] = pl.BlockSpec((1, MOBA_W, tm), lambda i: (i // nt, 0, i % nt))
    return pl.pallas_call(
        functools.partial(_proj_kernel, qm_scale=qm_scale, kv_transposed=kv_seq_len is not None),
        grid=(n // tm,),
        in_specs=[row(d), _const_spec((1, d)), _const_spec(w_in_p.shape),
                  _const_spec(wa2_p.shape), _const_spec((1, GLA_KW))],
        out_specs=out_specs,
        out_shape=out_shape,
        compiler_params=_params(1),
        name="proj",
    )(x2d, nw, w_in_p, wa2_p, ba)


def _memkv_kernel(m_ref, nw_ref, w_ref, mk_ref, mv_ref):
    mb = _rms(m_ref[...], nw_ref[...]).astype(BF16)
    mk_ref[...] = _dot(mb, w_ref[:, :CROSS_W])
    mv_ref[...] = _dot(mb, w_ref[:, CROSS_W:])


def _memory_kv(mem2d, nw, w_kv):
    n, d = mem2d.shape
    tm = min(ROW_TILE, n)
    assert n % tm == 0
    return pl.pallas_call(
        _memkv_kernel,
        grid=(n // tm,),
        in_specs=[pl.BlockSpec((tm, d), lambda i: (i, 0)), _const_spec((1, d)), _const_spec(w_kv.shape)],
        out_specs=[pl.BlockSpec((tm, CROSS_W), lambda i: (i, 0))] * 2,
        out_shape=[jax.ShapeDtypeStruct((n, CROSS_W), F32)] * 2,
        compiler_params=_params(1),
        name="memkv",
    )(mem2d, nw, w_kv)


def _select_topk(gate, idx, valid, axis):
    return _select_topk_each([gate], idx, valid, axis)[0]


def _select_topk_each(gates, idx, valid, axis):
    gs = [jnp.where(valid, gate, NEG_INF) for gate in gates]
    picked = [jnp.zeros(gate.shape, jnp.bool_) for gate in gates]
    for _ in range(MOBA_TOPK):
        ms = [jnp.max(g, axis=axis, keepdims=True) for g in gs]
        firsts = [jnp.min(jnp.where(g == m, idx, float(2 ** 24)), axis=axis, keepdims=True) for g, m in zip(gs, ms)]
        picks = [(idx == first) & (m > NEG_INF) & valid for first, m in zip(firsts, ms)]
        picked = [a | b for a, b in zip(picked, picks)]
        gs = [jnp.where(pick, NEG_INF, g) for pick, g in zip(picks, gs)]
    return picked


def _moba_prompt_kernel(q_ref, k_ref, v_ref, ksum_ref, o_ref, ka_sc, va_sc, s_sc):
    t = pl.program_id(2)
    tq = q_ref.shape[0]
    blk = MOBA_BLOCK
    nblk = ksum_ref.shape[0]
    seq_len = k_ref.shape[0]
    group = s_sc.shape[3] // blk
    n_heads = LANES // HD_MOBA
    spare_off = [((half + 1) % n_heads) * HD_MOBA for half in range(n_heads)]

    @pl.when(t == 0)
    def _():
        key_blk = lax.broadcasted_iota(jnp.int32, (seq_len, LANES), 0) // blk
        key_lane = lax.broadcasted_iota(jnp.int32, (seq_len, LANES), 1)
        k = k_ref[...]
        v = v_ref[...]
        for half in range(n_heads):
            in_head = (key_lane >= half * HD_MOBA) & (key_lane < (half + 1) * HD_MOBA)
            onehot = jnp.where(key_lane - spare_off[half] == key_blk, 1.0, 0.0).astype(BF16)
            ka_sc[half] = jnp.where(in_head, k, onehot)
            va_sc[half] = jnp.where(in_head, v, jnp.ones_like(v))

    q = q_ref[...]
    kmean = ksum_ref[...] * (1.0 / blk)
    km_hi = kmean.astype(BF16)
    km_lo = (kmean - km_hi.astype(F32)).astype(BF16)
    lane = lax.broadcasted_iota(jnp.int32, (1, LANES), 1)
    row = lax.broadcasted_iota(jnp.int32, (tq, blk), 0)
    col = lax.broadcasted_iota(jnp.int32, (tq, blk), 1)
    blk_id = lax.broadcasted_iota(jnp.int32, (nblk, tq), 0).astype(F32)
    place_row = lax.broadcasted_iota(jnp.int32, (nblk, LANES), 0)
    place_lane = lax.broadcasted_iota(jnp.int32, (nblk, LANES), 1)
    own = pl.ds(pl.multiple_of(t * blk, blk), blk)
    n_groups = (t + group - 1) // group
    heads = range(n_heads)

    in_head = [(lane >= half * HD_MOBA) & (lane < (half + 1) * HD_MOBA) for half in heads]
    qh = [jnp.where(in_head[half], q, jnp.zeros_like(q)) for half in heads]
    gate_t = [_dot_nt(km_hi, qh[half]) + _dot_nt(km_lo, qh[half]) for half in heads]
    s_own = [jnp.where(col <= row, _dot_nt(qh[half], ka_sc[half, own, :]), MASK_BIAS) for half in heads]
    picked_t = _select_topk_each(gate_t, blk_id, blk_id < t.astype(F32), axis=0)
    place = [jnp.where(place_lane - spare_off[half] == place_row, 1.0, 0.0).astype(BF16) for half in heads]
    picked = [_dot_tn(jnp.where(picked_t[half], 1.0, 0.0).astype(BF16), place[half]) for half in heads]
    q_aug = []
    for half in heads:
        in_range = (lane >= spare_off[half]) & (lane < spare_off[half] + nblk)
        bias = jnp.where(in_range & (picked[half] < 0.5), MASK_BIAS, 0.0)
        q_aug.append(qh[half] + bias.astype(BF16))

    def group_rows(g):
        return pl.ds(pl.multiple_of(g * (group * blk), group * blk), group * blk)

    def score_pass(g, mxs):
        out = []
        for half in heads:
            s = _dot_nt(q_aug[half], ka_sc[half, group_rows(g), :])
            s_sc[half, g] = s
            mx = mxs[half]
            for i in range(group):
                mx = jnp.maximum(mx, s[:, i * blk:(i + 1) * blk])
            out.append(mx)
        return tuple(out)

    mxs = lax.fori_loop(0, n_groups, score_pass, tuple(s_own))
    m = [jnp.max(mx, axis=-1, keepdims=True) for mx in mxs]

    def value_pass(g, accs):
        out = []
        for half in heads:
            p = jnp.exp2(s_sc[half, g] - m[half])
            out.append(accs[half] + _dot(p.astype(BF16), va_sc[half, group_rows(g), :]))
        return tuple(out)

    acc0 = tuple(_dot(jnp.exp2(s_own[half] - m[half]).astype(BF16), va_sc[half, own, :]) for half in heads)
    accs = lax.fori_loop(0, n_groups, value_pass, acc0)

    o = jnp.zeros((tq, LANES), F32)
    for half in heads:
        row_sum = accs[half][:, spare_off[half]:spare_off[half] + 1]
        o = jnp.where(in_head[half], accs[half] / row_sum, o)
    o_ref[...] = o.astype(o_ref.dtype)


def _moba_prompt(qm, kb, vb, ksum, n_seq, seq_len):
    assert seq_len % MOBA_BLOCK == 0
    nblk = seq_len // MOBA_BLOCK
    assert nblk <= HD_MOBA
    group = MOBA_GROUP if nblk % MOBA_GROUP == 0 else 1
    tq = MOBA_BLOCK
    n = n_seq * seq_len
    n_pair = MOBA_W // LANES
    n_heads = LANES // HD_MOBA
    seq_block = pl.BlockSpec((seq_len, LANES), lambda b, hp, t: (b, hp))
    return pl.pallas_call(
        _moba_prompt_kernel,
        grid=(n_seq, n_pair, nblk),
        in_specs=[
            pl.BlockSpec((tq, LANES), lambda b, hp, t: (b * nblk + t, hp)),
            seq_block, seq_block,
            pl.BlockSpec((nblk, LANES), lambda b, hp, t: (b, hp)),
        ],
        out_specs=pl.BlockSpec((tq, LANES), lambda b, hp, t: (b * nblk + t, hp)),
        out_shape=jax.ShapeDtypeStruct((n, MOBA_W), BF16),
        scratch_shapes=[pltpu.VMEM((n_heads, seq_len, LANES), BF16),
                        pltpu.VMEM((n_heads, seq_len, LANES), BF16),
                        pltpu.VMEM((n_heads, nblk // group, tq, group * MOBA_BLOCK), F32)],
        compiler_params=_params(3),
        name="moba_prompt",
    )(qm, kb, vb, ksum)


def _moba_sample_kernel(pt_ref, qbd_ref, kn_ref, vnt_ref, *rest, n_q):
    del pt_ref
    pps = PAGES_PER_STEP
    k_pages, v_pages = rest[:pps], rest[pps:2 * pps]
    o_ref, m_sc, l_sc, gate_sc, acc_sc = rest[2 * pps:]
    s_id = pl.program_id(1)
    n_step = pl.num_programs(1)
    nblk = m_sc.shape[1]
    n_hq = qbd_ref.shape[1]
    qbd = qbd_ref[0]
    col_head = lax.broadcasted_iota(jnp.int32, (HD_MOBA, n_hq), 1) // n_q
    blk_col = lax.broadcasted_iota(jnp.int32, (n_hq, nblk), 1)
    pages_per_blk = MOBA_BLOCK // PAGE_SIZE

    def own_head(pvt):
        out = pvt[0:HD_MOBA]
        for h in range(1, H_MOBA):
            out = jnp.where(col_head == h, pvt[h * HD_MOBA:(h + 1) * HD_MOBA], out)
        return out

    def block_t(page_refs, i):
        pages = range(i * pages_per_blk, (i + 1) * pages_per_blk)
        return jnp.concatenate([page_refs[j][0] for j in pages], axis=1).astype(BF16)

    blocks = range(pps // pages_per_blk)
    b_idx = [s_id * (pps // pages_per_blk) + i for i in blocks]
    s = [_dot(qbd, block_t(k_pages, i)) for i in blocks]
    gate = [jnp.sum(si, axis=-1, keepdims=True) for si in s]
    m = [jnp.max(si, axis=-1, keepdims=True) for si in s]
    p = [jnp.exp(si - mi) for si, mi in zip(s, m)]
    l = [jnp.sum(pi, axis=-1, keepdims=True) for pi in p]
    pv = [own_head(_dot_nt(block_t(v_pages, i), p[i].astype(BF16))) for i in blocks]
    for i in blocks:
        acc_sc[b_idx[i]] = pv[i]
    for ref, vals in ((m_sc, m), (l_sc, l), (gate_sc, gate)):
        cur = ref[...]
        for i in blocks:
            cur = jnp.where(blk_col == b_idx[i], vals[i], cur)
        ref[...] = cur

    @pl.when(s_id == n_step - 1)
    def _():
        picked = _select_topk(gate_sc[...], blk_col.astype(F32), blk_col >= 0, axis=1)
        m_all = jnp.where(picked, m_sc[...], NEG_INF)
        kn = kn_ref[0].astype(BF16)
        vnt = vnt_ref[0].astype(BF16)
        n_new = kn.shape[0]
        key_i = lax.broadcasted_iota(jnp.int32, (n_hq, n_new), 1)
        q_i = lax.broadcasted_iota(jnp.int32, (n_hq, n_new), 0) % n_q
        s_own = jnp.where(key_i <= q_i, _dot_nt(qbd, kn), NEG_INF)
        m_tot = jnp.maximum(jnp.max(s_own, axis=-1, keepdims=True),
                            jnp.max(m_all, axis=-1, keepdims=True))
        p_own = jnp.exp(s_own - m_tot)
        c = jnp.exp(m_all - m_tot)
        l_tot = jnp.sum(p_own, axis=-1, keepdims=True) + jnp.sum(c * l_sc[...], axis=-1, keepdims=True)
        inv_l = 1.0 / l_tot
        c_t = jnp.transpose(c * inv_l)
        out = own_head(_dot_nt(vnt, (p_own * inv_l).astype(BF16)))
        for b in range(nblk):
            out = out + acc_sc[b] * c_t[b:b + 1, :]
        o_ref[0] = out


def _moba_sample(q, k_new, v_new, cache_k, cache_v, page_table):
    n_seq, n_q, _ = q.shape
    n_phys = cache_k.shape[0]
    n_pages = page_table.shape[1]
    past = n_pages * PAGE_SIZE
    assert past % MOBA_BLOCK == 0 and n_pages % PAGES_PER_STEP == 0
    assert n_q <= MOBA_BLOCK and n_q % SUBLANES == 0
    nblk = past // MOBA_BLOCK
    n_hq = H_MOBA * n_q
    lane_head = jnp.arange(MOBA_W) // HD_MOBA
    qbd = jnp.where(lane_head[None, None, None, :] == jnp.arange(H_MOBA)[None, :, None, None],
                    q[:, None, :, :] * (HD_MOBA ** -0.5), 0.0).reshape(n_seq, n_hq, MOBA_W).astype(BF16)
    n_new = max(2 * SUBLANES, n_q)
    pad = ((0, 0), (0, n_new - n_q), (0, 0))
    kn = jnp.pad(k_new, pad)
    vnt = jnp.transpose(jnp.pad(v_new, pad), (0, 2, 1))
    to_t = lambda c: jnp.transpose(c, (0, 2, 3, 1)).reshape(n_phys, MOBA_W, PAGE_SIZE)

    def page_spec(i):
        return pl.BlockSpec((1, MOBA_W, PAGE_SIZE), lambda b, s, pt: (pt[b, s * PAGES_PER_STEP + i], 0, 0))

    seq3 = lambda r, c: pl.BlockSpec((1, r, c), lambda b, s, pt: (b, 0, 0))
    grid_spec = pltpu.PrefetchScalarGridSpec(
        num_scalar_prefetch=1,
        grid=(n_seq, n_pages // PAGES_PER_STEP),
        in_specs=[seq3(n_hq, MOBA_W), seq3(n_new, MOBA_W), seq3(MOBA_W, n_new)]
        + [page_spec(i) for i in range(PAGES_PER_STEP)] * 2,
        out_specs=seq3(HD_MOBA, n_hq),
        scratch_shapes=[
            pltpu.VMEM((n_hq, nblk), F32),
            pltpu.VMEM((n_hq, nblk), F32),
            pltpu.VMEM((n_hq, nblk), F32),
            pltpu.VMEM((nblk, HD_MOBA, n_hq), F32),
        ],
    )
    out = pl.pallas_call(
        functools.partial(_moba_sample_kernel, n_q=n_q),
        grid_spec=grid_spec,
        out_shape=jax.ShapeDtypeStruct((n_seq, HD_MOBA, n_hq), F32),
        compiler_params=_params(2),
        name="moba_sample",
    )(page_table, qbd, kn, vnt, *([to_t(cache_k)] * PAGES_PER_STEP), *([to_t(cache_v)] * PAGES_PER_STEP))
    return jnp.transpose(out.reshape(n_seq, HD_MOBA, H_MOBA, n_q), (0, 3, 2, 1)).reshape(n_seq, n_q, MOBA_W)


def _cumsum_rows(x):
    n = x.shape[0]
    row = lax.broadcasted_iota(jnp.int32, x.shape, 0)
    s = 1
    while s < n:
        x = x + jnp.where(row >= s, pltpu.roll(x, s, axis=0), 0.0)
        s *= 2
    return x


def _gla_chunk(q, k, v, la, st, cast):
    c = q.shape[0]
    sub = min(GLA_SUB, c)
    lane = lax.broadcasted_iota(jnp.int32, (1, GLA_KW), 1)
    head_masks = [(lane >= h * DK_GLA) & (lane < (h + 1) * DK_GLA) for h in range(H_GLA)]
    g = _cumsum_rows(la)
    krow = lax.broadcasted_iota(jnp.int32, (c, GLA_KW), 0)
    a_row = lax.broadcasted_iota(jnp.int32, (H_GLA * sub, c), 0)
    a_col = lax.broadcasted_iota(jnp.int32, (H_GLA * sub, c), 1)

    a_parts = []
    for i in range(c // sub):
        g_ref = jnp.zeros((1, GLA_KW), F32) if i == 0 else g[i * sub - 1:i * sub]
        qt = q[i * sub:(i + 1) * sub] * jnp.exp(g[i * sub:(i + 1) * sub] - g_ref)
        kt = jnp.where(krow < (i + 1) * sub, k * jnp.exp(g_ref - g), 0.0)
        q_stack = jnp.concatenate([jnp.where(hm, qt, 0.0) for hm in head_masks], axis=0)
        a = _dot_nt(cast(q_stack), cast(kt))
        a_parts.append(jnp.where(a_col <= i * sub + a_row % sub, a, 0.0))

    qe = q * jnp.exp(g)
    g_last = g[c - 1:c]
    k_dec = cast(k * jnp.exp(g_last - g))
    st_c = cast(st)
    outs = []
    st_new = st * jnp.exp(g_last)
    for h in range(H_GLA):
        a_h = jnp.concatenate([a[h * sub:(h + 1) * sub] for a in a_parts], axis=0)
        v_h = cast(v[:, h * DV_GLA:(h + 1) * DV_GLA])
        o_intra = _dot(cast(a_h), v_h)
        o_inter = _dot_nt(cast(jnp.where(head_masks[h], qe, 0.0)), st_c)
        outs.append(o_intra + o_inter)
        st_new = st_new + jnp.where(head_masks[h], _dot_tn(v_h, k_dec), 0.0)
    return outs, st_new


def _gla_kernel(q_ref, k_ref, v_ref, la_ref, r_ref, s0_ref, nw_ref, o_ref, sfin_ref, st_sc, *, chunk):
    t = pl.program_id(1)

    @pl.when(t == 0)
    def _():
        st_sc[...] = s0_ref[...]

    cast = (lambda a: a.astype(BF16)) if chunk >= 2 * SUBLANES else (lambda a: a)
    n_group = q_ref.shape[0]
    sts = [st_sc[g] for g in range(n_group)]
    for c in range(q_ref.shape[1] // chunk):
        rows = slice(c * chunk, (c + 1) * chunk)
        for g in range(n_group):
            q = q_ref[g, rows, :].astype(F32) * (DK_GLA ** -0.5)
            k = k_ref[g, rows, :].astype(F32)
            outs, sts[g] = _gla_chunk(q, k, v_ref[g, rows, :], la_ref[g, rows, :], sts[g], cast)
            for h, o in enumerate(outs):
                lanes = slice(h * DV_GLA, (h + 1) * DV_GLA)
                r = r_ref[g, rows, lanes]
                o = o * lax.rsqrt(jnp.mean(o * o, axis=-1, keepdims=True) + EPS)
                o_ref[g, rows, lanes] = (o * nw_ref[:, lanes] * (r * _sigmoid(r))).astype(o_ref.dtype)
    for g in range(n_group):
        st_sc[g] = sts[g]

    @pl.when(t == pl.num_programs(1) - 1)
    def _():
        for g in range(n_group):
            sfin_ref[g] = sts[g]


def _gla(qg, kg, vg, la, rg, state_t, nw, n_seq, seq_len, out_dtype):
    chunk = math.gcd(seq_len, GLA_CHUNK)
    tl = min(seq_len, SEQ_TILE)
    assert seq_len % tl == 0 and tl % chunk == 0
    nt = seq_len // tl
    n = n_seq * seq_len
    n_group = math.gcd(n_seq, max(1, GLA_STEP_CHUNKS * chunk // tl))
    seq3 = lambda a: a.reshape(n_seq, seq_len, a.shape[-1])
    row = lambda w: pl.BlockSpec((n_group, tl, w), lambda b, t: (b, t, 0))
    st_spec = pl.BlockSpec((n_group, DV_GLA, GLA_KW), lambda b, t: (b, 0, 0))
    o, st = pl.pallas_call(
        functools.partial(_gla_kernel, chunk=chunk),
        grid=(n_seq // n_group, nt),
        in_specs=[row(GLA_KW), row(GLA_KW), row(GLA_VW), row(GLA_KW), row(GLA_VW), st_spec,
                  _const_spec((1, GLA_VW))],
        out_specs=[row(GLA_VW), st_spec],
        out_shape=[jax.ShapeDtypeStruct((n_seq, seq_len, GLA_VW), out_dtype),
                   jax.ShapeDtypeStruct((n_seq, DV_GLA, GLA_KW), F32)],
        scratch_shapes=[pltpu.VMEM((n_group, DV_GLA, GLA_KW), F32)],
        compiler_params=_params(2),
        name="gla",
    )(seq3(qg), seq3(kg), seq3(vg), seq3(la), seq3(rg), state_t, nw)
    return o.reshape(n, GLA_VW), st


def _state_to_t(s):
    n = s.shape[0]
    return jnp.transpose(s, (0, 3, 1, 2)).reshape(n, DV_GLA, GLA_KW)


def _state_from_t(st):
    n = st.shape[0]
    return jnp.transpose(st.reshape(n, DV_GLA, H_GLA, DK_GLA), (0, 2, 3, 1))


def _cross_kernel(q_ref, mk_ref, mv_ref, o_ref):
    scale = HD_CROSS ** -0.5
    small = q_ref.shape[1] < 2 * SUBLANES
    cast = (lambda a: a) if small else (lambda a: a.astype(BF16))
    units = [(g, slice(h * HD_CROSS, (h + 1) * HD_CROSS)) for g in range(q_ref.shape[0]) for h in range(H_CROSS)]
    s = [_dot_nt(cast(q_ref[g, :, lanes]), cast(mk_ref[g, :, lanes])) * scale for g, lanes in units]
    p = [jnp.exp(si - jnp.max(si, axis=-1, keepdims=True)) for si in s]
    l = [jnp.sum(pi, axis=-1, keepdims=True) for pi in p]
    for (g, lanes), pi, li in zip(units, p, l):
        o_ref[g, :, lanes] = (_dot(cast(pi), cast(mv_ref[g, :, lanes])) / li).astype(o_ref.dtype)


def _cross(qc, mk, mv, n_seq, seq_len, n_mem, out_dtype):
    tl = min(seq_len, CROSS_STEP_ROWS)
    assert seq_len % tl == 0
    nt = seq_len // tl
    n_group = math.gcd(n_seq, min(CROSS_MAX_GROUP, max(1, CROSS_STEP_ROWS // tl)))
    row = pl.BlockSpec((n_group, tl, CROSS_W), lambda b, t: (b, t, 0))
    mem = pl.BlockSpec((n_group, n_mem, CROSS_W), lambda b, t: (b, 0, 0))
    o = pl.pallas_call(
        _cross_kernel,
        grid=(n_seq // n_group, nt),
        in_specs=[row, mem, mem],
        out_specs=row,
        out_shape=jax.ShapeDtypeStruct((n_seq, seq_len, CROSS_W), out_dtype),
        compiler_params=_params(2),
        name="cross",
    )(qc.reshape(n_seq, seq_len, CROSS_W), mk.reshape(n_seq, n_mem, CROSS_W), mv.reshape(n_seq, n_mem, CROSS_W))
    return o.reshape(n_seq * seq_len, CROSS_W)


def _merge_kernel(x_ref, om_ref, og_ref, oc_ref, nw_ref, wg_ref, bg_ref, wbm_ref, wbg_ref, wbc_ref,
                  wo_ref, h_ref):
    x = x_ref[...]
    d = x.shape[1]
    xb = _rms(x, nw_ref[...]).astype(BF16)
    merged = jnp.zeros(x.shape, F32)
    for i, (o_ref, wb_ref) in enumerate(((om_ref, wbm_ref), (og_ref, wbg_ref), (oc_ref, wbc_ref))):
        cols = slice(i * d, (i + 1) * d)
        gate = _sigmoid(_dot(xb, wg_ref[:, cols]) + bg_ref[:, cols])
        merged = merged + gate * _dot(o_ref[...].astype(BF16), wb_ref[...])
    h_ref[...] = x + _dot(merged.astype(BF16), wo_ref[...])


def _merge(x2d, om, og, oc, nw, wg, bg, wbm, wbg, wbc, wo):
    n, d = x2d.shape
    tm = min(ROW_TILE, n)
    row = lambda w: pl.BlockSpec((tm, w), lambda i: (i, 0))
    return pl.pallas_call(
        _merge_kernel,
        grid=(n // tm,),
        in_specs=[row(d), row(MOBA_W), row(GLA_VW), row(CROSS_W), _const_spec((1, d)),
                  _const_spec(wg.shape), _const_spec(bg.shape), _const_spec(wbm.shape),
                  _const_spec(wbg.shape), _const_spec(wbc.shape), _const_spec(wo.shape)],
        out_specs=row(d),
        out_shape=jax.ShapeDtypeStruct((n, d), F32),
        compiler_params=_params(1),
        name="merge",
    )(x2d, om, og, oc, nw, wg, bg, wbm, wbg, wbc, wo)


def _gelu_tanh(x):
    return 0.5 * x * (1.0 + jnp.tanh(0.7978845608028654 * (x + 0.044715 * (x * x * x))))


def _ffn_body(h_ref, nw_ref, wup_ref, wc_ref, bc_ref, wdn_ref, nf_ref, y_ref, prev_rows, store_u):
    h = h_ref[...]
    hb = _rms(h, nw_ref[...]).astype(BF16)
    d_ff = wdn_ref.shape[0]
    n_col_chunks = 2
    fc = d_ff // n_col_chunks
    assert fc % LANES == 0
    acc = jnp.zeros(h.shape, F32)
    for c in range(n_col_chunks):
        cols = slice(c * fc, (c + 1) * fc)
        u = _dot(hb, wup_ref[:, cols])
        gate = _dot(hb, wup_ref[:, d_ff + c * fc:d_ff + (c + 1) * fc])
        u1, u2 = prev_rows(cols, u)
        store_u(cols, u)
        conv = bc_ref[:, cols] + wc_ref[0:1, cols] * u2 + wc_ref[1:2, cols] * u1 + wc_ref[2:3, cols] * u
        act = (_gelu_tanh(conv) * gate).astype(BF16)
        acc = acc + _dot(act, wdn_ref[cols, :])
    y_ref[...] = _rms(h + acc, nf_ref[...])


def _ffn_long_kernel(h_ref, nw_ref, wup_ref, wc_ref, bc_ref, wdn_ref, nf_ref, y_ref, cs_ref, carry_sc):
    tm = h_ref.shape[0]

    @pl.when(pl.program_id(1) == 0)
    def _():
        carry_sc[...] = jnp.zeros(carry_sc.shape, F32)

    row = lax.broadcasted_iota(jnp.int32, (tm, 1), 0)

    def prev_rows(cols, u):
        last = carry_sc[SUBLANES - 1:SUBLANES, cols]
        last2 = carry_sc[SUBLANES - 2:SUBLANES - 1, cols]
        u1 = jnp.where(row == 0, last, pltpu.roll(u, 1, axis=0))
        u2 = jnp.where(row == 0, last2, jnp.where(row == 1, last, pltpu.roll(u, 2, axis=0)))
        return u1, u2

    def store_u(cols, u):
        carry_sc[:, cols] = u[tm - SUBLANES:tm]
        cs_ref[0, :, cols] = u[tm - (CONV_W - 1):tm]

    _ffn_body(h_ref, nw_ref, wup_ref, wc_ref, bc_ref, wdn_ref, nf_ref, y_ref, prev_rows, store_u)


def _ffn_short_kernel(h_ref, p1_ref, p2_ref, nw_ref, wup_ref, wc_ref, bc_ref, wdn_ref, nf_ref,
                      y_ref, u_ref, *, seq_len):
    tm = h_ref.shape[0]
    pos = lax.broadcasted_iota(jnp.int32, (tm, 1), 0) % seq_len

    def prev_rows(cols, u):
        u1 = jnp.where(pos >= 1, pltpu.roll(u, 1, axis=0), p1_ref[:, cols])
        u2 = jnp.where(pos >= 2, pltpu.roll(u, 2, axis=0), p2_ref[:, cols])
        return u1, u2

    def store_u(cols, u):
        u_ref[:, cols] = u

    _ffn_body(h_ref, nw_ref, wup_ref, wc_ref, bc_ref, wdn_ref, nf_ref, y_ref, prev_rows, store_u)


def _ffn_weights_specs(d, wup, wc, bc, wdn):
    return [_const_spec((1, d)), _const_spec(wup.shape), _const_spec(wc.shape), _const_spec(bc.shape),
            _const_spec(wdn.shape), _const_spec((1, d))]


def _ffn_long(h2d, n_seq, seq_len, nw, wup, wc, bc, wdn, nf):
    n, d = h2d.shape
    d_ff = wdn.shape[0]
    tm = min(ROW_TILE, n)
    assert seq_len % tm == 0
    nt = seq_len // tm
    row = pl.BlockSpec((tm, d), lambda b, t: (b * nt + t, 0))
    return pl.pallas_call(
        _ffn_long_kernel,
        grid=(n_seq, nt),
        in_specs=[row] + _ffn_weights_specs(d, wup, wc, bc, wdn),
        out_specs=[row, pl.BlockSpec((1, CONV_W - 1, d_ff), lambda b, t: (b, 0, 0))],
        out_shape=[jax.ShapeDtypeStruct((n, d), F32),
                   jax.ShapeDtypeStruct((n_seq, CONV_W - 1, d_ff), F32)],
        scratch_shapes=[pltpu.VMEM((SUBLANES, d_ff), F32)],
        compiler_params=_params(2),
        name="ffn_long",
    )(h2d, nw, wup, wc, bc, wdn, nf)


def _ffn_short(h2d, conv_prev, n_seq, seq_len, nw, wup, wc, bc, wdn, nf):
    n, d = h2d.shape
    d_ff = wdn.shape[0]
    tm = min(ROW_TILE, n)
    assert tm % seq_len == 0 and n % tm == 0 and seq_len >= CONV_W - 1
    zeros = jnp.zeros((n_seq, seq_len, d_ff), F32)
    p1 = zeros.at[:, 0].set(conv_prev[:, 1]).reshape(n, d_ff)
    p2 = zeros.at[:, 0].set(conv_prev[:, 0]).at[:, 1].set(conv_prev[:, 1]).reshape(n, d_ff)
    row = lambda w: pl.BlockSpec((tm, w), lambda i: (i, 0))
    y, u = pl.pallas_call(
        functools.partial(_ffn_short_kernel, seq_len=seq_len),
        grid=(n // tm,),
        in_specs=[row(d), row(d_ff), row(d_ff)] + _ffn_weights_specs(d, wup, wc, bc, wdn),
        out_specs=[row(d), row(d_ff)],
        out_shape=[jax.ShapeDtypeStruct((n, d), F32), jax.ShapeDtypeStruct((n, d_ff), F32)],
        compiler_params=_params(1),
        name="ffn_short",
    )(h2d, p1, p2, nw, wup, wc, bc, wdn, nf)
    return y, u.reshape(n_seq, seq_len, d_ff)[:, seq_len - (CONV_W - 1):]


def _prep_weights(norm_mix, w_in, w_gla_a2, b_gla_a, norm_gla, norm_mem, w_mem_kv, w_br_moba, w_br_gla,
                  w_br_cross, w_gate, b_gate, w_out, norm_ffn, w_up, w_conv, b_conv, w_down, norm_final):
    d = w_in.shape[0]
    o_a = C_QC
    w_in_p = jnp.concatenate(
        [w_in[:, :o_a], w_in[:, o_a + GLA_RANK:], w_in[:, o_a:o_a + GLA_RANK],
         jnp.zeros((d, A_PAD - GLA_RANK), w_in.dtype)], axis=1).astype(BF16)
    wa2_p = jnp.concatenate([w_gla_a2, jnp.zeros((A_PAD - GLA_RANK, GLA_KW), w_gla_a2.dtype)],
                            axis=0).astype(BF16)
    r2 = lambda a: a.reshape(1, -1)
    return dict(
        norm_mix=r2(norm_mix), w_in=w_in_p, wa2=wa2_p, ba=r2(b_gla_a), norm_gla=r2(norm_gla),
        norm_mem=r2(norm_mem), w_mem_kv=w_mem_kv.astype(BF16), wbm=w_br_moba.astype(BF16),
        wbg=w_br_gla.astype(BF16), wbc=w_br_cross.astype(BF16), wg=w_gate.astype(BF16), bg=r2(b_gate),
        wo=w_out.astype(BF16), norm_ffn=r2(norm_ffn), wup=w_up.astype(BF16), wc=w_conv, bc=r2(b_conv),
        wdn=w_down.astype(BF16), norm_final=r2(norm_final))


def kernel(x_prompt, x_sample, cache_moba_k, cache_moba_v, state_gla, state_conv, cache_mem_k, cache_mem_v, page_table, mem_prompt, norm_mix, w_in, w_gla_a2, b_gla_a, norm_gla, norm_mem, w_mem_kv, w_br_moba, w_br_gla, w_br_cross, w_gate, b_gate, w_out, norm_ffn, w_up, w_conv, b_conv, w_down, norm_final):
    depth = w_in.shape[0]
    assert depth == 1
    bp, sp, d = x_prompt.shape
    bs, ss, _ = x_sample.shape
    n_mem = mem_prompt.shape[1]
    w = _prep_weights(norm_mix[0], w_in[0], w_gla_a2[0], b_gla_a[0], norm_gla[0], norm_mem[0], w_mem_kv[0],
                      w_br_moba[0], w_br_gla[0], w_br_cross[0], w_gate[0], b_gate[0], w_out[0], norm_ffn[0],
                      w_up[0], w_conv[0], b_conv[0], w_down[0], norm_final)

    def mix(x2d, o_m, o_g, o_c):
        return _merge(x2d, o_m, o_g, o_c, w["norm_mix"], w["wg"], w["bg"], w["wbm"], w["wbg"], w["wbc"], w["wo"])

    ffn_w = (w["norm_ffn"], w["wup"], w["wc"], w["bc"], w["wdn"], w["norm_final"])

    xp = x_prompt.reshape(bp * sp, d)
    qm, kt_p, vt_p, qg, kg, vg, rg, qc, la, ksum, kb, vb = _project(
        xp, w["norm_mix"], w["w_in"], w["wa2"], w["ba"], BF16, MOBA_Q_SCALE, kv_seq_len=sp)
    o_m = _moba_prompt(qm, kb, vb, ksum.reshape(-1, MOBA_W), bp, sp)
    mk_p, mv_p = _memory_kv(mem_prompt.reshape(bp * n_mem, d), w["norm_mem"], w["w_mem_kv"])
    o_g, gla_p = _gla(qg, kg, vg, la, rg, jnp.zeros((bp, DV_GLA, GLA_KW), F32), w["norm_gla"], bp, sp, BF16)
    o_c = _cross(qc, mk_p, mv_p, bp, sp, n_mem, BF16)
    h_p = mix(xp, o_m, o_g, o_c)
    y_p, conv_p = _ffn_long(h_p, bp, sp, *ffn_w)

    xs = x_sample.reshape(bs * ss, d)
    qm, k_s, v_s, qg, kg, vg, rg, qc, la, _, _, _ = _project(
        xs, w["norm_mix"], w["w_in"], w["wa2"], w["ba"], F32, 1.0)
    r3 = lambda a: a.reshape(bs, ss, MOBA_W)
    o_m = _moba_sample(r3(qm), r3(k_s), r3(v_s), cache_moba_k[0], cache_moba_v[0],
                       page_table).reshape(bs * ss, MOBA_W)
    o_g, gla_s = _gla(qg, kg, vg, la, rg, _state_to_t(state_gla[0]), w["norm_gla"], bs, ss, F32)
    o_c = _cross(qc, cache_mem_k[0].reshape(bs * n_mem, CROSS_W), cache_mem_v[0].reshape(bs * n_mem, CROSS_W),
                 bs, ss, n_mem, F32)
    h_s = mix(xs, o_m, o_g, o_c)
    y_s, conv_s = _ffn_short(h_s, state_conv[0], bs, ss, *ffn_w)

    kv5 = lambda a, b, s: a.reshape(1, b, s, H_MOBA, HD_MOBA)
    kv5_t = lambda a: jnp.transpose(a.reshape(bp, H_MOBA, HD_MOBA, sp), (0, 3, 1, 2))[None]
    return (y_p.reshape(bp, sp, d), y_s.reshape(bs, ss, d),
            kv5_t(kt_p), kv5_t(vt_p), kv5(k_s, bs, ss), kv5(v_s, bs, ss),
            _state_from_t(gla_p)[None], _state_from_t(gla_s)[None],
            conv_p[None], conv_s[None],
            mk_p.reshape(1, bp, n_mem, H_CROSS, HD_CROSS), mv_p.reshape(1, bp, n_mem, H_CROSS, HD_CROSS))
```

```python
import functools
import math

import jax
import jax.numpy as jnp
from jax import lax
from jax.experimental import pallas as pl
from jax.experimental.pallas import tpu as pltpu

F32 = jnp.float32
BF16 = jnp.bfloat16
NEG_INF = float("-inf")

EPS = 1e-6
H_MOBA, HD_MOBA = 8, 64
MOBA_W = H_MOBA * HD_MOBA
MOBA_BLOCK = 256
MOBA_TOPK = 3
PAGE_SIZE = 128
H_GLA, DK_GLA, DV_GLA = 4, 64, 128
GLA_KW, GLA_VW = H_GLA * DK_GLA, H_GLA * DV_GLA
GLA_RANK = 16
GLA_TAU = 16.0
GLA_CHUNK = 64
GLA_SUB = 16
H_CROSS, HD_CROSS = 4, 128
CROSS_W = H_CROSS * HD_CROSS
N_BRANCH = 3
CONV_W = 3

LANES = 128
SUBLANES = 8
VMEM_LIMIT_BYTES = 56 * 1024 * 1024

MASK_BIAS = -1e30
MOBA_GROUP = 8
MOBA_TQ = 2 * MOBA_BLOCK
MOBA_Q_SCALE = HD_MOBA ** -0.5 * math.log2(math.e)
ROW_TILE = 512
SEQ_TILE = 256
GLA_STEP_CHUNKS = 8
CROSS_STEP_ROWS = 512
CROSS_MAX_GROUP = 8
PAGES_PER_STEP = 16
A_PAD = LANES

C_QM, C_K, C_V = 0, MOBA_W, 2 * MOBA_W
C_QG = 3 * MOBA_W
C_KG = C_QG + GLA_KW
C_VG = C_KG + GLA_KW
C_RG = C_VG + GLA_VW
C_QC = C_RG + GLA_VW
C_A = C_QC + CROSS_W
IN_COLS_PAD = C_A + A_PAD


def _params(n_axes):
    return pltpu.CompilerParams(
        dimension_semantics=("arbitrary",) * n_axes,
        vmem_limit_bytes=VMEM_LIMIT_BYTES,
    )


def _const_spec(shape):
    nd = len(shape)
    return pl.BlockSpec(shape, lambda *_: (0,) * nd, pipeline_mode=pl.Buffered(1))


def _rms(x, w):
    return x * lax.rsqrt(jnp.mean(x * x, axis=-1, keepdims=True) + EPS) * w


def _sigmoid(x):
    return 1.0 / (1.0 + jnp.exp(-x))


def _dot(a, b):
    return jnp.dot(a, b, preferred_element_type=F32)


def _dot_nt(a, b):
    return lax.dot_general(a, b, (((1,), (1,)), ((), ())), preferred_element_type=F32)


def _dot_tn(a, b):
    return lax.dot_general(a, b, (((0,), (0,)), ((), ())), preferred_element_type=F32)


def _proj_kernel(x_ref, nw_ref, w_ref, wa2_ref, ba_ref,
                 qm_ref, k_ref, v_ref, qg_ref, kg_ref, vg_ref, rg_ref, qc_ref, la_ref, ksum_ref,
                 kb_ref, vb_ref, *, qm_scale, kv_transposed):
    xb = _rms(x_ref[...], nw_ref[...]).astype(BF16)

    def mm(lo, hi):
        return _dot(xb, w_ref[:, lo:hi])

    def store_kv(ref, val):
        if kv_transposed:
            ref[0] = val.T
        else:
            ref[...] = val

    qm_ref[...] = (mm(C_QM, C_K) * qm_scale).astype(qm_ref.dtype)
    k = mm(C_K, C_V)
    store_kv(k_ref, k)
    kb_ref[...] = k.astype(BF16)
    for g in range(k.shape[0] // MOBA_BLOCK):
        ksum_ref[g] = jnp.sum(k[g * MOBA_BLOCK:(g + 1) * MOBA_BLOCK], axis=0, keepdims=True)
    v = mm(C_V, C_QG)
    store_kv(v_ref, v)
    vb_ref[...] = v.astype(BF16)
    qg_ref[...] = mm(C_QG, C_KG).astype(qg_ref.dtype)
    kg_ref[...] = mm(C_KG, C_VG).astype(kg_ref.dtype)
    vg_ref[...] = mm(C_VG, C_RG).astype(vg_ref.dtype)
    rg_ref[...] = mm(C_RG, C_QC)
    qc_ref[...] = mm(C_QC, C_A).astype(qc_ref.dtype)
    z = _dot(mm(C_A, IN_COLS_PAD).astype(BF16), wa2_ref[...]) + ba_ref[...]
    la_ref[...] = (jnp.minimum(z, 0.0) - jnp.log1p(jnp.exp(-jnp.abs(z)))) * (1.0 / GLA_TAU)


def _project(x2d, nw, w_in_p, wa2_p, ba, act_dtype, qm_scale, kv_seq_len=None):
    n, d = x2d.shape
    tm = min(ROW_TILE, n)
    assert n % tm == 0 and tm % MOBA_BLOCK == 0
    row = lambda w: pl.BlockSpec((tm, w), lambda i: (i, 0))
    widths = (MOBA_W, MOBA_W, MOBA_W, GLA_KW, GLA_KW, GLA_VW, GLA_VW, CROSS_W, GLA_KW)
    dtypes = (act_dtype, F32, F32, act_dtype, act_dtype, act_dtype, F32, act_dtype, F32)
    out_shape = [jax.ShapeDtypeStruct((n, w), dt) for w, dt in zip(widths, dtypes)]
    out_shape.append(jax.ShapeDtypeStruct((n // MOBA_BLOCK, 1, MOBA_W), F32))
    out_shape += [jax.ShapeDtypeStruct((n, MOBA_W), BF16)] * 2
    out_specs = [row(w) for w in widths]
    out_specs.append(pl.BlockSpec((tm // MOBA_BLOCK, 1, MOBA_W), lambda i: (i, 0, 0)))
    out_specs += [row(MOBA_W)] * 2
    if kv_seq_len is not None:
        assert kv_seq_len % tm == 0
        nt = kv_seq_len // tm
        for i in (1, 2):
            out_shape[i] = jax.ShapeDtypeStruct((n // kv_seq_len, MOBA_W, kv_seq_len), F32)
            out_specs[i] = pl.BlockSpec((1, MOBA_W, tm), lambda i: (i // nt, 0, i % nt))
    return pl.pallas_call(
        functools.partial(_proj_kernel, qm_scale=qm_scale, kv_transposed=kv_seq_len is not None),
        grid=(n // tm,),
        in_specs=[row(d), _const_spec((1, d)), _const_spec(w_in_p.shape),
                  _const_spec(wa2_p.shape), _const_spec((1, GLA_KW))],
        out_specs=out_specs,
        out_shape=out_shape,
        compiler_params=_params(1),
        name="proj",
    )(x2d, nw, w_in_p, wa2_p, ba)


def _memkv_kernel(m_ref, nw_ref, w_ref, mk_ref, mv_ref):
    mb = _rms(m_ref[...], nw_ref[...]).astype(BF16)
    mk_ref[...] = _dot(mb, w_ref[:, :CROSS_W])
    mv_ref[...] = _dot(mb, w_ref[:, CROSS_W:])


def _memory_kv(mem2d, nw, w_kv):
    n, d = mem2d.shape
    tm = min(ROW_TILE, n)
    assert n % tm == 0
    return pl.pallas_call(
        _memkv_kernel,
        grid=(n // tm,),
        in_specs=[pl.BlockSpec((tm, d), lambda i: (i, 0)), _const_spec((1, d)), _const_spec(w_kv.shape)],
        out_specs=[pl.BlockSpec((tm, CROSS_W), lambda i: (i, 0))] * 2,
        out_shape=[jax.ShapeDtypeStruct((n, CROSS_W), F32)] * 2,
        compiler_params=_params(1),
        name="memkv",
    )(mem2d, nw, w_kv)


def _select_topk(gate, idx, valid, axis):
    return _select_topk_each([gate], idx, valid, axis)[0]


def _select_topk_each(gates, idx, valid, axis):
    gs = [jnp.where(valid, gate, NEG_INF) for gate in gates]
    picked = [jnp.zeros(gate.shape, jnp.bool_) for gate in gates]
    for _ in range(MOBA_TOPK):
        ms = [jnp.max(g, axis=axis, keepdims=True) for g in gs]
        firsts = [jnp.min(jnp.where(g == m, idx, float(2 ** 24)), axis=axis, keepdims=True) for g, m in zip(gs, ms)]
        picks = [(idx == first) & (m > NEG_INF) & valid for first, m in zip(firsts, ms)]
        picked = [a | b for a, b in zip(picked, picks)]
        gs = [jnp.where(pick, NEG_INF, g) for pick, g in zip(picks, gs)]
    return picked


def _moba_prompt_kernel(q_ref, k_ref, v_ref, ksum_ref, o_ref, ka_sc, va_sc, s_sc):
    t = pl.program_id(2)
    tq = q_ref.shape[0]
    blk = MOBA_BLOCK
    nblk = ksum_ref.shape[0]
    seq_len = k_ref.shape[0]
    group = s_sc.shape[3] // blk
    n_heads = LANES // HD_MOBA
    spare_off = [((half + 1) % n_heads) * HD_MOBA for half in range(n_heads)]

    @pl.when(t == 0)
    def _():
        key_blk = lax.broadcasted_iota(jnp.int32, (seq_len, LANES), 0) // blk
        key_lane = lax.broadcasted_iota(jnp.int32, (seq_len, LANES), 1)
        k = k_ref[...]
        v = v_ref[...]
        for half in range(n_heads):
            in_head = (key_lane >= half * HD_MOBA) & (key_lane < (half + 1) * HD_MOBA)
            onehot = jnp.where(key_lane - spare_off[half] == key_blk, 1.0, 0.0).astype(BF16)
            ka_sc[half] = jnp.where(in_head, k, onehot)
            va_sc[half] = jnp.where(in_head, v, jnp.ones_like(v))

    q = q_ref[...]
    kmean = ksum_ref[...] * (1.0 / blk)
    km_hi = kmean.astype(BF16)
    km_lo = (kmean - km_hi.astype(F32)).astype(BF16)
    lane = lax.broadcasted_iota(jnp.int32, (1, LANES), 1)
    row = lax.broadcasted_iota(jnp.int32, (tq, tq), 0)
    col = lax.broadcasted_iota(jnp.int32, (tq, tq), 1)
    blk_id = lax.broadcasted_iota(jnp.int32, (nblk, tq), 0)
    place_row = lax.broadcasted_iota(jnp.int32, (nblk, LANES), 0)
    place_lane = lax.broadcasted_iota(jnp.int32, (nblk, LANES), 1)
    first_blk = t * (tq // blk)
    own_blk = first_blk + lax.broadcasted_iota(jnp.int32, (nblk, tq), 1) // blk
    own = pl.ds(pl.multiple_of(t * tq, tq), tq)
    n_groups = (first_blk + tq // blk - 1 + group - 1) // group
    heads = range(n_heads)

    in_head = [(lane >= half * HD_MOBA) & (lane < (half + 1) * HD_MOBA) for half in heads]
    qh = [jnp.where(in_head[half], q, jnp.zeros_like(q)) for half in heads]
    gate_t = [_dot_nt(km_hi, qh[half]) + _dot_nt(km_lo, qh[half]) for half in heads]
    own_mask = (col <= row) & (col // blk == row // blk)
    s_own = [jnp.where(own_mask, _dot_nt(qh[half], ka_sc[half, own, :]), MASK_BIAS) for half in heads]
    picked_t = _select_topk_each(gate_t, blk_id.astype(F32), blk_id < own_blk, axis=0)
    place = [jnp.where(place_lane - spare_off[half] == place_row, 1.0, 0.0).astype(BF16) for half in heads]
    picked = [_dot_tn(jnp.where(picked_t[half], 1.0, 0.0).astype(BF16), place[half]) for half in heads]
    q_aug = []
    for half in heads:
        in_range = (lane >= spare_off[half]) & (lane < spare_off[half] + nblk)
        bias = jnp.where(in_range & (picked[half] < 0.5), MASK_BIAS, 0.0)
        q_aug.append(qh[half] + bias.astype(BF16))

    def group_rows(g):
        return pl.ds(pl.multiple_of(g * (group * blk), group * blk), group * blk)

    def score_pass(g, mxs):
        out = []
        for half in heads:
            s = _dot_nt(q_aug[half], ka_sc[half, group_rows(g), :])
            s_sc[half, g] = s
            mx = mxs[half]
            for i in range(group):
                mx = jnp.maximum(mx, s[:, i * blk:(i + 1) * blk])
            out.append(mx)
        return tuple(out)

    mx0 = []
    for half in heads:
        mx = s_own[half][:, 0:blk]
        for i in range(1, tq // blk):
            mx = jnp.maximum(mx, s_own[half][:, i * blk:(i + 1) * blk])
        mx0.append(mx)
    mxs = lax.fori_loop(0, n_groups, score_pass, tuple(mx0))
    m = [jnp.max(mx, axis=-1, keepdims=True) for mx in mxs]

    def value_pass(g, accs):
        out = []
        for half in heads:
            p = jnp.exp2(s_sc[half, g] - m[half])
            out.append(accs[half] + _dot(p.astype(BF16), va_sc[half, group_rows(g), :]))
        return tuple(out)

    acc0 = tuple(_dot(jnp.exp2(s_own[half] - m[half]).astype(BF16), va_sc[half, own, :]) for half in heads)
    accs = lax.fori_loop(0, n_groups, value_pass, acc0)

    o = jnp.zeros((tq, LANES), F32)
    for half in heads:
        row_sum = accs[half][:, spare_off[half]:spare_off[half] + 1]
        o = jnp.where(in_head[half], accs[half] / row_sum, o)
    o_ref[...] = o.astype(o_ref.dtype)


def _moba_prompt(qm, kb, vb, ksum, n_seq, seq_len):
    assert seq_len % MOBA_BLOCK == 0
    nblk = seq_len // MOBA_BLOCK
    assert nblk <= HD_MOBA
    group = MOBA_GROUP if nblk % MOBA_GROUP == 0 else 1
    tq = MOBA_TQ if seq_len % MOBA_TQ == 0 else MOBA_BLOCK
    nt = seq_len // tq
    n = n_seq * seq_len
    n_pair = MOBA_W // LANES
    n_heads = LANES // HD_MOBA
    seq_block = pl.BlockSpec((seq_len, LANES), lambda b, hp, t: (b, hp), pipeline_mode=pl.Buffered(1))
    return pl.pallas_call(
        _moba_prompt_kernel,
        grid=(n_seq, n_pair, nt),
        in_specs=[
            pl.BlockSpec((tq, LANES), lambda b, hp, t: (b * nt + t, hp)),
            seq_block, seq_block,
            pl.BlockSpec((nblk, LANES), lambda b, hp, t: (b, hp)),
        ],
        out_specs=pl.BlockSpec((tq, LANES), lambda b, hp, t: (b * nt + t, hp)),
        out_shape=jax.ShapeDtypeStruct((n, MOBA_W), BF16),
        scratch_shapes=[pltpu.VMEM((n_heads, seq_len, LANES), BF16),
                        pltpu.VMEM((n_heads, seq_len, LANES), BF16),
                        pltpu.VMEM((n_heads, nblk // group, tq, group * MOBA_BLOCK), F32)],
        compiler_params=_params(3),
        name="moba_prompt",
    )(qm, kb, vb, ksum)


def _moba_sample_kernel(pt_ref, qbd_ref, kn_ref, vnt_ref, *rest, n_q):
    del pt_ref
    pps = PAGES_PER_STEP
    k_pages, v_pages = rest[:pps], rest[pps:2 * pps]
    o_ref, m_sc, l_sc, gate_sc, acc_sc = rest[2 * pps:]
    s_id = pl.program_id(1)
    n_step = pl.num_programs(1)
    nblk = m_sc.shape[1]
    n_hq = qbd_ref.shape[1]
    qbd = qbd_ref[0]
    col_head = lax.broadcasted_iota(jnp.int32, (HD_MOBA, n_hq), 1) // n_q
    blk_col = lax.broadcasted_iota(jnp.int32, (n_hq, nblk), 1)
    pages_per_blk = MOBA_BLOCK // PAGE_SIZE

    def own_head(pvt):
        out = pvt[0:HD_MOBA]
        for h in range(1, H_MOBA):
            out = jnp.where(col_head == h, pvt[h * HD_MOBA:(h + 1) * HD_MOBA], out)
        return out

    def block_t(page_refs, i):
        pages = range(i * pages_per_blk, (i + 1) * pages_per_blk)
        return jnp.concatenate([page_refs[j][0] for j in pages], axis=1).astype(BF16)

    blocks = range(pps // pages_per_blk)
    b_idx = [s_id * (pps // pages_per_blk) + i for i in blocks]
    s = [_dot(qbd, block_t(k_pages, i)) for i in blocks]
    gate = [jnp.sum(si, axis=-1, keepdims=True) for si in s]
    m = [jnp.max(si, axis=-1, keepdims=True) for si in s]
    p = [jnp.exp(si - mi) for si, mi in zip(s, m)]
    l = [jnp.sum(pi, axis=-1, keepdims=True) for pi in p]
    pv = [own_head(_dot_nt(block_t(v_pages, i), p[i].astype(BF16))) for i in blocks]
    for i in blocks:
        acc_sc[b_idx[i]] = pv[i]
    for ref, vals in ((m_sc, m), (l_sc, l), (gate_sc, gate)):
        cur = ref[...]
        for i in blocks:
            cur = jnp.where(blk_col == b_idx[i], vals[i], cur)
        ref[...] = cur

    @pl.when(s_id == n_step - 1)
    def _():
        picked = _select_topk(gate_sc[...], blk_col.astype(F32), blk_col >= 0, axis=1)
        m_all = jnp.where(picked, m_sc[...], NEG_INF)
        kn = kn_ref[0].astype(BF16)
        vnt = vnt_ref[0].astype(BF16)
        n_new = kn.shape[0]
        key_i = lax.broadcasted_iota(jnp.int32, (n_hq, n_new), 1)
        q_i = lax.broadcasted_iota(jnp.int32, (n_hq, n_new), 0) % n_q
        s_own = jnp.where(key_i <= q_i, _dot_nt(qbd, kn), NEG_INF)
        m_tot = jnp.maximum(jnp.max(s_own, axis=-1, keepdims=True),
                            jnp.max(m_all, axis=-1, keepdims=True))
        p_own = jnp.exp(s_own - m_tot)
        c = jnp.exp(m_all - m_tot)
        l_tot = jnp.sum(p_own, axis=-1, keepdims=True) + jnp.sum(c * l_sc[...], axis=-1, keepdims=True)
        inv_l = 1.0 / l_tot
        c_t = jnp.transpose(c * inv_l)
        out = own_head(_dot_nt(vnt, (p_own * inv_l).astype(BF16)))
        for b in range(nblk):
            out = out + acc_sc[b] * c_t[b:b + 1, :]
        o_ref[0] = out


def _moba_sample(q, k_new, v_new, cache_k, cache_v, page_table):
    n_seq, n_q, _ = q.shape
    n_phys = cache_k.shape[0]
    n_pages = page_table.shape[1]
    past = n_pages * PAGE_SIZE
    assert past % MOBA_BLOCK == 0 and n_pages % PAGES_PER_STEP == 0
    assert n_q <= MOBA_BLOCK and n_q % SUBLANES == 0
    nblk = past // MOBA_BLOCK
    n_hq = H_MOBA * n_q
    lane_head = jnp.arange(MOBA_W) // HD_MOBA
    qbd = jnp.where(lane_head[None, None, None, :] == jnp.arange(H_MOBA)[None, :, None, None],
                    q[:, None, :, :] * (HD_MOBA ** -0.5), 0.0).reshape(n_seq, n_hq, MOBA_W).astype(BF16)
    n_new = max(2 * SUBLANES, n_q)
    pad = ((0, 0), (0, n_new - n_q), (0, 0))
    kn = jnp.pad(k_new, pad)
    vnt = jnp.transpose(jnp.pad(v_new, pad), (0, 2, 1))
    to_t = lambda c: jnp.transpose(c, (0, 2, 3, 1)).reshape(n_phys, MOBA_W, PAGE_SIZE)

    def page_spec(i):
        return pl.BlockSpec((1, MOBA_W, PAGE_SIZE), lambda b, s, pt: (pt[b, s * PAGES_PER_STEP + i], 0, 0))

    seq3 = lambda r, c: pl.BlockSpec((1, r, c), lambda b, s, pt: (b, 0, 0))
    grid_spec = pltpu.PrefetchScalarGridSpec(
        num_scalar_prefetch=1,
        grid=(n_seq, n_pages // PAGES_PER_STEP),
        in_specs=[seq3(n_hq, MOBA_W), seq3(n_new, MOBA_W), seq3(MOBA_W, n_new)]
        + [page_spec(i) for i in range(PAGES_PER_STEP)] * 2,
        out_specs=seq3(HD_MOBA, n_hq),
        scratch_shapes=[
            pltpu.VMEM((n_hq, nblk), F32),
            pltpu.VMEM((n_hq, nblk), F32),
            pltpu.VMEM((n_hq, nblk), F32),
            pltpu.VMEM((nblk, HD_MOBA, n_hq), F32),
        ],
    )
    out = pl.pallas_call(
        functools.partial(_moba_sample_kernel, n_q=n_q),
        grid_spec=grid_spec,
        out_shape=jax.ShapeDtypeStruct((n_seq, HD_MOBA, n_hq), F32),
        compiler_params=_params(2),
        name="moba_sample",
    )(page_table, qbd, kn, vnt, *([to_t(cache_k)] * PAGES_PER_STEP), *([to_t(cache_v)] * PAGES_PER_STEP))
    return jnp.transpose(out.reshape(n_seq, HD_MOBA, H_MOBA, n_q), (0, 3, 2, 1)).reshape(n_seq, n_q, MOBA_W)


def _cumsum_rows(x):
    n = x.shape[0]
    row = lax.broadcasted_iota(jnp.int32, x.shape, 0)
    s = 1
    while s < n:
        x = x + jnp.where(row >= s, pltpu.roll(x, s, axis=0), 0.0)
        s *= 2
    return x


def _gla_chunk(q, k, v, la, st, cast):
    c = q.shape[0]
    sub = min(GLA_SUB, c)
    lane = lax.broadcasted_iota(jnp.int32, (1, GLA_KW), 1)
    head_masks = [(lane >= h * DK_GLA) & (lane < (h + 1) * DK_GLA) for h in range(H_GLA)]
    g = _cumsum_rows(la)
    krow = lax.broadcasted_iota(jnp.int32, (c, GLA_KW), 0)
    a_row = lax.broadcasted_iota(jnp.int32, (H_GLA * sub, c), 0)
    a_col = lax.broadcasted_iota(jnp.int32, (H_GLA * sub, c), 1)

    a_parts = []
    for i in range(c // sub):
        g_ref = jnp.zeros((1, GLA_KW), F32) if i == 0 else g[i * sub - 1:i * sub]
        qt = q[i * sub:(i + 1) * sub] * jnp.exp(g[i * sub:(i + 1) * sub] - g_ref)
        kt = jnp.where(krow < (i + 1) * sub, k * jnp.exp(g_ref - g), 0.0)
        q_stack = jnp.concatenate([jnp.where(hm, qt, 0.0) for hm in head_masks], axis=0)
        a = _dot_nt(cast(q_stack), cast(kt))
        a_parts.append(jnp.where(a_col <= i * sub + a_row % sub, a, 0.0))

    qe = q * jnp.exp(g)
    g_last = g[c - 1:c]
    k_dec = cast(k * jnp.exp(g_last - g))
    st_c = cast(st)
    outs = []
    st_new = st * jnp.exp(g_last)
    for h in range(H_GLA):
        a_h = jnp.concatenate([a[h * sub:(h + 1) * sub] for a in a_parts], axis=0)
        v_h = cast(v[:, h * DV_GLA:(h + 1) * DV_GLA])
        o_intra = _dot(cast(a_h), v_h)
        o_inter = _dot_nt(cast(jnp.where(head_masks[h], qe, 0.0)), st_c)
        outs.append(o_intra + o_inter)
        st_new = st_new + jnp.where(head_masks[h], _dot_tn(v_h, k_dec), 0.0)
    return outs, st_new


def _gla_kernel(q_ref, k_ref, v_ref, la_ref, r_ref, s0_ref, nw_ref, o_ref, sfin_ref, st_sc, *, chunk):
    t = pl.program_id(1)

    @pl.when(t == 0)
    def _():
        st_sc[...] = s0_ref[...]

    cast = (lambda a: a.astype(BF16)) if chunk >= 2 * SUBLANES else (lambda a: a)
    n_group = q_ref.shape[0]
    sts = [st_sc[g] for g in range(n_group)]
    for c in range(q_ref.shape[1] // chunk):
        rows = slice(c * chunk, (c + 1) * chunk)
        for g in range(n_group):
            q = q_ref[g, rows, :].astype(F32) * (DK_GLA ** -0.5)
            k = k_ref[g, rows, :].astype(F32)
            outs, sts[g] = _gla_chunk(q, k, v_ref[g, rows, :], la_ref[g, rows, :], sts[g], cast)
            for h, o in enumerate(outs):
                lanes = slice(h * DV_GLA, (h + 1) * DV_GLA)
                r = r_ref[g, rows, lanes]
                o = o * lax.rsqrt(jnp.mean(o * o, axis=-1, keepdims=True) + EPS)
                o_ref[g, rows, lanes] = (o * nw_ref[:, lanes] * (r * _sigmoid(r))).astype(o_ref.dtype)
    for g in range(n_group):
        st_sc[g] = sts[g]

    @pl.when(t == pl.num_programs(1) - 1)
    def _():
        for g in range(n_group):
            sfin_ref[g] = sts[g]


def _gla(qg, kg, vg, la, rg, state_t, nw, n_seq, seq_len, out_dtype):
    chunk = math.gcd(seq_len, GLA_CHUNK)
    tl = min(seq_len, SEQ_TILE)
    assert seq_len % tl == 0 and tl % chunk == 0
    nt = seq_len // tl
    n = n_seq * seq_len
    n_group = math.gcd(n_seq, max(1, GLA_STEP_CHUNKS * chunk // tl))
    seq3 = lambda a: a.reshape(n_seq, seq_len, a.shape[-1])
    row = lambda w: pl.BlockSpec((n_group, tl, w), lambda b, t: (b, t, 0))
    st_spec = pl.BlockSpec((n_group, DV_GLA, GLA_KW), lambda b, t: (b, 0, 0))
    o, st = pl.pallas_call(
        functools.partial(_gla_kernel, chunk=chunk),
        grid=(n_seq // n_group, nt),
        in_specs=[row(GLA_KW), row(GLA_KW), row(GLA_VW), row(GLA_KW), row(GLA_VW), st_spec,
                  _const_spec((1, GLA_VW))],
        out_specs=[row(GLA_VW), st_spec],
        out_shape=[jax.ShapeDtypeStruct((n_seq, seq_len, GLA_VW), out_dtype),
                   jax.ShapeDtypeStruct((n_seq, DV_GLA, GLA_KW), F32)],
        scratch_shapes=[pltpu.VMEM((n_group, DV_GLA, GLA_KW), F32)],
        compiler_params=_params(2),
        name="gla",
    )(seq3(qg), seq3(kg), seq3(vg), seq3(la), seq3(rg), state_t, nw)
    return o.reshape(n, GLA_VW), st


def _state_to_t(s):
    n = s.shape[0]
    return jnp.transpose(s, (0, 3, 1, 2)).reshape(n, DV_GLA, GLA_KW)


def _state_from_t(st):
    n = st.shape[0]
    return jnp.transpose(st.reshape(n, DV_GLA, H_GLA, DK_GLA), (0, 2, 3, 1))


def _cross_kernel(q_ref, mk_ref, mv_ref, o_ref):
    scale = HD_CROSS ** -0.5
    small = q_ref.shape[1] < 2 * SUBLANES
    cast = (lambda a: a) if small else (lambda a: a.astype(BF16))
    units = [(g, slice(h * HD_CROSS, (h + 1) * HD_CROSS)) for g in range(q_ref.shape[0]) for h in range(H_CROSS)]
    s = [_dot_nt(cast(q_ref[g, :, lanes]), cast(mk_ref[g, :, lanes])) * scale for g, lanes in units]
    p = [jnp.exp(si - jnp.max(si, axis=-1, keepdims=True)) for si in s]
    l = [jnp.sum(pi, axis=-1, keepdims=True) for pi in p]
    for (g, lanes), pi, li in zip(units, p, l):
        o_ref[g, :, lanes] = (_dot(cast(pi), cast(mv_ref[g, :, lanes])) / li).astype(o_ref.dtype)


def _cross(qc, mk, mv, n_seq, seq_len, n_mem, out_dtype):
    tl = min(seq_len, CROSS_STEP_ROWS)
    assert seq_len % tl == 0
    nt = seq_len // tl
    n_group = math.gcd(n_seq, min(CROSS_MAX_GROUP, max(1, CROSS_STEP_ROWS // tl)))
    row = pl.BlockSpec((n_group, tl, CROSS_W), lambda b, t: (b, t, 0))
    mem = pl.BlockSpec((n_group, n_mem, CROSS_W), lambda b, t: (b, 0, 0))
    o = pl.pallas_call(
        _cross_kernel,
        grid=(n_seq // n_group, nt),
        in_specs=[row, mem, mem],
        out_specs=row,
        out_shape=jax.ShapeDtypeStruct((n_seq, seq_len, CROSS_W), out_dtype),
        compiler_params=_params(2),
        name="cross",
    )(qc.reshape(n_seq, seq_len, CROSS_W), mk.reshape(n_seq, n_mem, CROSS_W), mv.reshape(n_seq, n_mem, CROSS_W))
    return o.reshape(n_seq * seq_len, CROSS_W)


def _merge_kernel(x_ref, om_ref, og_ref, oc_ref, nw_ref, wg_ref, bg_ref, wbm_ref, wbg_ref, wbc_ref,
                  wo_ref, h_ref):
    x = x_ref[...]
    d = x.shape[1]
    xb = _rms(x, nw_ref[...]).astype(BF16)
    merged = jnp.zeros(x.shape, F32)
    for i, (o_ref, wb_ref) in enumerate(((om_ref, wbm_ref), (og_ref, wbg_ref), (oc_ref, wbc_ref))):
        cols = slice(i * d, (i + 1) * d)
        gate = _sigmoid(_dot(xb, wg_ref[:, cols]) + bg_ref[:, cols])
        merged = merged + gate * _dot(o_ref[...].astype(BF16), wb_ref[...])
    h_ref[...] = x + _dot(merged.astype(BF16), wo_ref[...])


def _merge(x2d, om, og, oc, nw, wg, bg, wbm, wbg, wbc, wo):
    n, d = x2d.shape
    tm = min(ROW_TILE, n)
    row = lambda w: pl.BlockSpec((tm, w), lambda i: (i, 0))
    return pl.pallas_call(
        _merge_kernel,
        grid=(n // tm,),
        in_specs=[row(d), row(MOBA_W), row(GLA_VW), row(CROSS_W), _const_spec((1, d)),
                  _const_spec(wg.shape), _const_spec(bg.shape), _const_spec(wbm.shape),
                  _const_spec(wbg.shape), _const_spec(wbc.shape), _const_spec(wo.shape)],
        out_specs=row(d),
        out_shape=jax.ShapeDtypeStruct((n, d), F32),
        compiler_params=_params(1),
        name="merge",
    )(x2d, om, og, oc, nw, wg, bg, wbm, wbg, wbc, wo)


def _gelu_tanh(x):
    return 0.5 * x * (1.0 + jnp.tanh(0.7978845608028654 * (x + 0.044715 * (x * x * x))))


def _ffn_body(h_ref, nw_ref, wup_ref, wc_ref, bc_ref, wdn_ref, nf_ref, y_ref, prev_rows, store_u):
    h = h_ref[...]
    hb = _rms(h, nw_ref[...]).astype(BF16)
    d_ff = wdn_ref.shape[0]
    n_col_chunks = 2
    fc = d_ff // n_col_chunks
    assert fc % LANES == 0
    acc = jnp.zeros(h.shape, F32)
    for c in range(n_col_chunks):
        cols = slice(c * fc, (c + 1) * fc)
        u = _dot(hb, wup_ref[:, cols])
        gate = _dot(hb, wup_ref[:, d_ff + c * fc:d_ff + (c + 1) * fc])
        u1, u2 = prev_rows(cols, u)
        store_u(cols, u)
        conv = bc_ref[:, cols] + wc_ref[0:1, cols] * u2 + wc_ref[1:2, cols] * u1 + wc_ref[2:3, cols] * u
        act = (_gelu_tanh(conv) * gate).astype(BF16)
        acc = acc + _dot(act, wdn_ref[cols, :])
    y_ref[...] = _rms(h + acc, nf_ref[...])


def _ffn_long_kernel(h_ref, nw_ref, wup_ref, wc_ref, bc_ref, wdn_ref, nf_ref, y_ref, cs_ref, carry_sc):
    tm = h_ref.shape[0]

    @pl.when(pl.program_id(1) == 0)
    def _():
        carry_sc[...] = jnp.zeros(carry_sc.shape, F32)

    row = lax.broadcasted_iota(jnp.int32, (tm, 1), 0)

    def prev_rows(cols, u):
        last = carry_sc[SUBLANES - 1:SUBLANES, cols]
        last2 = carry_sc[SUBLANES - 2:SUBLANES - 1, cols]
        u1 = jnp.where(row == 0, last, pltpu.roll(u, 1, axis=0))
        u2 = jnp.where(row == 0, last2, jnp.where(row == 1, last, pltpu.roll(u, 2, axis=0)))
        return u1, u2

    def store_u(cols, u):
        carry_sc[:, cols] = u[tm - SUBLANES:tm]
        cs_ref[0, :, cols] = u[tm - (CONV_W - 1):tm]

    _ffn_body(h_ref, nw_ref, wup_ref, wc_ref, bc_ref, wdn_ref, nf_ref, y_ref, prev_rows, store_u)


def _ffn_short_kernel(h_ref, p1_ref, p2_ref, nw_ref, wup_ref, wc_ref, bc_ref, wdn_ref, nf_ref,
                      y_ref, u_ref, *, seq_len):
    tm = h_ref.shape[0]
    pos = lax.broadcasted_iota(jnp.int32, (tm, 1), 0) % seq_len

    def prev_rows(cols, u):
        u1 = jnp.where(pos >= 1, pltpu.roll(u, 1, axis=0), p1_ref[:, cols])
        u2 = jnp.where(pos >= 2, pltpu.roll(u, 2, axis=0), p2_ref[:, cols])
        return u1, u2

    def store_u(cols, u):
        u_ref[:, cols] = u

    _ffn_body(h_ref, nw_ref, wup_ref, wc_ref, bc_ref, wdn_ref, nf_ref, y_ref, prev_rows, store_u)


def _ffn_weights_specs(d, wup, wc, bc, wdn):
    return [_const_spec((1, d)), _const_spec(wup.shape), _const_spec(wc.shape), _const_spec(bc.shape),
            _const_spec(wdn.shape), _const_spec((1, d))]


def _ffn_long(h2d, n_seq, seq_len, nw, wup, wc, bc, wdn, nf):
    n, d = h2d.shape
    d_ff = wdn.shape[0]
    tm = min(ROW_TILE, n)
    assert seq_len % tm == 0
    nt = seq_len // tm
    row = pl.BlockSpec((tm, d), lambda b, t: (b * nt + t, 0))
    return pl.pallas_call(
        _ffn_long_kernel,
        grid=(n_seq, nt),
        in_specs=[row] + _ffn_weights_specs(d, wup, wc, bc, wdn),
        out_specs=[row, pl.BlockSpec((1, CONV_W - 1, d_ff), lambda b, t: (b, 0, 0))],
        out_shape=[jax.ShapeDtypeStruct((n, d), F32),
                   jax.ShapeDtypeStruct((n_seq, CONV_W - 1, d_ff), F32)],
        scratch_shapes=[pltpu.VMEM((SUBLANES, d_ff), F32)],
        compiler_params=_params(2),
        name="ffn_long",
    )(h2d, nw, wup, wc, bc, wdn, nf)


def _ffn_short(h2d, conv_prev, n_seq, seq_len, nw, wup, wc, bc, wdn, nf):
    n, d = h2d.shape
    d_ff = wdn.shape[0]
    tm = min(ROW_TILE, n)
    assert tm % seq_len == 0 and n % tm == 0 and seq_len >= CONV_W - 1
    zeros = jnp.zeros((n_seq, seq_len, d_ff), F32)
    p1 = zeros.at[:, 0].set(conv_prev[:, 1]).reshape(n, d_ff)
    p2 = zeros.at[:, 0].set(conv_prev[:, 0]).at[:, 1].set(conv_prev[:, 1]).reshape(n, d_ff)
    row = lambda w: pl.BlockSpec((tm, w), lambda i: (i, 0))
    y, u = pl.pallas_call(
        functools.partial(_ffn_short_kernel, seq_len=seq_len),
        grid=(n // tm,),
        in_specs=[row(d), row(d_ff), row(d_ff)] + _ffn_weights_specs(d, wup, wc, bc, wdn),
        out_specs=[row(d), row(d_ff)],
        out_shape=[jax.ShapeDtypeStruct((n, d), F32), jax.ShapeDtypeStruct((n, d_ff), F32)],
        compiler_params=_params(1),
        name="ffn_short",
    )(h2d, p1, p2, nw, wup, wc, bc, wdn, nf)
    return y, u.reshape(n_seq, seq_len, d_ff)[:, seq_len - (CONV_W - 1):]


def _prep_weights(norm_mix, w_in, w_gla_a2, b_gla_a, norm_gla, norm_mem, w_mem_kv, w_br_moba, w_br_gla,
                  w_br_cross, w_gate, b_gate, w_out, norm_ffn, w_up, w_conv, b_conv, w_down, norm_final):
    d = w_in.shape[0]
    o_a = C_QC
    w_in_p = jnp.concatenate(
        [w_in[:, :o_a], w_in[:, o_a + GLA_RANK:], w_in[:, o_a:o_a + GLA_RANK],
         jnp.zeros((d, A_PAD - GLA_RANK), w_in.dtype)], axis=1).astype(BF16)
    wa2_p = jnp.concatenate([w_gla_a2, jnp.zeros((A_PAD - GLA_RANK, GLA_KW), w_gla_a2.dtype)],
                            axis=0).astype(BF16)
    r2 = lambda a: a.reshape(1, -1)
    return dict(
        norm_mix=r2(norm_mix), w_in=w_in_p, wa2=wa2_p, ba=r2(b_gla_a), norm_gla=r2(norm_gla),
        norm_mem=r2(norm_mem), w_mem_kv=w_mem_kv.astype(BF16), wbm=w_br_moba.astype(BF16),
        wbg=w_br_gla.astype(BF16), wbc=w_br_cross.astype(BF16), wg=w_gate.astype(BF16), bg=r2(b_gate),
        wo=w_out.astype(BF16), norm_ffn=r2(norm_ffn), wup=w_up.astype(BF16), wc=w_conv, bc=r2(b_conv),
        wdn=w_down.astype(BF16), norm_final=r2(norm_final))


def kernel(x_prompt, x_sample, cache_moba_k, cache_moba_v, state_gla, state_conv, cache_mem_k, cache_mem_v, page_table, mem_prompt, norm_mix, w_in, w_gla_a2, b_gla_a, norm_gla, norm_mem, w_mem_kv, w_br_moba, w_br_gla, w_br_cross, w_gate, b_gate, w_out, norm_ffn, w_up, w_conv, b_conv, w_down, norm_final):
    depth = w_in.shape[0]
    assert depth == 1
    bp, sp, d = x_prompt.shape
    bs, ss, _ = x_sample.shape
    n_mem = mem_prompt.shape[1]
    w = _prep_weights(norm_mix[0], w_in[0], w_gla_a2[0], b_gla_a[0], norm_gla[0], norm_mem[0], w_mem_kv[0],
                      w_br_moba[0], w_br_gla[0], w_br_cross[0], w_gate[0], b_gate[0], w_out[0], norm_ffn[0],
                      w_up[0], w_conv[0], b_conv[0], w_down[0], norm_final)

    def mix(x2d, o_m, o_g, o_c):
        return _merge(x2d, o_m, o_g, o_c, w["norm_mix"], w["wg"], w["bg"], w["wbm"], w["wbg"], w["wbc"], w["wo"])

    ffn_w = (w["norm_ffn"], w["wup"], w["wc"], w["bc"], w["wdn"], w["norm_final"])

    xp = x_prompt.reshape(bp * sp, d)
    qm, kt_p, vt_p, qg, kg, vg, rg, qc, la, ksum, kb, vb = _project(
        xp, w["norm_mix"], w["w_in"], w["wa2"], w["ba"], BF16, MOBA_Q_SCALE, kv_seq_len=sp)
    o_m = _moba_prompt(qm, kb, vb, ksum.reshape(-1, MOBA_W), bp, sp)
    mk_p, mv_p = _memory_kv(mem_prompt.reshape(bp * n_mem, d), w["norm_mem"], w["w_mem_kv"])
    o_g, gla_p = _gla(qg, kg, vg, la, rg, jnp.zeros((bp, DV_GLA, GLA_KW), F32), w["norm_gla"], bp, sp, BF16)
    o_c = _cross(qc, mk_p, mv_p, bp, sp, n_mem, BF16)
    h_p = mix(xp, o_m, o_g, o_c)
    y_p, conv_p = _ffn_long(h_p, bp, sp, *ffn_w)

    xs = x_sample.reshape(bs * ss, d)
    qm, k_s, v_s, qg, kg, vg, rg, qc, la, _, _, _ = _project(
        xs, w["norm_mix"], w["w_in"], w["wa2"], w["ba"], F32, 1.0)
    r3 = lambda a: a.reshape(bs, ss, MOBA_W)
    o_m = _moba_sample(r3(qm), r3(k_s), r3(v_s), cache_moba_k[0], cache_moba_v[0],
                       page_table).reshape(bs * ss, MOBA_W)
    o_g, gla_s = _gla(qg, kg, vg, la, rg, _state_to_t(state_gla[0]), w["norm_gla"], bs, ss, F32)
    o_c = _cross(qc, cache_mem_k[0].reshape(bs * n_mem, CROSS_W), cache_mem_v[0].reshape(bs * n_mem, CROSS_W),
                 bs, ss, n_mem, F32)
    h_s = mix(xs, o_m, o_g, o_c)
    y_s, conv_s = _ffn_short(h_s, state_conv[0], bs, ss, *ffn_w)

    kv5 = lambda a, b, s: a.reshape(1, b, s, H_MOBA, HD_MOBA)
    kv5_t = lambda a: jnp.transpose(a.reshape(bp, H_MOBA, HD_MOBA, sp), (0, 3, 1, 2))[None]
    return (y_p.reshape(bp, sp, d), y_s.reshape(bs, ss, d),
            kv5_t(kt_p), kv5_t(vt_p), kv5(k_s, bs, ss), kv5(v_s, bs, ss),
            _state_from_t(gla_p)[None], _state_from_t(gla_s)[None],
            conv_p[None], conv_s[None],
            mk_p.reshape(1, bp, n_mem, H_CROSS, HD_CROSS), mv_p.reshape(1, bp, n_mem, H_CROSS, HD_CROSS))
```

```python
import functools
import math

import jax
import jax.numpy as jnp
from jax import lax
from jax.experimental import pallas as pl
from jax.experimental.pallas import tpu as pltpu

F32 = jnp.float32
BF16 = jnp.bfloat16
NEG_INF = float("-inf")

EPS = 1e-6
H_MOBA, HD_MOBA = 8, 64
MOBA_W = H_MOBA * HD_MOBA
MOBA_BLOCK = 256
MOBA_TOPK = 3
PAGE_SIZE = 128
H_GLA, DK_GLA, DV_GLA = 4, 64, 128
GLA_KW, GLA_VW = H_GLA * DK_GLA, H_GLA * DV_GLA
GLA_RANK = 16
GLA_TAU = 16.0
GLA_CHUNK = 64
GLA_SUB = 16
H_CROSS, HD_CROSS = 4, 128
CROSS_W = H_CROSS * HD_CROSS
N_BRANCH = 3
CONV_W = 3

LANES = 128
SUBLANES = 8
VMEM_LIMIT_BYTES = 56 * 1024 * 1024

MASK_BIAS = -1e30
MOBA_GROUP = 8
MOBA_TQ = 2 * MOBA_BLOCK
MOBA_Q_SCALE = HD_MOBA ** -0.5 * math.log2(math.e)
ROW_TILE = 512
SEQ_TILE = 256
GLA_STEP_CHUNKS = 8
CROSS_STEP_ROWS = 512
CROSS_MAX_GROUP = 8
PAGES_PER_STEP = 16
A_PAD = LANES

C_QM, C_K, C_V = 0, MOBA_W, 2 * MOBA_W
C_QG = 3 * MOBA_W
C_KG = C_QG + GLA_KW
C_VG = C_KG + GLA_KW
C_RG = C_VG + GLA_VW
C_QC = C_RG + GLA_VW
C_A = C_QC + CROSS_W
IN_COLS_PAD = C_A + A_PAD


def _params(n_axes):
    return pltpu.CompilerParams(
        dimension_semantics=("arbitrary",) * n_axes,
        vmem_limit_bytes=VMEM_LIMIT_BYTES,
    )


def _const_spec(shape):
    nd = len(shape)
    return pl.BlockSpec(shape, lambda *_: (0,) * nd, pipeline_mode=pl.Buffered(1))


def _rms(x, w):
    return x * lax.rsqrt(jnp.mean(x * x, axis=-1, keepdims=True) + EPS) * w


def _sigmoid(x):
    return 1.0 / (1.0 + jnp.exp(-x))


def _dot(a, b):
    return jnp.dot(a, b, preferred_element_type=F32)


def _dot_nt(a, b):
    return lax.dot_general(a, b, (((1,), (1,)), ((), ())), preferred_element_type=F32)


def _dot_tn(a, b):
    return lax.dot_general(a, b, (((0,), (0,)), ((), ())), preferred_element_type=F32)


def _proj_kernel(x_ref, nw_ref, w_ref, wa2_ref, ba_ref,
                 qm_ref, k_ref, v_ref, qg_ref, kg_ref, vg_ref, rg_ref, qc_ref, la_ref, ksum_ref,
                 kb_ref, vb_ref, *, qm_scale, kv_transposed):
    xb = _rms(x_ref[...], nw_ref[...]).astype(BF16)

    def mm(lo, hi):
        return _dot(xb, w_ref[:, lo:hi])

    def store_kv(ref, val):
        if kv_transposed:
            ref[0] = val.T
        else:
            ref[...] = val

    qm_ref[...] = (mm(C_QM, C_K) * qm_scale).astype(qm_ref.dtype)
    k = mm(C_K, C_V)
    store_kv(k_ref, k)
    kb_ref[...] = k.astype(BF16)
    for g in range(k.shape[0] // MOBA_BLOCK):
        ksum_ref[g] = jnp.sum(k[g * MOBA_BLOCK:(g + 1) * MOBA_BLOCK], axis=0, keepdims=True)
    v = mm(C_V, C_QG)
    store_kv(v_ref, v)
    vb_ref[...] = v.astype(BF16)
    qg_ref[...] = mm(C_QG, C_KG).astype(qg_ref.dtype)
    kg_ref[...] = mm(C_KG, C_VG).astype(kg_ref.dtype)
    vg_ref[...] = mm(C_VG, C_RG).astype(vg_ref.dtype)
    rg_ref[...] = mm(C_RG, C_QC)
    qc_ref[...] = mm(C_QC, C_A).astype(qc_ref.dtype)
    z = _dot(mm(C_A, IN_COLS_PAD).astype(BF16), wa2_ref[...]) + ba_ref[...]
    la_ref[...] = (jnp.minimum(z, 0.0) - jnp.log1p(jnp.exp(-jnp.abs(z)))) * (1.0 / GLA_TAU)


def _project(x2d, nw, w_in_p, wa2_p, ba, act_dtype, qm_scale, kv_seq_len=None):
    n, d = x2d.shape
    tm = min(ROW_TILE, n)
    assert n % tm == 0 and tm % MOBA_BLOCK == 0
    row = lambda w: pl.BlockSpec((tm, w), lambda i: (i, 0))
    widths = (MOBA_W, MOBA_W, MOBA_W, GLA_KW, GLA_KW, GLA_VW, GLA_VW, CROSS_W, GLA_KW)
    dtypes = (act_dtype, F32, F32, act_dtype, act_dtype, act_dtype, F32, act_dtype, F32)
    out_shape = [jax.ShapeDtypeStruct((n, w), dt) for w, dt in zip(widths, dtypes)]
    out_shape.append(jax.ShapeDtypeStruct((n // MOBA_BLOCK, 1, MOBA_W), F32))
    out_shape += [jax.ShapeDtypeStruct((n, MOBA_W), BF16)] * 2
    out_specs = [row(w) for w in widths]
    out_specs.append(pl.BlockSpec((tm // MOBA_BLOCK, 1, MOBA_W), lambda i: (i, 0, 0)))
    out_specs += [row(MOBA_W)] * 2
    if kv_seq_len is not None:
        assert kv_seq_len % tm == 0
        nt = kv_seq_len // tm
        for i in (1, 2):
            out_shape[i] = jax.ShapeDtypeStruct((n // kv_seq_len, MOBA_W, kv_seq_len), F32)
            out_specs[i] = pl.BlockSpec((1, MOBA_W, tm), lambda i: (i // nt, 0, i % nt))
    return pl.pallas_call(
        functools.partial(_proj_kernel, qm_scale=qm_scale, kv_transposed=kv_seq_len is not None),
        grid=(n // tm,),
        in_specs=[row(d), _const_spec((1, d)), _const_spec(w_in_p.shape),
                  _const_spec(wa2_p.shape), _const_spec((1, GLA_KW))],
        out_specs=out_specs,
        out_shape=out_shape,
        compiler_params=_params(1),
        name="proj",
    )(x2d, nw, w_in_p, wa2_p, ba)


def _memkv_kernel(m_ref, nw_ref, w_ref, mk_ref, mv_ref):
    mb = _rms(m_ref[...], nw_ref[...]).astype(BF16)
    mk_ref[...] = _dot(mb, w_ref[:, :CROSS_W])
    mv_ref[...] = _dot(mb, w_ref[:, CROSS_W:])


def _memory_kv(mem2d, nw, w_kv):
    n, d = mem2d.shape
    tm = min(ROW_TILE, n)
    assert n % tm == 0
    return pl.pallas_call(
        _memkv_kernel,
        grid=(n // tm,),
        in_specs=[pl.BlockSpec((tm, d), lambda i: (i, 0)), _const_spec((1, d)), _const_spec(w_kv.shape)],
        out_specs=[pl.BlockSpec((tm, CROSS_W), lambda i: (i, 0))] * 2,
        out_shape=[jax.ShapeDtypeStruct((n, CROSS_W), F32)] * 2,
        compiler_params=_params(1),
        name="memkv",
    )(mem2d, nw, w_kv)


def _select_topk(gate, idx, valid, axis):
    return _select_topk_each([gate], idx, valid, axis)[0]


def _select_topk_each(gates, idx, valid, axis):
    gs = [jnp.where(valid, gate, NEG_INF) for gate in gates]
    picked = [jnp.zeros(gate.shape, jnp.bool_) for gate in gates]
    for _ in range(MOBA_TOPK):
        ms = [jnp.max(g, axis=axis, keepdims=True) for g in gs]
        firsts = [jnp.min(jnp.where(g == m, idx, float(2 ** 24)), axis=axis, keepdims=True) for g, m in zip(gs, ms)]
        picks = [(idx == first) & (m > NEG_INF) & valid for first, m in zip(firsts, ms)]
        picked = [a | b for a, b in zip(picked, picks)]
        gs = [jnp.where(pick, NEG_INF, g) for pick, g in zip(picks, gs)]
    return picked


def _moba_prompt_kernel(q_ref, k_ref, v_ref, ksum_ref, o_ref, ka_sc, va_sc, s_sc):
    t = pl.program_id(2)
    tq = q_ref.shape[0]
    blk = MOBA_BLOCK
    nblk = ksum_ref.shape[0]
    seq_len = k_ref.shape[0]
    group = s_sc.shape[3] // blk
    n_heads = LANES // HD_MOBA
    spare_off = [((half + 1) % n_heads) * HD_MOBA for half in range(n_heads)]

    @pl.when(t == 0)
    def _():
        key_blk = lax.broadcasted_iota(jnp.int32, (seq_len, LANES), 0) // blk
        key_lane = lax.broadcasted_iota(jnp.int32, (seq_len, LANES), 1)
        k = k_ref[...]
        v = v_ref[...]
        for half in range(n_heads):
            in_head = (key_lane >= half * HD_MOBA) & (key_lane < (half + 1) * HD_MOBA)
            onehot = jnp.where(key_lane - spare_off[half] == key_blk, 1.0, 0.0).astype(BF16)
            ka_sc[half] = jnp.where(in_head, k, onehot)
            va_sc[half] = jnp.where(in_head, v, jnp.ones_like(v))

    q = q_ref[...]
    kmean = ksum_ref[...] * (1.0 / blk)
    km_hi = kmean.astype(BF16)
    km_lo = (kmean - km_hi.astype(F32)).astype(BF16)
    lane = lax.broadcasted_iota(jnp.int32, (1, LANES), 1)
    causal = (lax.broadcasted_iota(jnp.int32, (blk, blk), 1) <= lax.broadcasted_iota(jnp.int32, (blk, blk), 0))
    blk_id = lax.broadcasted_iota(jnp.int32, (nblk, tq), 0)
    place_row = lax.broadcasted_iota(jnp.int32, (nblk, LANES), 0)
    place_lane = lax.broadcasted_iota(jnp.int32, (nblk, LANES), 1)
    first_blk = t * (tq // blk)
    own_blk = first_blk + lax.broadcasted_iota(jnp.int32, (nblk, tq), 1) // blk
    own_parts = range(tq // blk)
    q_rows = [slice(j * blk, (j + 1) * blk) for j in own_parts]
    k_rows = [pl.ds(pl.multiple_of(t * tq + j * blk, blk), blk) for j in own_parts]
    n_groups = (first_blk + tq // blk - 1 + group - 1) // group
    heads = range(n_heads)

    in_head = [(lane >= half * HD_MOBA) & (lane < (half + 1) * HD_MOBA) for half in heads]
    qh = [jnp.where(in_head[half], q, jnp.zeros_like(q)) for half in heads]
    gate_t = [_dot_nt(km_hi, qh[half]) + _dot_nt(km_lo, qh[half]) for half in heads]
    s_own = [[jnp.where(causal, _dot_nt(qh[half][q_rows[j]], ka_sc[half, k_rows[j], :]), MASK_BIAS)
              for j in own_parts] for half in heads]
    picked_t = _select_topk_each(gate_t, blk_id.astype(F32), blk_id < own_blk, axis=0)
    place = [jnp.where(place_lane - spare_off[half] == place_row, 1.0, 0.0).astype(BF16) for half in heads]
    picked = [_dot_tn(jnp.where(picked_t[half], 1.0, 0.0).astype(BF16), place[half]) for half in heads]
    q_aug = []
    for half in heads:
        in_range = (lane >= spare_off[half]) & (lane < spare_off[half] + nblk)
        bias = jnp.where(in_range & (picked[half] < 0.5), MASK_BIAS, 0.0)
        q_aug.append(qh[half] + bias.astype(BF16))

    def group_rows(g):
        return pl.ds(pl.multiple_of(g * (group * blk), group * blk), group * blk)

    def score_pass(g, mxs):
        out = []
        for half in heads:
            s = _dot_nt(q_aug[half], ka_sc[half, group_rows(g), :])
            s_sc[half, g] = s
            mx = mxs[half]
            for i in range(group):
                mx = jnp.maximum(mx, s[:, i * blk:(i + 1) * blk])
            out.append(mx)
        return tuple(out)

    mx0 = tuple(jnp.concatenate(s_own[half], axis=0) for half in heads)
    mxs = lax.fori_loop(0, n_groups, score_pass, mx0)
    m = [jnp.max(mx, axis=-1, keepdims=True) for mx in mxs]

    def value_pass(g, accs):
        out = []
        for half in heads:
            p = jnp.exp2(s_sc[half, g] - m[half])
            out.append(accs[half] + _dot(p.astype(BF16), va_sc[half, group_rows(g), :]))
        return tuple(out)

    acc0 = tuple(
        jnp.concatenate([_dot(jnp.exp2(s_own[half][j] - m[half][q_rows[j]]).astype(BF16), va_sc[half, k_rows[j], :])
                         for j in own_parts], axis=0)
        for half in heads)
    accs = lax.fori_loop(0, n_groups, value_pass, acc0)

    o = jnp.zeros((tq, LANES), F32)
    for half in heads:
        row_sum = accs[half][:, spare_off[half]:spare_off[half] + 1]
        o = jnp.where(in_head[half], accs[half] / row_sum, o)
    o_ref[...] = o.astype(o_ref.dtype)


def _moba_prompt(qm, kb, vb, ksum, n_seq, seq_len):
    assert seq_len % MOBA_BLOCK == 0
    nblk = seq_len // MOBA_BLOCK
    assert nblk <= HD_MOBA
    group = MOBA_GROUP if nblk % MOBA_GROUP == 0 else 1
    tq = MOBA_TQ if seq_len % MOBA_TQ == 0 else MOBA_BLOCK
    nt = seq_len // tq
    n = n_seq * seq_len
    n_pair = MOBA_W // LANES
    n_heads = LANES // HD_MOBA
    seq_block = pl.BlockSpec((seq_len, LANES), lambda b, hp, t: (b, hp), pipeline_mode=pl.Buffered(1))
    return pl.pallas_call(
        _moba_prompt_kernel,
        grid=(n_seq, n_pair, nt),
        in_specs=[
            pl.BlockSpec((tq, LANES), lambda b, hp, t: (b * nt + t, hp)),
            seq_block, seq_block,
            pl.BlockSpec((nblk, LANES), lambda b, hp, t: (b, hp)),
        ],
        out_specs=pl.BlockSpec((tq, LANES), lambda b, hp, t: (b * nt + t, hp)),
        out_shape=jax.ShapeDtypeStruct((n, MOBA_W), BF16),
        scratch_shapes=[pltpu.VMEM((n_heads, seq_len, LANES), BF16),
                        pltpu.VMEM((n_heads, seq_len, LANES), BF16),
                        pltpu.VMEM((n_heads, nblk // group, tq, group * MOBA_BLOCK), F32)],
        compiler_params=_params(3),
        name="moba_prompt",
    )(qm, kb, vb, ksum)


def _moba_sample_kernel(pt_ref, qbd_ref, kn_ref, vnt_ref, *rest, n_q):
    del pt_ref
    pps = PAGES_PER_STEP
    k_pages, v_pages = rest[:pps], rest[pps:2 * pps]
    o_ref, m_sc, l_sc, gate_sc, acc_sc = rest[2 * pps:]
    s_id = pl.program_id(1)
    n_step = pl.num_programs(1)
    nblk = m_sc.shape[1]
    n_hq = qbd_ref.shape[1]
    qbd = qbd_ref[0]
    col_head = lax.broadcasted_iota(jnp.int32, (HD_MOBA, n_hq), 1) // n_q
    blk_col = lax.broadcasted_iota(jnp.int32, (n_hq, nblk), 1)
    pages_per_blk = MOBA_BLOCK // PAGE_SIZE

    def own_head(pvt):
        out = pvt[0:HD_MOBA]
        for h in range(1, H_MOBA):
            out = jnp.where(col_head == h, pvt[h * HD_MOBA:(h + 1) * HD_MOBA], out)
        return out

    def block_t(page_refs, i):
        pages = range(i * pages_per_blk, (i + 1) * pages_per_blk)
        return jnp.concatenate([page_refs[j][0] for j in pages], axis=1).astype(BF16)

    blocks = range(pps // pages_per_blk)
    b_idx = [s_id * (pps // pages_per_blk) + i for i in blocks]
    s = [_dot(qbd, block_t(k_pages, i)) for i in blocks]
    gate = [jnp.sum(si, axis=-1, keepdims=True) for si in s]
    m = [jnp.max(si, axis=-1, keepdims=True) for si in s]
    p = [jnp.exp(si - mi) for si, mi in zip(s, m)]
    l = [jnp.sum(pi, axis=-1, keepdims=True) for pi in p]
    pv = [own_head(_dot_nt(block_t(v_pages, i), p[i].astype(BF16))) for i in blocks]
    for i in blocks:
        acc_sc[b_idx[i]] = pv[i]
    for ref, vals in ((m_sc, m), (l_sc, l), (gate_sc, gate)):
        cur = ref[...]
        for i in blocks:
            cur = jnp.where(blk_col == b_idx[i], vals[i], cur)
        ref[...] = cur

    @pl.when(s_id == n_step - 1)
    def _():
        picked = _select_topk(gate_sc[...], blk_col.astype(F32), blk_col >= 0, axis=1)
        m_all = jnp.where(picked, m_sc[...], NEG_INF)
        kn = kn_ref[0].astype(BF16)
        vnt = vnt_ref[0].astype(BF16)
        n_new = kn.shape[0]
        key_i = lax.broadcasted_iota(jnp.int32, (n_hq, n_new), 1)
        q_i = lax.broadcasted_iota(jnp.int32, (n_hq, n_new), 0) % n_q
        s_own = jnp.where(key_i <= q_i, _dot_nt(qbd, kn), NEG_INF)
        m_tot = jnp.maximum(jnp.max(s_own, axis=-1, keepdims=True),
                            jnp.max(m_all, axis=-1, keepdims=True))
        p_own = jnp.exp(s_own - m_tot)
        c = jnp.exp(m_all - m_tot)
        l_tot = jnp.sum(p_own, axis=-1, keepdims=True) + jnp.sum(c * l_sc[...], axis=-1, keepdims=True)
        inv_l = 1.0 / l_tot
        c_t = jnp.transpose(c * inv_l)
        out = own_head(_dot_nt(vnt, (p_own * inv_l).astype(BF16)))
        for b in range(nblk):
            out = out + acc_sc[b] * c_t[b:b + 1, :]
        o_ref[0] = out


def _moba_sample(q, k_new, v_new, cache_k, cache_v, page_table):
    n_seq, n_q, _ = q.shape
    n_phys = cache_k.shape[0]
    n_pages = page_table.shape[1]
    past = n_pages * PAGE_SIZE
    assert past % MOBA_BLOCK == 0 and n_pages % PAGES_PER_STEP == 0
    assert n_q <= MOBA_BLOCK and n_q % SUBLANES == 0
    nblk = past // MOBA_BLOCK
    n_hq = H_MOBA * n_q
    lane_head = jnp.arange(MOBA_W) // HD_MOBA
    qbd = jnp.where(lane_head[None, None, None, :] == jnp.arange(H_MOBA)[None, :, None, None],
                    q[:, None, :, :] * (HD_MOBA ** -0.5), 0.0).reshape(n_seq, n_hq, MOBA_W).astype(BF16)
    n_new = max(2 * SUBLANES, n_q)
    pad = ((0, 0), (0, n_new - n_q), (0, 0))
    kn = jnp.pad(k_new, pad)
    vnt = jnp.transpose(jnp.pad(v_new, pad), (0, 2, 1))
    to_t = lambda c: jnp.transpose(c, (0, 2, 3, 1)).reshape(n_phys, MOBA_W, PAGE_SIZE)

    def page_spec(i):
        return pl.BlockSpec((1, MOBA_W, PAGE_SIZE), lambda b, s, pt: (pt[b, s * PAGES_PER_STEP + i], 0, 0))

    seq3 = lambda r, c: pl.BlockSpec((1, r, c), lambda b, s, pt: (b, 0, 0))
    grid_spec = pltpu.PrefetchScalarGridSpec(
        num_scalar_prefetch=1,
        grid=(n_seq, n_pages // PAGES_PER_STEP),
        in_specs=[seq3(n_hq, MOBA_W), seq3(n_new, MOBA_W), seq3(MOBA_W, n_new)]
        + [page_spec(i) for i in range(PAGES_PER_STEP)] * 2,
        out_specs=seq3(HD_MOBA, n_hq),
        scratch_shapes=[
            pltpu.VMEM((n_hq, nblk), F32),
            pltpu.VMEM((n_hq, nblk), F32),
            pltpu.VMEM((n_hq, nblk), F32),
            pltpu.VMEM((nblk, HD_MOBA, n_hq), F32),
        ],
    )
    out = pl.pallas_call(
        functools.partial(_moba_sample_kernel, n_q=n_q),
        grid_spec=grid_spec,
        out_shape=jax.ShapeDtypeStruct((n_seq, HD_MOBA, n_hq), F32),
        compiler_params=_params(2),
        name="moba_sample",
    )(page_table, qbd, kn, vnt, *([to_t(cache_k)] * PAGES_PER_STEP), *([to_t(cache_v)] * PAGES_PER_STEP))
    return jnp.transpose(out.reshape(n_seq, HD_MOBA, H_MOBA, n_q), (0, 3, 2, 1)).reshape(n_seq, n_q, MOBA_W)


def _cumsum_rows(x):
    n = x.shape[0]
    row = lax.broadcasted_iota(jnp.int32, x.shape, 0)
    s = 1
    while s < n:
        x = x + jnp.where(row >= s, pltpu.roll(x, s, axis=0), 0.0)
        s *= 2
    return x


def _gla_chunk(q, k, v, la, st, cast):
    c = q.shape[0]
    sub = min(GLA_SUB, c)
    lane = lax.broadcasted_iota(jnp.int32, (1, GLA_KW), 1)
    head_masks = [(lane >= h * DK_GLA) & (lane < (h + 1) * DK_GLA) for h in range(H_GLA)]
    g = _cumsum_rows(la)
    krow = lax.broadcasted_iota(jnp.int32, (c, GLA_KW), 0)
    a_row = lax.broadcasted_iota(jnp.int32, (H_GLA * sub, c), 0)
    a_col = lax.broadcasted_iota(jnp.int32, (H_GLA * sub, c), 1)

    a_parts = []
    for i in range(c // sub):
        g_ref = jnp.zeros((1, GLA_KW), F32) if i == 0 else g[i * sub - 1:i * sub]
        qt = q[i * sub:(i + 1) * sub] * jnp.exp(g[i * sub:(i + 1) * sub] - g_ref)
        kt = jnp.where(krow < (i + 1) * sub, k * jnp.exp(g_ref - g), 0.0)
        q_stack = jnp.concatenate([jnp.where(hm, qt, 0.0) for hm in head_masks], axis=0)
        a = _dot_nt(cast(q_stack), cast(kt))
        a_parts.append(jnp.where(a_col <= i * sub + a_row % sub, a, 0.0))

    qe = q * jnp.exp(g)
    g_last = g[c - 1:c]
    k_dec = cast(k * jnp.exp(g_last - g))
    st_c = cast(st)
    outs = []
    st_new = st * jnp.exp(g_last)
    for h in range(H_GLA):
        a_h = jnp.concatenate([a[h * sub:(h + 1) * sub] for a in a_parts], axis=0)
        v_h = cast(v[:, h * DV_GLA:(h + 1) * DV_GLA])
        o_intra = _dot(cast(a_h), v_h)
        o_inter = _dot_nt(cast(jnp.where(head_masks[h], qe, 0.0)), st_c)
        outs.append(o_intra + o_inter)
        st_new = st_new + jnp.where(head_masks[h], _dot_tn(v_h, k_dec), 0.0)
    return outs, st_new


def _gla_kernel(q_ref, k_ref, v_ref, la_ref, r_ref, s0_ref, nw_ref, o_ref, sfin_ref, st_sc, *, chunk):
    t = pl.program_id(1)

    @pl.when(t == 0)
    def _():
        st_sc[...] = s0_ref[...]

    cast = (lambda a: a.astype(BF16)) if chunk >= 2 * SUBLANES else (lambda a: a)
    n_group = q_ref.shape[0]
    sts = [st_sc[g] for g in range(n_group)]
    for c in range(q_ref.shape[1] // chunk):
        rows = slice(c * chunk, (c + 1) * chunk)
        for g in range(n_group):
            q = q_ref[g, rows, :].astype(F32) * (DK_GLA ** -0.5)
            k = k_ref[g, rows, :].astype(F32)
            outs, sts[g] = _gla_chunk(q, k, v_ref[g, rows, :], la_ref[g, rows, :], sts[g], cast)
            for h, o in enumerate(outs):
                lanes = slice(h * DV_GLA, (h + 1) * DV_GLA)
                r = r_ref[g, rows, lanes]
                o = o * lax.rsqrt(jnp.mean(o * o, axis=-1, keepdims=True) + EPS)
                o_ref[g, rows, lanes] = (o * nw_ref[:, lanes] * (r * _sigmoid(r))).astype(o_ref.dtype)
    for g in range(n_group):
        st_sc[g] = sts[g]

    @pl.when(t == pl.num_programs(1) - 1)
    def _():
        for g in range(n_group):
            sfin_ref[g] = sts[g]


def _gla(qg, kg, vg, la, rg, state_t, nw, n_seq, seq_len, out_dtype):
    chunk = math.gcd(seq_len, GLA_CHUNK)
    tl = min(seq_len, SEQ_TILE)
    assert seq_len % tl == 0 and tl % chunk == 0
    nt = seq_len // tl
    n = n_seq * seq_len
    n_group = math.gcd(n_seq, max(1, GLA_STEP_CHUNKS * chunk // tl))
    seq3 = lambda a: a.reshape(n_seq, seq_len, a.shape[-1])
    row = lambda w: pl.BlockSpec((n_group, tl, w), lambda b, t: (b, t, 0))
    st_spec = pl.BlockSpec((n_group, DV_GLA, GLA_KW), lambda b, t: (b, 0, 0))
    o, st = pl.pallas_call(
        functools.partial(_gla_kernel, chunk=chunk),
        grid=(n_seq // n_group, nt),
        in_specs=[row(GLA_KW), row(GLA_KW), row(GLA_VW), row(GLA_KW), row(GLA_VW), st_spec,
                  _const_spec((1, GLA_VW))],
        out_specs=[row(GLA_VW), st_spec],
        out_shape=[jax.ShapeDtypeStruct((n_seq, seq_len, GLA_VW), out_dtype),
                   jax.ShapeDtypeStruct((n_seq, DV_GLA, GLA_KW), F32)],
        scratch_shapes=[pltpu.VMEM((n_group, DV_GLA, GLA_KW), F32)],
        compiler_params=_params(2),
        name="gla",
    )(seq3(qg), seq3(kg), seq3(vg), seq3(la), seq3(rg), state_t, nw)
    return o.reshape(n, GLA_VW), st


def _state_to_t(s):
    n = s.shape[0]
    return jnp.transpose(s, (0, 3, 1, 2)).reshape(n, DV_GLA, GLA_KW)


def _state_from_t(st):
    n = st.shape[0]
    return jnp.transpose(st.reshape(n, DV_GLA, H_GLA, DK_GLA), (0, 2, 3, 1))


def _cross_kernel(q_ref, mk_ref, mv_ref, o_ref, *, n_mem, rows_by_head):
    scale = HD_CROSS ** -0.5
    small = q_ref.shape[1] < 2 * SUBLANES
    cast = (lambda a: a) if small else (lambda a: a.astype(BF16))
    units = [(g, h) for g in range(q_ref.shape[0]) for h in range(H_CROSS)]
    lanes = lambda h: slice(h * HD_CROSS, (h + 1) * HD_CROSS)

    def mem_head(ref, g, h):
        if rows_by_head:
            return ref[g, pl.ds(h, n_mem, stride=H_CROSS), :]
        return ref[g, :, lanes(h)]

    s = [_dot_nt(cast(q_ref[g, :, lanes(h)]), cast(mem_head(mk_ref, g, h))) * scale for g, h in units]
    p = [jnp.exp(si - jnp.max(si, axis=-1, keepdims=True)) for si in s]
    l = [jnp.sum(pi, axis=-1, keepdims=True) for pi in p]
    for (g, h), pi, li in zip(units, p, l):
        o_ref[g, :, lanes(h)] = (_dot(cast(pi), cast(mem_head(mv_ref, g, h))) / li).astype(o_ref.dtype)


def _cross(qc, mk, mv, n_seq, seq_len, n_mem, out_dtype):
    tl = min(seq_len, CROSS_STEP_ROWS)
    assert seq_len % tl == 0
    nt = seq_len // tl
    n_group = math.gcd(n_seq, min(CROSS_MAX_GROUP, max(1, CROSS_STEP_ROWS // tl)))
    row = pl.BlockSpec((n_group, tl, CROSS_W), lambda b, t: (b, t, 0))
    rows_by_head = mk.ndim == 4
    mem_shape = (n_mem * H_CROSS, HD_CROSS) if rows_by_head else (n_mem, CROSS_W)
    mem = pl.BlockSpec((n_group,) + mem_shape, lambda b, t: (b, 0, 0))
    o = pl.pallas_call(
        functools.partial(_cross_kernel, n_mem=n_mem, rows_by_head=rows_by_head),
        grid=(n_seq // n_group, nt),
        in_specs=[row, mem, mem],
        out_specs=row,
        out_shape=jax.ShapeDtypeStruct((n_seq, seq_len, CROSS_W), out_dtype),
        compiler_params=_params(2),
        name="cross",
    )(qc.reshape(n_seq, seq_len, CROSS_W), mk.reshape((n_seq,) + mem_shape), mv.reshape((n_seq,) + mem_shape))
    return o.reshape(n_seq * seq_len, CROSS_W)


def _merge_kernel(x_ref, om_ref, og_ref, oc_ref, nw_ref, wg_ref, bg_ref, wbm_ref, wbg_ref, wbc_ref,
                  wo_ref, h_ref):
    x = x_ref[...]
    d = x.shape[1]
    xb = _rms(x, nw_ref[...]).astype(BF16)
    merged = jnp.zeros(x.shape, F32)
    for i, (o_ref, wb_ref) in enumerate(((om_ref, wbm_ref), (og_ref, wbg_ref), (oc_ref, wbc_ref))):
        cols = slice(i * d, (i + 1) * d)
        gate = _sigmoid(_dot(xb, wg_ref[:, cols]) + bg_ref[:, cols])
        merged = merged + gate * _dot(o_ref[...].astype(BF16), wb_ref[...])
    h_ref[...] = x + _dot(merged.astype(BF16), wo_ref[...])


def _merge(x2d, om, og, oc, nw, wg, bg, wbm, wbg, wbc, wo):
    n, d = x2d.shape
    tm = min(ROW_TILE, n)
    row = lambda w: pl.BlockSpec((tm, w), lambda i: (i, 0))
    return pl.pallas_call(
        _merge_kernel,
        grid=(n // tm,),
        in_specs=[row(d), row(MOBA_W), row(GLA_VW), row(CROSS_W), _const_spec((1, d)),
                  _const_spec(wg.shape), _const_spec(bg.shape), _const_spec(wbm.shape),
                  _const_spec(wbg.shape), _const_spec(wbc.shape), _const_spec(wo.shape)],
        out_specs=row(d),
        out_shape=jax.ShapeDtypeStruct((n, d), F32),
        compiler_params=_params(1),
        name="merge",
    )(x2d, om, og, oc, nw, wg, bg, wbm, wbg, wbc, wo)


def _gelu_tanh(x):
    return 0.5 * x * (1.0 + jnp.tanh(0.7978845608028654 * (x + 0.044715 * (x * x * x))))


def _ffn_body(h_ref, nw_ref, wup_ref, wc_ref, bc_ref, wdn_ref, nf_ref, y_ref, prev_rows, store_u):
    h = h_ref[...]
    hb = _rms(h, nw_ref[...]).astype(BF16)
    d_ff = wdn_ref.shape[0]
    n_col_chunks = 2
    fc = d_ff // n_col_chunks
    assert fc % LANES == 0
    acc = jnp.zeros(h.shape, F32)
    for c in range(n_col_chunks):
        cols = slice(c * fc, (c + 1) * fc)
        u = _dot(hb, wup_ref[:, cols])
        gate = _dot(hb, wup_ref[:, d_ff + c * fc:d_ff + (c + 1) * fc])
        u1, u2 = prev_rows(cols, u)
        store_u(cols, u)
        conv = bc_ref[:, cols] + wc_ref[0:1, cols] * u2 + wc_ref[1:2, cols] * u1 + wc_ref[2:3, cols] * u
        act = (_gelu_tanh(conv) * gate).astype(BF16)
        acc = acc + _dot(act, wdn_ref[cols, :])
    y_ref[...] = _rms(h + acc, nf_ref[...])


def _ffn_long_kernel(h_ref, nw_ref, wup_ref, wc_ref, bc_ref, wdn_ref, nf_ref, y_ref, cs_ref, carry_sc):
    tm = h_ref.shape[0]

    @pl.when(pl.program_id(1) == 0)
    def _():
        carry_sc[...] = jnp.zeros(carry_sc.shape, F32)

    row = lax.broadcasted_iota(jnp.int32, (tm, 1), 0)

    def prev_rows(cols, u):
        last = carry_sc[SUBLANES - 1:SUBLANES, cols]
        last2 = carry_sc[SUBLANES - 2:SUBLANES - 1, cols]
        u1 = jnp.where(row == 0, last, pltpu.roll(u, 1, axis=0))
        u2 = jnp.where(row == 0, last2, jnp.where(row == 1, last, pltpu.roll(u, 2, axis=0)))
        return u1, u2

    def store_u(cols, u):
        carry_sc[:, cols] = u[tm - SUBLANES:tm]
        cs_ref[0, :, cols] = u[tm - (CONV_W - 1):tm]

    _ffn_body(h_ref, nw_ref, wup_ref, wc_ref, bc_ref, wdn_ref, nf_ref, y_ref, prev_rows, store_u)


def _ffn_short_kernel(h_ref, p1_ref, p2_ref, nw_ref, wup_ref, wc_ref, bc_ref, wdn_ref, nf_ref,
                      y_ref, u_ref, *, seq_len):
    tm = h_ref.shape[0]
    pos = lax.broadcasted_iota(jnp.int32, (tm, 1), 0) % seq_len

    def prev_rows(cols, u):
        u1 = jnp.where(pos >= 1, pltpu.roll(u, 1, axis=0), p1_ref[:, cols])
        u2 = jnp.where(pos >= 2, pltpu.roll(u, 2, axis=0), p2_ref[:, cols])
        return u1, u2

    def store_u(cols, u):
        u_ref[:, cols] = u

    _ffn_body(h_ref, nw_ref, wup_ref, wc_ref, bc_ref, wdn_ref, nf_ref, y_ref, prev_rows, store_u)


def _ffn_weights_specs(d, wup, wc, bc, wdn):
    return [_const_spec((1, d)), _const_spec(wup.shape), _const_spec(wc.shape), _const_spec(bc.shape),
            _const_spec(wdn.shape), _const_spec((1, d))]


def _ffn_long(h2d, n_seq, seq_len, nw, wup, wc, bc, wdn, nf):
    n, d = h2d.shape
    d_ff = wdn.shape[0]
    tm = min(ROW_TILE, n)
    assert seq_len % tm == 0
    nt = seq_len // tm
    row = pl.BlockSpec((tm, d), lambda b, t: (b * nt + t, 0))
    return pl.pallas_call(
        _ffn_long_kernel,
        grid=(n_seq, nt),
        in_specs=[row] + _ffn_weights_specs(d, wup, wc, bc, wdn),
        out_specs=[row, pl.BlockSpec((1, CONV_W - 1, d_ff), lambda b, t: (b, 0, 0))],
        out_shape=[jax.ShapeDtypeStruct((n, d), F32),
                   jax.ShapeDtypeStruct((n_seq, CONV_W - 1, d_ff), F32)],
        scratch_shapes=[pltpu.VMEM((SUBLANES, d_ff), F32)],
        compiler_params=_params(2),
        name="ffn_long",
    )(h2d, nw, wup, wc, bc, wdn, nf)


def _ffn_short(h2d, conv_prev, n_seq, seq_len, nw, wup, wc, bc, wdn, nf):
    n, d = h2d.shape
    d_ff = wdn.shape[0]
    tm = min(ROW_TILE, n)
    assert tm % seq_len == 0 and n % tm == 0 and seq_len >= CONV_W - 1
    pos = jnp.arange(seq_len)[None, :, None]
    older, newer = conv_prev[:, 0][:, None, :], conv_prev[:, 1][:, None, :]
    p1 = jnp.where(pos == 0, newer, 0.0).reshape(n, d_ff)
    p2 = jnp.where(pos == 0, older, jnp.where(pos == 1, newer, 0.0)).reshape(n, d_ff)
    row = lambda w: pl.BlockSpec((tm, w), lambda i: (i, 0))
    y, u = pl.pallas_call(
        functools.partial(_ffn_short_kernel, seq_len=seq_len),
        grid=(n // tm,),
        in_specs=[row(d), row(d_ff), row(d_ff)] + _ffn_weights_specs(d, wup, wc, bc, wdn),
        out_specs=[row(d), row(d_ff)],
        out_shape=[jax.ShapeDtypeStruct((n, d), F32), jax.ShapeDtypeStruct((n, d_ff), F32)],
        compiler_params=_params(1),
        name="ffn_short",
    )(h2d, p1, p2, nw, wup, wc, bc, wdn, nf)
    return y, u.reshape(n_seq, seq_len, d_ff)[:, seq_len - (CONV_W - 1):]


def _prep_weights(norm_mix, w_in, w_gla_a2, b_gla_a, norm_gla, norm_mem, w_mem_kv, w_br_moba, w_br_gla,
                  w_br_cross, w_gate, b_gate, w_out, norm_ffn, w_up, w_conv, b_conv, w_down, norm_final):
    d = w_in.shape[0]
    o_a = C_QC
    w_in_p = jnp.concatenate(
        [w_in[:, :o_a].astype(BF16), w_in[:, o_a + GLA_RANK:].astype(BF16),
         w_in[:, o_a:o_a + GLA_RANK].astype(BF16), jnp.zeros((d, A_PAD - GLA_RANK), BF16)], axis=1)
    wa2_p = jnp.concatenate([w_gla_a2, jnp.zeros((A_PAD - GLA_RANK, GLA_KW), w_gla_a2.dtype)],
                            axis=0).astype(BF16)
    r2 = lambda a: a.reshape(1, -1)
    return dict(
        norm_mix=r2(norm_mix), w_in=w_in_p, wa2=wa2_p, ba=r2(b_gla_a), norm_gla=r2(norm_gla),
        norm_mem=r2(norm_mem), w_mem_kv=w_mem_kv.astype(BF16), wbm=w_br_moba.astype(BF16),
        wbg=w_br_gla.astype(BF16), wbc=w_br_cross.astype(BF16), wg=w_gate.astype(BF16), bg=r2(b_gate),
        wo=w_out.astype(BF16), norm_ffn=r2(norm_ffn), wup=w_up.astype(BF16), wc=w_conv, bc=r2(b_conv),
        wdn=w_down.astype(BF16), norm_final=r2(norm_final))


def kernel(x_prompt, x_sample, cache_moba_k, cache_moba_v, state_gla, state_conv, cache_mem_k, cache_mem_v, page_table, mem_prompt, norm_mix, w_in, w_gla_a2, b_gla_a, norm_gla, norm_mem, w_mem_kv, w_br_moba, w_br_gla, w_br_cross, w_gate, b_gate, w_out, norm_ffn, w_up, w_conv, b_conv, w_down, norm_final):
    depth = w_in.shape[0]
    assert depth == 1
    bp, sp, d = x_prompt.shape
    bs, ss, _ = x_sample.shape
    n_mem = mem_prompt.shape[1]
    w = _prep_weights(norm_mix[0], w_in[0], w_gla_a2[0], b_gla_a[0], norm_gla[0], norm_mem[0], w_mem_kv[0],
                      w_br_moba[0], w_br_gla[0], w_br_cross[0], w_gate[0], b_gate[0], w_out[0], norm_ffn[0],
                      w_up[0], w_conv[0], b_conv[0], w_down[0], norm_final)

    def mix(x2d, o_m, o_g, o_c):
        return _merge(x2d, o_m, o_g, o_c, w["norm_mix"], w["wg"], w["bg"], w["wbm"], w["wbg"], w["wbc"], w["wo"])

    ffn_w = (w["norm_ffn"], w["wup"], w["wc"], w["bc"], w["wdn"], w["norm_final"])

    xp = x_prompt.reshape(bp * sp, d)
    qm, kt_p, vt_p, qg, kg, vg, rg, qc, la, ksum, kb, vb = _project(
        xp, w["norm_mix"], w["w_in"], w["wa2"], w["ba"], BF16, MOBA_Q_SCALE, kv_seq_len=sp)
    o_m = _moba_prompt(qm, kb, vb, ksum.reshape(-1, MOBA_W), bp, sp)
    mk_p, mv_p = _memory_kv(mem_prompt.reshape(bp * n_mem, d), w["norm_mem"], w["w_mem_kv"])
    o_g, gla_p = _gla(qg, kg, vg, la, rg, jnp.zeros((bp, DV_GLA, GLA_KW), F32), w["norm_gla"], bp, sp, BF16)
    o_c = _cross(qc, mk_p, mv_p, bp, sp, n_mem, BF16)
    h_p = mix(xp, o_m, o_g, o_c)
    y_p, conv_p = _ffn_long(h_p, bp, sp, *ffn_w)

    xs = x_sample.reshape(bs * ss, d)
    qm, k_s, v_s, qg, kg, vg, rg, qc, la, _, _, _ = _project(
        xs, w["norm_mix"], w["w_in"], w["wa2"], w["ba"], F32, 1.0)
    r3 = lambda a: a.reshape(bs, ss, MOBA_W)
    o_m = _moba_sample(r3(qm), r3(k_s), r3(v_s), cache_moba_k[0], cache_moba_v[0],
                       page_table).reshape(bs * ss, MOBA_W)
    o_g, gla_s = _gla(qg, kg, vg, la, rg, _state_to_t(state_gla[0]), w["norm_gla"], bs, ss, F32)
    o_c = _cross(qc, cache_mem_k[0], cache_mem_v[0], bs, ss, n_mem, F32)
    h_s = mix(xs, o_m, o_g, o_c)
    y_s, conv_s = _ffn_short(h_s, state_conv[0], bs, ss, *ffn_w)

    kv5 = lambda a, b, s: a.reshape(1, b, s, H_MOBA, HD_MOBA)
    kv5_t = lambda a: jnp.transpose(a.reshape(bp, H_MOBA, HD_MOBA, sp), (0, 3, 1, 2))[None]
    return (y_p.reshape(bp, sp, d), y_s.reshape(bs, ss, d),
            kv5_t(kt_p), kv5_t(vt_p), kv5(k_s, bs, ss), kv5(v_s, bs, ss),
            _state_from_t(gla_p)[None], _state_from_t(gla_s)[None],
            conv_p[None], conv_s[None],
            mk_p.reshape(1, bp, n_mem, H_CROSS, HD_CROSS), mv_p.reshape(1, bp, n_mem, H_CROSS, HD_CROSS))
```

```python
import functools
import math

import jax
import jax.numpy as jnp
from jax import lax
from jax.experimental import pallas as pl
from jax.experimental.pallas import tpu as pltpu

F32 = jnp.float32
BF16 = jnp.bfloat16
NEG_INF = float("-inf")

EPS = 1e-6
H_MOBA, HD_MOBA = 8, 64
MOBA_W = H_MOBA * HD_MOBA
MOBA_BLOCK = 256
MOBA_TOPK = 3
PAGE_SIZE = 128
H_GLA, DK_GLA, DV_GLA = 4, 64, 128
GLA_KW, GLA_VW = H_GLA * DK_GLA, H_GLA * DV_GLA
GLA_RANK = 16
GLA_TAU = 16.0
GLA_CHUNK = 64
GLA_SUB = 16
H_CROSS, HD_CROSS = 4, 128
CROSS_W = H_CROSS * HD_CROSS
N_BRANCH = 3
CONV_W = 3

LANES = 128
SUBLANES = 8
VMEM_LIMIT_BYTES = 56 * 1024 * 1024

MASK_BIAS = -1e30
MOBA_GROUP = 8
MOBA_TQ = 2 * MOBA_BLOCK
MOBA_Q_SCALE = HD_MOBA ** -0.5 * math.log2(math.e)
ROW_TILE = 512
SEQ_TILE = 256
GLA_STEP_CHUNKS = 8
CROSS_STEP_ROWS = 512
CROSS_MAX_GROUP = 8
PAGES_PER_STEP = 16
A_PAD = LANES

C_QM, C_K, C_V = 0, MOBA_W, 2 * MOBA_W
C_QG = 3 * MOBA_W
C_KG = C_QG + GLA_KW
C_VG = C_KG + GLA_KW
C_RG = C_VG + GLA_VW
C_QC = C_RG + GLA_VW
C_A = C_QC + CROSS_W
IN_COLS_PAD = C_A + A_PAD


def _params(n_axes):
    return pltpu.CompilerParams(
        dimension_semantics=("arbitrary",) * n_axes,
        vmem_limit_bytes=VMEM_LIMIT_BYTES,
    )


def _const_spec(shape):
    nd = len(shape)
    return pl.BlockSpec(shape, lambda *_: (0,) * nd, pipeline_mode=pl.Buffered(1))


def _rms(x, w):
    return x * lax.rsqrt(jnp.mean(x * x, axis=-1, keepdims=True) + EPS) * w


def _sigmoid(x):
    return 1.0 / (1.0 + jnp.exp(-x))


def _dot(a, b):
    return jnp.dot(a, b, preferred_element_type=F32)


def _dot_nt(a, b):
    return lax.dot_general(a, b, (((1,), (1,)), ((), ())), preferred_element_type=F32)


def _dot_tn(a, b):
    return lax.dot_general(a, b, (((0,), (0,)), ((), ())), preferred_element_type=F32)


def _proj_kernel(x_ref, nw_ref, w_ref, wa2_ref, ba_ref,
                 qm_ref, k_ref, v_ref, qg_ref, kg_ref, vg_ref, rg_ref, qc_ref, la_ref, ksum_ref,
                 kb_ref, vb_ref, *, qm_scale, kv_transposed):
    xb = _rms(x_ref[...], nw_ref[...]).astype(BF16)

    def mm(lo, hi):
        return _dot(xb, w_ref[:, lo:hi])

    def store_kv(ref, val):
        if kv_transposed:
            ref[0] = val.T
        else:
            ref[...] = val

    qm_ref[...] = (mm(C_QM, C_K) * qm_scale).astype(qm_ref.dtype)
    k = mm(C_K, C_V)
    store_kv(k_ref, k)
    kb_ref[...] = k.astype(BF16)
    for g in range(k.shape[0] // MOBA_BLOCK):
        ksum_ref[g] = jnp.sum(k[g * MOBA_BLOCK:(g + 1) * MOBA_BLOCK], axis=0, keepdims=True)
    v = mm(C_V, C_QG)
    store_kv(v_ref, v)
    vb_ref[...] = v.astype(BF16)
    qg_ref[...] = mm(C_QG, C_KG).astype(qg_ref.dtype)
    kg_ref[...] = mm(C_KG, C_VG).astype(kg_ref.dtype)
    vg_ref[...] = mm(C_VG, C_RG).astype(vg_ref.dtype)
    rg_ref[...] = mm(C_RG, C_QC)
    qc_ref[...] = mm(C_QC, C_A).astype(qc_ref.dtype)
    z = _dot(mm(C_A, IN_COLS_PAD).astype(BF16), wa2_ref[...]) + ba_ref[...]
    la_ref[...] = (jnp.minimum(z, 0.0) - jnp.log1p(jnp.exp(-jnp.abs(z)))) * (1.0 / GLA_TAU)


def _project(x2d, nw, w_in_p, wa2_p, ba, act_dtype, qm_scale, kv_seq_len=None):
    n, d = x2d.shape
    tm = min(ROW_TILE, n)
    assert n % tm == 0 and tm % MOBA_BLOCK == 0
    row = lambda w: pl.BlockSpec((tm, w), lambda i: (i, 0))
    widths = (MOBA_W, MOBA_W, MOBA_W, GLA_KW, GLA_KW, GLA_VW, GLA_VW, CROSS_W, GLA_KW)
    dtypes = (act_dtype, F32, F32, act_dtype, act_dtype, act_dtype, F32, act_dtype, F32)
    out_shape = [jax.ShapeDtypeStruct((n, w), dt) for w, dt in zip(widths, dtypes)]
    out_shape.append(jax.ShapeDtypeStruct((n // MOBA_BLOCK, 1, MOBA_W), F32))
    out_shape += [jax.ShapeDtypeStruct((n, MOBA_W), BF16)] * 2
    out_specs = [row(w) for w in widths]
    out_specs.append(pl.BlockSpec((tm // MOBA_BLOCK, 1, MOBA_W), lambda i: (i, 0, 0)))
    out_specs += [row(MOBA_W)] * 2
    if kv_seq_len is not None:
        assert kv_seq_len % tm == 0
        nt = kv_seq_len // tm
        for i in (1, 2):
            out_shape[i] = jax.ShapeDtypeStruct((n // kv_seq_len, MOBA_W, kv_seq_len), F32)
            out_specs[i] = pl.BlockSpec((1, MOBA_W, tm), lambda i: (i // nt, 0, i % nt))
    return pl.pallas_call(
        functools.partial(_proj_kernel, qm_scale=qm_scale, kv_transposed=kv_seq_len is not None),
        grid=(n // tm,),
        in_specs=[row(d), _const_spec((1, d)), _const_spec(w_in_p.shape),
                  _const_spec(wa2_p.shape), _const_spec((1, GLA_KW))],
        out_specs=out_specs,
        out_shape=out_shape,
        compiler_params=_params(1),
        name="proj",
    )(x2d, nw, w_in_p, wa2_p, ba)


def _memkv_kernel(m_ref, nw_ref, w_ref, mk_ref, mv_ref):
    mb = _rms(m_ref[...], nw_ref[...]).astype(BF16)
    mk_ref[...] = _dot(mb, w_ref[:, :CROSS_W])
    mv_ref[...] = _dot(mb, w_ref[:, CROSS_W:])


def _memory_kv(mem2d, nw, w_kv):
    n, d = mem2d.shape
    tm = min(ROW_TILE, n)
    assert n % tm == 0
    return pl.pallas_call(
        _memkv_kernel,
        grid=(n // tm,),
        in_specs=[pl.BlockSpec((tm, d), lambda i: (i, 0)), _const_spec((1, d)), _const_spec(w_kv.shape)],
        out_specs=[pl.BlockSpec((tm, CROSS_W), lambda i: (i, 0))] * 2,
        out_shape=[jax.ShapeDtypeStruct((n, CROSS_W), F32)] * 2,
        compiler_params=_params(1),
        name="memkv",
    )(mem2d, nw, w_kv)


def _select_topk(gate, idx, valid, axis):
    return _select_topk_each([gate], idx, valid, axis)[0]


def _select_topk_each(gates, idx, valid, axis):
    gs = [jnp.where(valid, gate, NEG_INF) for gate in gates]
    picked = [jnp.zeros(gate.shape, jnp.bool_) for gate in gates]
    for _ in range(MOBA_TOPK):
        ms = [jnp.max(g, axis=axis, keepdims=True) for g in gs]
        firsts = [jnp.min(jnp.where(g == m, idx, float(2 ** 24)), axis=axis, keepdims=True) for g, m in zip(gs, ms)]
        picks = [(idx == first) & (m > NEG_INF) & valid for first, m in zip(firsts, ms)]
        picked = [a | b for a, b in zip(picked, picks)]
        gs = [jnp.where(pick, NEG_INF, g) for pick, g in zip(picks, gs)]
    return picked


def _moba_prompt_kernel(q_ref, k_ref, v_ref, ksum_ref, o_ref, ka_sc, va_sc, s_sc):
    t = pl.program_id(2)
    tq = q_ref.shape[0]
    blk = MOBA_BLOCK
    nblk = ksum_ref.shape[0]
    seq_len = k_ref.shape[0]
    group = s_sc.shape[3] // blk
    n_heads = LANES // HD_MOBA
    spare_off = [((half + 1) % n_heads) * HD_MOBA for half in range(n_heads)]

    @pl.when(t == 0)
    def _():
        key_blk = lax.broadcasted_iota(jnp.int32, (seq_len, LANES), 0) // blk
        key_lane = lax.broadcasted_iota(jnp.int32, (seq_len, LANES), 1)
        k = k_ref[...]
        v = v_ref[...]
        for half in range(n_heads):
            in_head = (key_lane >= half * HD_MOBA) & (key_lane < (half + 1) * HD_MOBA)
            onehot = jnp.where(key_lane - spare_off[half] == key_blk, 1.0, 0.0).astype(BF16)
            ka_sc[half] = jnp.where(in_head, k, onehot)
            va_sc[half] = jnp.where(in_head, v, jnp.ones_like(v))

    q = q_ref[...]
    kmean = ksum_ref[...] * (1.0 / blk)
    km_hi = kmean.astype(BF16)
    km_lo = (kmean - km_hi.astype(F32)).astype(BF16)
    lane = lax.broadcasted_iota(jnp.int32, (1, LANES), 1)
    causal = (lax.broadcasted_iota(jnp.int32, (blk, blk), 1) <= lax.broadcasted_iota(jnp.int32, (blk, blk), 0))
    blk_id = lax.broadcasted_iota(jnp.int32, (nblk, tq), 0)
    place_row = lax.broadcasted_iota(jnp.int32, (nblk, LANES), 0)
    place_lane = lax.broadcasted_iota(jnp.int32, (nblk, LANES), 1)
    first_blk = t * (tq // blk)
    own_blk = first_blk + lax.broadcasted_iota(jnp.int32, (nblk, tq), 1) // blk
    own_parts = range(tq // blk)
    q_rows = [slice(j * blk, (j + 1) * blk) for j in own_parts]
    k_rows = [pl.ds(pl.multiple_of(t * tq + j * blk, blk), blk) for j in own_parts]
    n_groups = (first_blk + tq // blk - 1 + group - 1) // group
    heads = range(n_heads)

    in_head = [(lane >= half * HD_MOBA) & (lane < (half + 1) * HD_MOBA) for half in heads]
    qh = [jnp.where(in_head[half], q, jnp.zeros_like(q)) for half in heads]
    gate_t = [_dot_nt(km_hi, qh[half]) + _dot_nt(km_lo, qh[half]) for half in heads]
    s_own = [[jnp.where(causal, _dot_nt(qh[half][q_rows[j]], ka_sc[half, k_rows[j], :]), MASK_BIAS)
              for j in own_parts] for half in heads]
    picked_t = _select_topk_each(gate_t, blk_id.astype(F32), blk_id < own_blk, axis=0)
    place = [jnp.where(place_lane - spare_off[half] == place_row, 1.0, 0.0).astype(BF16) for half in heads]
    picked = [_dot_tn(jnp.where(picked_t[half], 1.0, 0.0).astype(BF16), place[half]) for half in heads]
    q_aug = []
    for half in heads:
        in_range = (lane >= spare_off[half]) & (lane < spare_off[half] + nblk)
        bias = jnp.where(in_range & (picked[half] < 0.5), MASK_BIAS, 0.0)
        q_aug.append(qh[half] + bias.astype(BF16))

    def group_rows(g):
        return pl.ds(pl.multiple_of(g * (group * blk), group * blk), group * blk)

    def score_pass(g, mxs):
        s = [_dot_nt(q_aug[half], ka_sc[half, group_rows(g), :]) for half in heads]
        out = []
        for half in heads:
            s_sc[half, g] = s[half]
            mx = mxs[half]
            for i in range(group):
                mx = jnp.maximum(mx, s[half][:, i * blk:(i + 1) * blk])
            out.append(mx)
        return tuple(out)

    mx0 = tuple(jnp.concatenate(s_own[half], axis=0) for half in heads)
    mxs = lax.fori_loop(0, n_groups, score_pass, mx0)
    m = [jnp.max(mx, axis=-1, keepdims=True) for mx in mxs]

    def value_pass(g, accs):
        p = [jnp.exp2(s_sc[half, g] - m[half]).astype(BF16) for half in heads]
        return tuple(accs[half] + _dot(p[half], va_sc[half, group_rows(g), :]) for half in heads)

    acc0 = tuple(
        jnp.concatenate([_dot(jnp.exp2(s_own[half][j] - m[half][q_rows[j]]).astype(BF16), va_sc[half, k_rows[j], :])
                         for j in own_parts], axis=0)
        for half in heads)
    accs = lax.fori_loop(0, n_groups, value_pass, acc0)

    o = jnp.zeros((tq, LANES), F32)
    for half in heads:
        row_sum = accs[half][:, spare_off[half]:spare_off[half] + 1]
        o = jnp.where(in_head[half], accs[half] / row_sum, o)
    o_ref[...] = o.astype(o_ref.dtype)


def _moba_prompt(qm, kb, vb, ksum, n_seq, seq_len):
    assert seq_len % MOBA_BLOCK == 0
    nblk = seq_len // MOBA_BLOCK
    assert nblk <= HD_MOBA
    group = MOBA_GROUP if nblk % MOBA_GROUP == 0 else 1
    tq = MOBA_TQ if seq_len % MOBA_TQ == 0 else MOBA_BLOCK
    nt = seq_len // tq
    n = n_seq * seq_len
    n_pair = MOBA_W // LANES
    n_heads = LANES // HD_MOBA
    seq_block = pl.BlockSpec((seq_len, LANES), lambda b, hp, t: (b, hp), pipeline_mode=pl.Buffered(1))
    return pl.pallas_call(
        _moba_prompt_kernel,
        grid=(n_seq, n_pair, nt),
        in_specs=[
            pl.BlockSpec((tq, LANES), lambda b, hp, t: (b * nt + t, hp)),
            seq_block, seq_block,
            pl.BlockSpec((nblk, LANES), lambda b, hp, t: (b, hp)),
        ],
        out_specs=pl.BlockSpec((tq, LANES), lambda b, hp, t: (b * nt + t, hp)),
        out_shape=jax.ShapeDtypeStruct((n, MOBA_W), BF16),
        scratch_shapes=[pltpu.VMEM((n_heads, seq_len, LANES), BF16),
                        pltpu.VMEM((n_heads, seq_len, LANES), BF16),
                        pltpu.VMEM((n_heads, nblk // group, tq, group * MOBA_BLOCK), F32)],
        compiler_params=_params(3),
        name="moba_prompt",
    )(qm, kb, vb, ksum)


def _moba_sample_kernel(pt_ref, qbd_ref, kn_ref, vnt_ref, *rest, n_q):
    del pt_ref
    pps = PAGES_PER_STEP
    k_pages, v_pages = rest[:pps], rest[pps:2 * pps]
    o_ref, m_sc, l_sc, gate_sc, acc_sc = rest[2 * pps:]
    s_id = pl.program_id(1)
    n_step = pl.num_programs(1)
    nblk = m_sc.shape[1]
    n_hq = qbd_ref.shape[1]
    qbd = qbd_ref[0]
    col_head = lax.broadcasted_iota(jnp.int32, (HD_MOBA, n_hq), 1) // n_q
    blk_col = lax.broadcasted_iota(jnp.int32, (n_hq, nblk), 1)
    pages_per_blk = MOBA_BLOCK // PAGE_SIZE

    def own_head(pvt):
        out = pvt[0:HD_MOBA]
        for h in range(1, H_MOBA):
            out = jnp.where(col_head == h, pvt[h * HD_MOBA:(h + 1) * HD_MOBA], out)
        return out

    def block_t(page_refs, i):
        pages = range(i * pages_per_blk, (i + 1) * pages_per_blk)
        return jnp.concatenate([page_refs[j][0] for j in pages], axis=1).astype(BF16)

    blocks = range(pps // pages_per_blk)
    b_idx = [s_id * (pps // pages_per_blk) + i for i in blocks]
    s = [_dot(qbd, block_t(k_pages, i)) for i in blocks]
    gate = [jnp.sum(si, axis=-1, keepdims=True) for si in s]
    m = [jnp.max(si, axis=-1, keepdims=True) for si in s]
    p = [jnp.exp(si - mi) for si, mi in zip(s, m)]
    l = [jnp.sum(pi, axis=-1, keepdims=True) for pi in p]
    pv = [own_head(_dot_nt(block_t(v_pages, i), p[i].astype(BF16))) for i in blocks]
    for i in blocks:
        acc_sc[b_idx[i]] = pv[i]
    for ref, vals in ((m_sc, m), (l_sc, l), (gate_sc, gate)):
        cur = ref[...]
        for i in blocks:
            cur = jnp.where(blk_col == b_idx[i], vals[i], cur)
        ref[...] = cur

    @pl.when(s_id == n_step - 1)
    def _():
        picked = _select_topk(gate_sc[...], blk_col.astype(F32), blk_col >= 0, axis=1)
        m_all = jnp.where(picked, m_sc[...], NEG_INF)
        kn = kn_ref[0].astype(BF16)
        vnt = vnt_ref[0].astype(BF16)
        n_new = kn.shape[0]
        key_i = lax.broadcasted_iota(jnp.int32, (n_hq, n_new), 1)
        q_i = lax.broadcasted_iota(jnp.int32, (n_hq, n_new), 0) % n_q
        s_own = jnp.where(key_i <= q_i, _dot_nt(qbd, kn), NEG_INF)
        m_tot = jnp.maximum(jnp.max(s_own, axis=-1, keepdims=True),
                            jnp.max(m_all, axis=-1, keepdims=True))
        p_own = jnp.exp(s_own - m_tot)
        c = jnp.exp(m_all - m_tot)
        l_tot = jnp.sum(p_own, axis=-1, keepdims=True) + jnp.sum(c * l_sc[...], axis=-1, keepdims=True)
        inv_l = 1.0 / l_tot
        c_t = jnp.transpose(c * inv_l)
        out = own_head(_dot_nt(vnt, (p_own * inv_l).astype(BF16)))
        for b in range(nblk):
            out = out + acc_sc[b] * c_t[b:b + 1, :]
        o_ref[0] = out


def _moba_sample(q, k_new, v_new, cache_k, cache_v, page_table):
    n_seq, n_q, _ = q.shape
    n_phys = cache_k.shape[0]
    n_pages = page_table.shape[1]
    past = n_pages * PAGE_SIZE
    assert past % MOBA_BLOCK == 0 and n_pages % PAGES_PER_STEP == 0
    assert n_q <= MOBA_BLOCK and n_q % SUBLANES == 0
    nblk = past // MOBA_BLOCK
    n_hq = H_MOBA * n_q
    lane_head = jnp.arange(MOBA_W) // HD_MOBA
    qbd = jnp.where(lane_head[None, None, None, :] == jnp.arange(H_MOBA)[None, :, None, None],
                    q[:, None, :, :] * (HD_MOBA ** -0.5), 0.0).reshape(n_seq, n_hq, MOBA_W).astype(BF16)
    n_new = max(2 * SUBLANES, n_q)
    pad = ((0, 0), (0, n_new - n_q), (0, 0))
    kn = jnp.pad(k_new, pad)
    vnt = jnp.transpose(jnp.pad(v_new, pad), (0, 2, 1))
    to_t = lambda c: jnp.transpose(c, (0, 2, 3, 1)).reshape(n_phys, MOBA_W, PAGE_SIZE)

    def page_spec(i):
        return pl.BlockSpec((1, MOBA_W, PAGE_SIZE), lambda b, s, pt: (pt[b, s * PAGES_PER_STEP + i], 0, 0))

    seq3 = lambda r, c: pl.BlockSpec((1, r, c), lambda b, s, pt: (b, 0, 0))
    grid_spec = pltpu.PrefetchScalarGridSpec(
        num_scalar_prefetch=1,
        grid=(n_seq, n_pages // PAGES_PER_STEP),
        in_specs=[seq3(n_hq, MOBA_W), seq3(n_new, MOBA_W), seq3(MOBA_W, n_new)]
        + [page_spec(i) for i in range(PAGES_PER_STEP)] * 2,
        out_specs=seq3(HD_MOBA, n_hq),
        scratch_shapes=[
            pltpu.VMEM((n_hq, nblk), F32),
            pltpu.VMEM((n_hq, nblk), F32),
            pltpu.VMEM((n_hq, nblk), F32),
            pltpu.VMEM((nblk, HD_MOBA, n_hq), F32),
        ],
    )
    out = pl.pallas_call(
        functools.partial(_moba_sample_kernel, n_q=n_q),
        grid_spec=grid_spec,
        out_shape=jax.ShapeDtypeStruct((n_seq, HD_MOBA, n_hq), F32),
        compiler_params=_params(2),
        name="moba_sample",
    )(page_table, qbd, kn, vnt, *([to_t(cache_k)] * PAGES_PER_STEP), *([to_t(cache_v)] * PAGES_PER_STEP))
    return jnp.transpose(out.reshape(n_seq, HD_MOBA, H_MOBA, n_q), (0, 3, 2, 1)).reshape(n_seq, n_q, MOBA_W)


def _cumsum_rows(x):
    n = x.shape[0]
    row = lax.broadcasted_iota(jnp.int32, x.shape, 0)
    s = 1
    while s < n:
        x = x + jnp.where(row >= s, pltpu.roll(x, s, axis=0), 0.0)
        s *= 2
    return x


def _gla_chunk(q, k, v, la, st, cast):
    c = q.shape[0]
    sub = min(GLA_SUB, c)
    lane = lax.broadcasted_iota(jnp.int32, (1, GLA_KW), 1)
    head_masks = [(lane >= h * DK_GLA) & (lane < (h + 1) * DK_GLA) for h in range(H_GLA)]
    g = _cumsum_rows(la)
    krow = lax.broadcasted_iota(jnp.int32, (c, GLA_KW), 0)
    a_row = lax.broadcasted_iota(jnp.int32, (H_GLA * sub, c), 0)
    a_col = lax.broadcasted_iota(jnp.int32, (H_GLA * sub, c), 1)

    subs = range(c // sub)
    g_ref = [jnp.zeros((1, GLA_KW), F32) if i == 0 else g[i * sub - 1:i * sub] for i in subs]
    qt = [q[i * sub:(i + 1) * sub] * jnp.exp(g[i * sub:(i + 1) * sub] - g_ref[i]) for i in subs]
    kt = [jnp.where(krow < (i + 1) * sub, k * jnp.exp(g_ref[i] - g), 0.0) for i in subs]
    q_stack = [jnp.concatenate([jnp.where(hm, qt[i], 0.0) for hm in head_masks], axis=0) for i in subs]
    a = [_dot_nt(cast(q_stack[i]), cast(kt[i])) for i in subs]
    a_parts = [jnp.where(a_col <= i * sub + a_row % sub, a[i], 0.0) for i in subs]

    heads = range(H_GLA)
    qe = q * jnp.exp(g)
    g_last = g[c - 1:c]
    k_dec = cast(k * jnp.exp(g_last - g))
    st_c = cast(st)
    a_h = [jnp.concatenate([ap[h * sub:(h + 1) * sub] for ap in a_parts], axis=0) for h in heads]
    v_h = [cast(v[:, h * DV_GLA:(h + 1) * DV_GLA]) for h in heads]
    o_inter = [_dot_nt(cast(jnp.where(head_masks[h], qe, 0.0)), st_c) for h in heads]
    o_intra = [_dot(cast(a_h[h]), v_h[h]) for h in heads]
    upd = [_dot_tn(v_h[h], k_dec) for h in heads]
    st_new = st * jnp.exp(g_last)
    for h in heads:
        st_new = st_new + jnp.where(head_masks[h], upd[h], 0.0)
    return [o_intra[h] + o_inter[h] for h in heads], st_new


def _gla_kernel(q_ref, k_ref, v_ref, la_ref, r_ref, s0_ref, nw_ref, o_ref, sfin_ref, st_sc, *, chunk):
    t = pl.program_id(1)

    @pl.when(t == 0)
    def _():
        st_sc[...] = s0_ref[...]

    cast = (lambda a: a.astype(BF16)) if chunk >= 2 * SUBLANES else (lambda a: a)
    n_group = q_ref.shape[0]
    sts = [st_sc[g] for g in range(n_group)]
    for c in range(q_ref.shape[1] // chunk):
        rows = slice(c * chunk, (c + 1) * chunk)
        for g in range(n_group):
            q = q_ref[g, rows, :].astype(F32) * (DK_GLA ** -0.5)
            k = k_ref[g, rows, :].astype(F32)
            outs, sts[g] = _gla_chunk(q, k, v_ref[g, rows, :], la_ref[g, rows, :], sts[g], cast)
            for h, o in enumerate(outs):
                lanes = slice(h * DV_GLA, (h + 1) * DV_GLA)
                r = r_ref[g, rows, lanes]
                o = o * lax.rsqrt(jnp.mean(o * o, axis=-1, keepdims=True) + EPS)
                o_ref[g, rows, lanes] = (o * nw_ref[:, lanes] * (r * _sigmoid(r))).astype(o_ref.dtype)
    for g in range(n_group):
        st_sc[g] = sts[g]

    @pl.when(t == pl.num_programs(1) - 1)
    def _():
        for g in range(n_group):
            sfin_ref[g] = sts[g]


def _gla(qg, kg, vg, la, rg, state_t, nw, n_seq, seq_len, out_dtype):
    chunk = math.gcd(seq_len, GLA_CHUNK)
    tl = min(seq_len, SEQ_TILE)
    assert seq_len % tl == 0 and tl % chunk == 0
    nt = seq_len // tl
    n = n_seq * seq_len
    n_group = math.gcd(n_seq, max(1, GLA_STEP_CHUNKS * chunk // tl))
    seq3 = lambda a: a.reshape(n_seq, seq_len, a.shape[-1])
    row = lambda w: pl.BlockSpec((n_group, tl, w), lambda b, t: (b, t, 0))
    st_spec = pl.BlockSpec((n_group, DV_GLA, GLA_KW), lambda b, t: (b, 0, 0))
    o, st = pl.pallas_call(
        functools.partial(_gla_kernel, chunk=chunk),
        grid=(n_seq // n_group, nt),
        in_specs=[row(GLA_KW), row(GLA_KW), row(GLA_VW), row(GLA_KW), row(GLA_VW), st_spec,
                  _const_spec((1, GLA_VW))],
        out_specs=[row(GLA_VW), st_spec],
        out_shape=[jax.ShapeDtypeStruct((n_seq, seq_len, GLA_VW), out_dtype),
                   jax.ShapeDtypeStruct((n_seq, DV_GLA, GLA_KW), F32)],
        scratch_shapes=[pltpu.VMEM((n_group, DV_GLA, GLA_KW), F32)],
        compiler_params=_params(2),
        name="gla",
    )(seq3(qg), seq3(kg), seq3(vg), seq3(la), seq3(rg), state_t, nw)
    return o.reshape(n, GLA_VW), st


def _state_to_t(s):
    n = s.shape[0]
    return jnp.transpose(s, (0, 3, 1, 2)).reshape(n, DV_GLA, GLA_KW)


def _state_from_t(st):
    n = st.shape[0]
    return jnp.transpose(st.reshape(n, DV_GLA, H_GLA, DK_GLA), (0, 2, 3, 1))


def _cross_kernel(q_ref, mk_ref, mv_ref, o_ref, *, n_mem, rows_by_head):
    scale = HD_CROSS ** -0.5
    small = q_ref.shape[1] < 2 * SUBLANES
    cast = (lambda a: a) if small else (lambda a: a.astype(BF16))
    units = [(g, h) for g in range(q_ref.shape[0]) for h in range(H_CROSS)]
    lanes = lambda h: slice(h * HD_CROSS, (h + 1) * HD_CROSS)

    def mem_head(ref, g, h):
        if rows_by_head:
            return ref[g, pl.ds(h, n_mem, stride=H_CROSS), :]
        return ref[g, :, lanes(h)]

    s = [_dot_nt(cast(q_ref[g, :, lanes(h)]), cast(mem_head(mk_ref, g, h))) * scale for g, h in units]
    p = [jnp.exp(si - jnp.max(si, axis=-1, keepdims=True)) for si in s]
    l = [jnp.sum(pi, axis=-1, keepdims=True) for pi in p]
    for (g, h), pi, li in zip(units, p, l):
        o_ref[g, :, lanes(h)] = (_dot(cast(pi), cast(mem_head(mv_ref, g, h))) / li).astype(o_ref.dtype)


def _cross(qc, mk, mv, n_seq, seq_len, n_mem, out_dtype):
    tl = min(seq_len, CROSS_STEP_ROWS)
    assert seq_len % tl == 0
    nt = seq_len // tl
    n_group = math.gcd(n_seq, min(CROSS_MAX_GROUP, max(1, CROSS_STEP_ROWS // tl)))
    row = pl.BlockSpec((n_group, tl, CROSS_W), lambda b, t: (b, t, 0))
    rows_by_head = mk.ndim == 4
    mem_shape = (n_mem * H_CROSS, HD_CROSS) if rows_by_head else (n_mem, CROSS_W)
    mem = pl.BlockSpec((n_group,) + mem_shape, lambda b, t: (b, 0, 0))
    o = pl.pallas_call(
        functools.partial(_cross_kernel, n_mem=n_mem, rows_by_head=rows_by_head),
        grid=(n_seq // n_group, nt),
        in_specs=[row, mem, mem],
        out_specs=row,
        out_shape=jax.ShapeDtypeStruct((n_seq, seq_len, CROSS_W), out_dtype),
        compiler_params=_params(2),
        name="cross",
    )(qc.reshape(n_seq, seq_len, CROSS_W), mk.reshape((n_seq,) + mem_shape), mv.reshape((n_seq,) + mem_shape))
    return o.reshape(n_seq * seq_len, CROSS_W)


def _merge_kernel(x_ref, om_ref, og_ref, oc_ref, nw_ref, wg_ref, bg_ref, wbm_ref, wbg_ref, wbc_ref,
                  wo_ref, h_ref):
    x = x_ref[...]
    d = x.shape[1]
    xb = _rms(x, nw_ref[...]).astype(BF16)
    merged = jnp.zeros(x.shape, F32)
    for i, (o_ref, wb_ref) in enumerate(((om_ref, wbm_ref), (og_ref, wbg_ref), (oc_ref, wbc_ref))):
        cols = slice(i * d, (i + 1) * d)
        gate = _sigmoid(_dot(xb, wg_ref[:, cols]) + bg_ref[:, cols])
        merged = merged + gate * _dot(o_ref[...].astype(BF16), wb_ref[...])
    h_ref[...] = x + _dot(merged.astype(BF16), wo_ref[...])


def _merge(x2d, om, og, oc, nw, wg, bg, wbm, wbg, wbc, wo):
    n, d = x2d.shape
    tm = min(ROW_TILE, n)
    row = lambda w: pl.BlockSpec((tm, w), lambda i: (i, 0))
    return pl.pallas_call(
        _merge_kernel,
        grid=(n // tm,),
        in_specs=[row(d), row(MOBA_W), row(GLA_VW), row(CROSS_W), _const_spec((1, d)),
                  _const_spec(wg.shape), _const_spec(bg.shape), _const_spec(wbm.shape),
                  _const_spec(wbg.shape), _const_spec(wbc.shape), _const_spec(wo.shape)],
        out_specs=row(d),
        out_shape=jax.ShapeDtypeStruct((n, d), F32),
        compiler_params=_params(1),
        name="merge",
    )(x2d, om, og, oc, nw, wg, bg, wbm, wbg, wbc, wo)


def _gelu_tanh(x):
    return 0.5 * x * (1.0 + jnp.tanh(0.7978845608028654 * (x + 0.044715 * (x * x * x))))


def _ffn_body(h_ref, nw_ref, wup_ref, wc_ref, bc_ref, wdn_ref, nf_ref, y_ref, prev_rows, store_u):
    h = h_ref[...]
    hb = _rms(h, nw_ref[...]).astype(BF16)
    d_ff = wdn_ref.shape[0]
    n_col_chunks = 2
    fc = d_ff // n_col_chunks
    assert fc % LANES == 0
    acc = jnp.zeros(h.shape, F32)
    for c in range(n_col_chunks):
        cols = slice(c * fc, (c + 1) * fc)
        u = _dot(hb, wup_ref[:, cols])
        gate = _dot(hb, wup_ref[:, d_ff + c * fc:d_ff + (c + 1) * fc])
        u1, u2 = prev_rows(cols, u)
        store_u(cols, u)
        conv = bc_ref[:, cols] + wc_ref[0:1, cols] * u2 + wc_ref[1:2, cols] * u1 + wc_ref[2:3, cols] * u
        act = (_gelu_tanh(conv) * gate).astype(BF16)
        acc = acc + _dot(act, wdn_ref[cols, :])
    y_ref[...] = _rms(h + acc, nf_ref[...])


def _ffn_long_kernel(h_ref, nw_ref, wup_ref, wc_ref, bc_ref, wdn_ref, nf_ref, y_ref, cs_ref, carry_sc):
    tm = h_ref.shape[0]

    @pl.when(pl.program_id(1) == 0)
    def _():
        carry_sc[...] = jnp.zeros(carry_sc.shape, F32)

    row = lax.broadcasted_iota(jnp.int32, (tm, 1), 0)

    def prev_rows(cols, u):
        last = carry_sc[SUBLANES - 1:SUBLANES, cols]
        last2 = carry_sc[SUBLANES - 2:SUBLANES - 1, cols]
        u1 = jnp.where(row == 0, last, pltpu.roll(u, 1, axis=0))
        u2 = jnp.where(row == 0, last2, jnp.where(row == 1, last, pltpu.roll(u, 2, axis=0)))
        return u1, u2

    def store_u(cols, u):
        carry_sc[:, cols] = u[tm - SUBLANES:tm]
        cs_ref[0, :, cols] = u[tm - (CONV_W - 1):tm]

    _ffn_body(h_ref, nw_ref, wup_ref, wc_ref, bc_ref, wdn_ref, nf_ref, y_ref, prev_rows, store_u)


def _ffn_short_kernel(h_ref, p1_ref, p2_ref, nw_ref, wup_ref, wc_ref, bc_ref, wdn_ref, nf_ref,
                      y_ref, u_ref, *, seq_len):
    tm = h_ref.shape[0]
    pos = lax.broadcasted_iota(jnp.int32, (tm, 1), 0) % seq_len

    def prev_rows(cols, u):
        u1 = jnp.where(pos >= 1, pltpu.roll(u, 1, axis=0), p1_ref[:, cols])
        u2 = jnp.where(pos >= 2, pltpu.roll(u, 2, axis=0), p2_ref[:, cols])
        return u1, u2

    def store_u(cols, u):
        u_ref[:, cols] = u

    _ffn_body(h_ref, nw_ref, wup_ref, wc_ref, bc_ref, wdn_ref, nf_ref, y_ref, prev_rows, store_u)


def _ffn_weights_specs(d, wup, wc, bc, wdn):
    return [_const_spec((1, d)), _const_spec(wup.shape), _const_spec(wc.shape), _const_spec(bc.shape),
            _const_spec(wdn.shape), _const_spec((1, d))]


def _ffn_long(h2d, n_seq, seq_len, nw, wup, wc, bc, wdn, nf):
    n, d = h2d.shape
    d_ff = wdn.shape[0]
    tm = min(ROW_TILE, n)
    assert seq_len % tm == 0
    nt = seq_len // tm
    row = pl.BlockSpec((tm, d), lambda b, t: (b * nt + t, 0))
    return pl.pallas_call(
        _ffn_long_kernel,
        grid=(n_seq, nt),
        in_specs=[row] + _ffn_weights_specs(d, wup, wc, bc, wdn),
        out_specs=[row, pl.BlockSpec((1, CONV_W - 1, d_ff), lambda b, t: (b, 0, 0))],
        out_shape=[jax.ShapeDtypeStruct((n, d), F32),
                   jax.ShapeDtypeStruct((n_seq, CONV_W - 1, d_ff), F32)],
        scratch_shapes=[pltpu.VMEM((SUBLANES, d_ff), F32)],
        compiler_params=_params(2),
        name="ffn_long",
    )(h2d, nw, wup, wc, bc, wdn, nf)


def _ffn_short(h2d, conv_prev, n_seq, seq_len, nw, wup, wc, bc, wdn, nf):
    n, d = h2d.shape
    d_ff = wdn.shape[0]
    tm = min(ROW_TILE, n)
    assert tm % seq_len == 0 and n % tm == 0 and seq_len >= CONV_W - 1
    pos = jnp.arange(seq_len)[None, :, None]
    older, newer = conv_prev[:, 0][:, None, :], conv_prev[:, 1][:, None, :]
    p1 = jnp.where(pos == 0, newer, 0.0).reshape(n, d_ff)
    p2 = jnp.where(pos == 0, older, jnp.where(pos == 1, newer, 0.0)).reshape(n, d_ff)
    row = lambda w: pl.BlockSpec((tm, w), lambda i: (i, 0))
    y, u = pl.pallas_call(
        functools.partial(_ffn_short_kernel, seq_len=seq_len),
        grid=(n // tm,),
        in_specs=[row(d), row(d_ff), row(d_ff)] + _ffn_weights_specs(d, wup, wc, bc, wdn),
        out_specs=[row(d), row(d_ff)],
        out_shape=[jax.ShapeDtypeStruct((n, d), F32), jax.ShapeDtypeStruct((n, d_ff), F32)],
        compiler_params=_params(1),
        name="ffn_short",
    )(h2d, p1, p2, nw, wup, wc, bc, wdn, nf)
    return y, u.reshape(n_seq, seq_len, d_ff)[:, seq_len - (CONV_W - 1):]


def _prep_weights(norm_mix, w_in, w_gla_a2, b_gla_a, norm_gla, norm_mem, w_mem_kv, w_br_moba, w_br_gla,
                  w_br_cross, w_gate, b_gate, w_out, norm_ffn, w_up, w_conv, b_conv, w_down, norm_final):
    d = w_in.shape[0]
    o_a = C_QC
    w_in_p = jnp.concatenate(
        [w_in[:, :o_a].astype(BF16), w_in[:, o_a + GLA_RANK:].astype(BF16),
         w_in[:, o_a:o_a + GLA_RANK].astype(BF16), jnp.zeros((d, A_PAD - GLA_RANK), BF16)], axis=1)
    wa2_p = jnp.concatenate([w_gla_a2, jnp.zeros((A_PAD - GLA_RANK, GLA_KW), w_gla_a2.dtype)],
                            axis=0).astype(BF16)
    r2 = lambda a: a.reshape(1, -1)
    return dict(
        norm_mix=r2(norm_mix), w_in=w_in_p, wa2=wa2_p, ba=r2(b_gla_a), norm_gla=r2(norm_gla),
        norm_mem=r2(norm_mem), w_mem_kv=w_mem_kv.astype(BF16), wbm=w_br_moba.astype(BF16),
        wbg=w_br_gla.astype(BF16), wbc=w_br_cross.astype(BF16), wg=w_gate.astype(BF16), bg=r2(b_gate),
        wo=w_out.astype(BF16), norm_ffn=r2(norm_ffn), wup=w_up.astype(BF16), wc=w_conv, bc=r2(b_conv),
        wdn=w_down.astype(BF16), norm_final=r2(norm_final))


def kernel(x_prompt, x_sample, cache_moba_k, cache_moba_v, state_gla, state_conv, cache_mem_k, cache_mem_v, page_table, mem_prompt, norm_mix, w_in, w_gla_a2, b_gla_a, norm_gla, norm_mem, w_mem_kv, w_br_moba, w_br_gla, w_br_cross, w_gate, b_gate, w_out, norm_ffn, w_up, w_conv, b_conv, w_down, norm_final):
    depth = w_in.shape[0]
    assert depth == 1
    bp, sp, d = x_prompt.shape
    bs, ss, _ = x_sample.shape
    n_mem = mem_prompt.shape[1]
    w = _prep_weights(norm_mix[0], w_in[0], w_gla_a2[0], b_gla_a[0], norm_gla[0], norm_mem[0], w_mem_kv[0],
                      w_br_moba[0], w_br_gla[0], w_br_cross[0], w_gate[0], b_gate[0], w_out[0], norm_ffn[0],
                      w_up[0], w_conv[0], b_conv[0], w_down[0], norm_final)

    def mix(x2d, o_m, o_g, o_c):
        return _merge(x2d, o_m, o_g, o_c, w["norm_mix"], w["wg"], w["bg"], w["wbm"], w["wbg"], w["wbc"], w["wo"])

    ffn_w = (w["norm_ffn"], w["wup"], w["wc"], w["bc"], w["wdn"], w["norm_final"])

    xp = x_prompt.reshape(bp * sp, d)
    qm, kt_p, vt_p, qg, kg, vg, rg, qc, la, ksum, kb, vb = _project(
        xp, w["norm_mix"], w["w_in"], w["wa2"], w["ba"], BF16, MOBA_Q_SCALE, kv_seq_len=sp)
    o_m = _moba_prompt(qm, kb, vb, ksum.reshape(-1, MOBA_W), bp, sp)
    mk_p, mv_p = _memory_kv(mem_prompt.reshape(bp * n_mem, d), w["norm_mem"], w["w_mem_kv"])
    o_g, gla_p = _gla(qg, kg, vg, la, rg, jnp.zeros((bp, DV_GLA, GLA_KW), F32), w["norm_gla"], bp, sp, BF16)
    o_c = _cross(qc, mk_p, mv_p, bp, sp, n_mem, BF16)
    h_p = mix(xp, o_m, o_g, o_c)
    y_p, conv_p = _ffn_long(h_p, bp, sp, *ffn_w)

    xs = x_sample.reshape(bs * ss, d)
    qm, k_s, v_s, qg, kg, vg, rg, qc, la, _, _, _ = _project(
        xs, w["norm_mix"], w["w_in"], w["wa2"], w["ba"], F32, 1.0)
    r3 = lambda a: a.reshape(bs, ss, MOBA_W)
    o_m = _moba_sample(r3(qm), r3(k_s), r3(v_s), cache_moba_k[0], cache_moba_v[0],
                       page_table).reshape(bs * ss, MOBA_W)
    o_g, gla_s = _gla(qg, kg, vg, la, rg, _state_to_t(state_gla[0]), w["norm_gla"], bs, ss, F32)
    o_c = _cross(qc, cache_mem_k[0], cache_mem_v[0], bs, ss, n_mem, F32)
    h_s = mix(xs, o_m, o_g, o_c)
    y_s, conv_s = _ffn_short(h_s, state_conv[0], bs, ss, *ffn_w)

    kv5 = lambda a, b, s: a.reshape(1, b, s, H_MOBA, HD_MOBA)
    kv5_t = lambda a: jnp.transpose(a.reshape(bp, H_MOBA, HD_MOBA, sp), (0, 3, 1, 2))[None]
    return (y_p.reshape(bp, sp, d), y_s.reshape(bs, ss, d),
            kv5_t(kt_p), kv5_t(vt_p), kv5(k_s, bs, ss), kv5(v_s, bs, ss),
            _state_from_t(gla_p)[None], _state_from_t(gla_s)[None],
            conv_p[None], conv_s[None],
            mk_p.reshape(1, bp, n_mem, H_CROSS, HD_CROSS), mv_p.reshape(1, bp, n_mem, H_CROSS, HD_CROSS))
```

```python
import functools
import math

import jax
import jax.numpy as jnp
from jax import lax
from jax.experimental import pallas as pl
from jax.experimental.pallas import tpu as pltpu

F32 = jnp.float32
BF16 = jnp.bfloat16
NEG_INF = float("-inf")

EPS = 1e-6
H_MOBA, HD_MOBA = 8, 64
MOBA_W = H_MOBA * HD_MOBA
MOBA_BLOCK = 256
MOBA_TOPK = 3
PAGE_SIZE = 128
H_GLA, DK_GLA, DV_GLA = 4, 64, 128
GLA_KW, GLA_VW = H_GLA * DK_GLA, H_GLA * DV_GLA
GLA_RANK = 16
GLA_TAU = 16.0
GLA_CHUNK = 64
GLA_SUB = 16
H_CROSS, HD_CROSS = 4, 128
CROSS_W = H_CROSS * HD_CROSS
CONV_W = 3

LANES = 128
SUBLANES = 8
BF16_ROWS = 2 * SUBLANES
VMEM_LIMIT_BYTES = 56 * 1024 * 1024

MASK_BIAS = -1e30
NO_INDEX = float(2 ** 24)
MOBA_GROUP = 8
MOBA_TQ = 2 * MOBA_BLOCK
MOBA_Q_SCALE = HD_MOBA ** -0.5 * math.log2(math.e)
ROW_TILE = 512
SEQ_TILE = 256
FFN_COL_CHUNKS = 2
GLA_STEP_CHUNKS = 8
CROSS_STEP_ROWS = 512
CROSS_MAX_GROUP = 8
PAGES_PER_STEP = 32
A_PAD = LANES

C_QM, C_K, C_V = 0, MOBA_W, 2 * MOBA_W
C_QG = 3 * MOBA_W
C_KG = C_QG + GLA_KW
C_VG = C_KG + GLA_KW
C_RG = C_VG + GLA_VW
C_QC = C_RG + GLA_VW
C_A = C_QC + CROSS_W
IN_COLS_PAD = C_A + A_PAD


def _params(n_axes):
    return pltpu.CompilerParams(
        dimension_semantics=("arbitrary",) * n_axes,
        vmem_limit_bytes=VMEM_LIMIT_BYTES,
    )


def _const_spec(shape):
    nd = len(shape)
    return pl.BlockSpec(shape, lambda *_: (0,) * nd, pipeline_mode=pl.Buffered(1))


def _rms(x, w):
    return x * lax.rsqrt(jnp.mean(x * x, axis=-1, keepdims=True) + EPS) * w


def _sigmoid(x):
    return 1.0 / (1.0 + jnp.exp(-x))


def _dot(a, b):
    return jnp.dot(a, b, preferred_element_type=F32)


def _dot_nt(a, b):
    return lax.dot_general(a, b, (((1,), (1,)), ((), ())), preferred_element_type=F32)


def _dot_tn(a, b):
    return lax.dot_general(a, b, (((0,), (0,)), ((), ())), preferred_element_type=F32)


def _proj_kernel(x_ref, nw_ref, w_ref, wa2_ref, ba_ref,
                 qm_ref, k_ref, v_ref, qg_ref, kg_ref, vg_ref, rg_ref, qc_ref, la_ref, ksum_ref,
                 kb_ref, vb_ref, *, qm_scale, kv_transposed):
    xb = _rms(x_ref[...], nw_ref[...]).astype(BF16)

    def mm(lo, hi):
        return _dot(xb, w_ref[:, lo:hi])

    def store_kv(ref, val):
        if kv_transposed:
            ref[0] = val.T
        else:
            ref[...] = val

    qm_ref[...] = (mm(C_QM, C_K) * qm_scale).astype(qm_ref.dtype)
    k = mm(C_K, C_V)
    store_kv(k_ref, k)
    kb_ref[...] = k.astype(BF16)
    for g in range(k.shape[0] // MOBA_BLOCK):
        ksum_ref[g] = jnp.sum(k[g * MOBA_BLOCK:(g + 1) * MOBA_BLOCK], axis=0, keepdims=True)
    v = mm(C_V, C_QG)
    store_kv(v_ref, v)
    vb_ref[...] = v.astype(BF16)
    qg_ref[...] = mm(C_QG, C_KG).astype(qg_ref.dtype)
    kg_ref[...] = mm(C_KG, C_VG).astype(kg_ref.dtype)
    vg_ref[...] = mm(C_VG, C_RG).astype(vg_ref.dtype)
    rg_ref[...] = mm(C_RG, C_QC)
    qc_ref[...] = mm(C_QC, C_A).astype(qc_ref.dtype)
    z = _dot(mm(C_A, IN_COLS_PAD).astype(BF16), wa2_ref[...]) + ba_ref[...]
    la_ref[...] = (jnp.minimum(z, 0.0) - jnp.log1p(jnp.exp(-jnp.abs(z)))) * (1.0 / GLA_TAU)


def _project(x2d, nw, w_in_p, wa2_p, ba, act_dtype, qm_scale, kv_seq_len=None):
    n, d = x2d.shape
    tm = min(ROW_TILE, n)
    assert n % tm == 0 and tm % MOBA_BLOCK == 0
    row = lambda w: pl.BlockSpec((tm, w), lambda i: (i, 0))
    widths = (MOBA_W, MOBA_W, MOBA_W, GLA_KW, GLA_KW, GLA_VW, GLA_VW, CROSS_W, GLA_KW)
    dtypes = (act_dtype, F32, F32, act_dtype, act_dtype, act_dtype, F32, act_dtype, F32)
    out_shape = [jax.ShapeDtypeStruct((n, w), dt) for w, dt in zip(widths, dtypes)]
    out_shape.append(jax.ShapeDtypeStruct((n // MOBA_BLOCK, 1, MOBA_W), F32))
    out_shape += [jax.ShapeDtypeStruct((n, MOBA_W), BF16)] * 2
    out_specs = [row(w) for w in widths]
    out_specs.append(pl.BlockSpec((tm // MOBA_BLOCK, 1, MOBA_W), lambda i: (i, 0, 0)))
    out_specs += [row(MOBA_W)] * 2
    if kv_seq_len is not None:
        assert kv_seq_len % tm == 0
        nt = kv_seq_len // tm
        for i in (1, 2):
            out_shape[i] = jax.ShapeDtypeStruct((n // kv_seq_len, MOBA_W, kv_seq_len), F32)
            out_specs[i] = pl.BlockSpec((1, MOBA_W, tm), lambda i: (i // nt, 0, i % nt))
    return pl.pallas_call(
        functools.partial(_proj_kernel, qm_scale=qm_scale, kv_transposed=kv_seq_len is not None),
        grid=(n // tm,),
        in_specs=[row(d), _const_spec((1, d)), _const_spec(w_in_p.shape),
                  _const_spec(wa2_p.shape), _const_spec((1, GLA_KW))],
        out_specs=out_specs,
        out_shape=out_shape,
        compiler_params=_params(1),
        name="proj",
    )(x2d, nw, w_in_p, wa2_p, ba)


def _memkv_kernel(m_ref, nw_ref, w_ref, mk_ref, mv_ref):
    mb = _rms(m_ref[...], nw_ref[...]).astype(BF16)
    mk_ref[...] = _dot(mb, w_ref[:, :CROSS_W])
    mv_ref[...] = _dot(mb, w_ref[:, CROSS_W:])


def _memory_kv(mem2d, nw, w_kv):
    n, d = mem2d.shape
    tm = min(ROW_TILE, n)
    assert n % tm == 0
    return pl.pallas_call(
        _memkv_kernel,
        grid=(n // tm,),
        in_specs=[pl.BlockSpec((tm, d), lambda i: (i, 0)), _const_spec((1, d)), _const_spec(w_kv.shape)],
        out_specs=[pl.BlockSpec((tm, CROSS_W), lambda i: (i, 0))] * 2,
        out_shape=[jax.ShapeDtypeStruct((n, CROSS_W), F32)] * 2,
        compiler_params=_params(1),
        name="memkv",
    )(mem2d, nw, w_kv)


def _select_topk(gate, idx, valid, axis):
    return _select_topk_each([gate], idx, valid, axis)[0]


def _select_topk_each(gates, idx, valid, axis):
    gs = [jnp.where(valid, gate, NEG_INF) for gate in gates]
    picked = [jnp.zeros(gate.shape, jnp.bool_) for gate in gates]
    for _ in range(MOBA_TOPK):
        ms = [jnp.max(g, axis=axis, keepdims=True) for g in gs]
        firsts = [jnp.min(jnp.where(g == m, idx, NO_INDEX), axis=axis, keepdims=True) for g, m in zip(gs, ms)]
        picks = [(idx == first) & (m > NEG_INF) & valid for first, m in zip(firsts, ms)]
        picked = [a | b for a, b in zip(picked, picks)]
        gs = [jnp.where(pick, NEG_INF, g) for pick, g in zip(picks, gs)]
    return picked


def _moba_prompt_kernel(q_ref, k_ref, v_ref, ksum_ref, o_ref, ka_sc, va_sc, s_sc):
    t = pl.program_id(2)
    tq = q_ref.shape[0]
    blk = MOBA_BLOCK
    nblk = ksum_ref.shape[0]
    seq_len = k_ref.shape[0]
    group = s_sc.shape[3] // blk
    n_heads = LANES // HD_MOBA
    spare_off = [((half + 1) % n_heads) * HD_MOBA for half in range(n_heads)]

    @pl.when(t == 0)
    def _():
        key_blk = lax.broadcasted_iota(jnp.int32, (seq_len, LANES), 0) // blk
        key_lane = lax.broadcasted_iota(jnp.int32, (seq_len, LANES), 1)
        k = k_ref[...]
        v = v_ref[...]
        for half in range(n_heads):
            in_head = (key_lane >= half * HD_MOBA) & (key_lane < (half + 1) * HD_MOBA)
            onehot = jnp.where(key_lane - spare_off[half] == key_blk, 1.0, 0.0).astype(BF16)
            ka_sc[half] = jnp.where(in_head, k, onehot)
            va_sc[half] = jnp.where(in_head, v, jnp.ones_like(v))

    q = q_ref[...]
    kmean = ksum_ref[...] * (1.0 / blk)
    km_hi = kmean.astype(BF16)
    km_lo = (kmean - km_hi.astype(F32)).astype(BF16)
    lane = lax.broadcasted_iota(jnp.int32, (1, LANES), 1)
    causal = (lax.broadcasted_iota(jnp.int32, (blk, blk), 1) <= lax.broadcasted_iota(jnp.int32, (blk, blk), 0))
    blk_id = lax.broadcasted_iota(jnp.int32, (nblk, tq), 0)
    place_row = lax.broadcasted_iota(jnp.int32, (nblk, LANES), 0)
    place_lane = lax.broadcasted_iota(jnp.int32, (nblk, LANES), 1)
    first_blk = t * (tq // blk)
    own_blk = first_blk + lax.broadcasted_iota(jnp.int32, (nblk, tq), 1) // blk
    own_parts = range(tq // blk)
    q_rows = [slice(j * blk, (j + 1) * blk) for j in own_parts]
    k_rows = [pl.ds(pl.multiple_of(t * tq + j * blk, blk), blk) for j in own_parts]
    n_groups = (first_blk + tq // blk - 1 + group - 1) // group
    heads = range(n_heads)

    in_head = [(lane >= half * HD_MOBA) & (lane < (half + 1) * HD_MOBA) for half in heads]
    qh = [jnp.where(in_head[half], q, jnp.zeros_like(q)) for half in heads]
    gate_t = [_dot_nt(km_hi, qh[half]) + _dot_nt(km_lo, qh[half]) for half in heads]
    s_own = [[jnp.where(causal, _dot_nt(qh[half][q_rows[j]], ka_sc[half, k_rows[j], :]), MASK_BIAS)
              for j in own_parts] for half in heads]
    picked_t = _select_topk_each(gate_t, blk_id.astype(F32), blk_id < own_blk, axis=0)
    place = [jnp.where(place_lane - spare_off[half] == place_row, 1.0, 0.0).astype(BF16) for half in heads]
    picked = [_dot_tn(jnp.where(picked_t[half], 1.0, 0.0).astype(BF16), place[half]) for half in heads]
    q_aug = []
    for half in heads:
        in_range = (lane >= spare_off[half]) & (lane < spare_off[half] + nblk)
        bias = jnp.where(in_range & (picked[half] < 0.5), MASK_BIAS, 0.0)
        q_aug.append(qh[half] + bias.astype(BF16))

    def group_rows(g):
        return pl.ds(pl.multiple_of(g * (group * blk), group * blk), group * blk)

    def score_pass(g, mxs):
        s = [_dot_nt(q_aug[half], ka_sc[half, group_rows(g), :]) for half in heads]
        out = []
        for half in heads:
            s_sc[half, g] = s[half]
            mx = mxs[half]
            for i in range(group):
                mx = jnp.maximum(mx, s[half][:, i * blk:(i + 1) * blk])
            out.append(mx)
        return tuple(out)

    mx0 = tuple(jnp.concatenate(s_own[half], axis=0) for half in heads)
    mxs = lax.fori_loop(0, n_groups, score_pass, mx0)
    m = [jnp.max(mx, axis=-1, keepdims=True) for mx in mxs]

    def value_pass(g, accs):
        p = [jnp.exp2(s_sc[half, g] - m[half]).astype(BF16) for half in heads]
        return tuple(accs[half] + _dot(p[half], va_sc[half, group_rows(g), :]) for half in heads)

    acc0 = tuple(
        jnp.concatenate([_dot(jnp.exp2(s_own[half][j] - m[half][q_rows[j]]).astype(BF16), va_sc[half, k_rows[j], :])
                         for j in own_parts], axis=0)
        for half in heads)
    accs = lax.fori_loop(0, n_groups, value_pass, acc0)

    o = jnp.zeros((tq, LANES), F32)
    for half in heads:
        row_sum = accs[half][:, spare_off[half]:spare_off[half] + 1]
        o = jnp.where(in_head[half], accs[half] / row_sum, o)
    o_ref[...] = o.astype(o_ref.dtype)


def _moba_prompt(qm, kb, vb, ksum, n_seq, seq_len):
    assert seq_len % MOBA_BLOCK == 0
    nblk = seq_len // MOBA_BLOCK
    assert nblk <= HD_MOBA
    group = MOBA_GROUP if nblk % MOBA_GROUP == 0 else 1
    tq = MOBA_TQ if seq_len % MOBA_TQ == 0 else MOBA_BLOCK
    nt = seq_len // tq
    n = n_seq * seq_len
    n_pair = MOBA_W // LANES
    n_heads = LANES // HD_MOBA
    seq_block = pl.BlockSpec((seq_len, LANES), lambda b, hp, t: (b, hp), pipeline_mode=pl.Buffered(1))
    return pl.pallas_call(
        _moba_prompt_kernel,
        grid=(n_seq, n_pair, nt),
        in_specs=[
            pl.BlockSpec((tq, LANES), lambda b, hp, t: (b * nt + t, hp)),
            seq_block, seq_block,
            pl.BlockSpec((nblk, LANES), lambda b, hp, t: (b, hp)),
        ],
        out_specs=pl.BlockSpec((tq, LANES), lambda b, hp, t: (b * nt + t, hp)),
        out_shape=jax.ShapeDtypeStruct((n, MOBA_W), BF16),
        scratch_shapes=[pltpu.VMEM((n_heads, seq_len, LANES), BF16),
                        pltpu.VMEM((n_heads, seq_len, LANES), BF16),
                        pltpu.VMEM((n_heads, nblk // group, tq, group * MOBA_BLOCK), F32)],
        compiler_params=_params(3),
        name="moba_prompt",
    )(qm, kb, vb, ksum)


def _moba_sample_kernel(pt_ref, qbd_ref, kn_ref, vnt_ref, *rest, n_q):
    del pt_ref
    pps = PAGES_PER_STEP
    k_pages, v_pages = rest[:pps], rest[pps:2 * pps]
    o_ref, m_sc, l_sc, gate_sc, acc_sc = rest[2 * pps:]
    s_id = pl.program_id(1)
    n_step = pl.num_programs(1)
    nblk = m_sc.shape[1]
    n_hq = qbd_ref.shape[1]
    qbd = qbd_ref[0]
    col_head = lax.broadcasted_iota(jnp.int32, (HD_MOBA, n_hq), 1) // n_q
    blk_col = lax.broadcasted_iota(jnp.int32, (n_hq, nblk), 1)
    pages_per_blk = MOBA_BLOCK // PAGE_SIZE

    def own_head(pvt):
        out = pvt[0:HD_MOBA]
        for h in range(1, H_MOBA):
            out = jnp.where(col_head == h, pvt[h * HD_MOBA:(h + 1) * HD_MOBA], out)
        return out

    def block_t(page_refs, i):
        pages = range(i * pages_per_blk, (i + 1) * pages_per_blk)
        return jnp.concatenate([page_refs[j][0] for j in pages], axis=1).astype(BF16)

    blocks = range(pps // pages_per_blk)
    b_idx = [s_id * (pps // pages_per_blk) + i for i in blocks]
    s = [_dot(qbd, block_t(k_pages, i)) for i in blocks]
    gate = [jnp.sum(si, axis=-1, keepdims=True) for si in s]
    m = [jnp.max(si, axis=-1, keepdims=True) for si in s]
    p = [jnp.exp(si - mi) for si, mi in zip(s, m)]
    l = [jnp.sum(pi, axis=-1, keepdims=True) for pi in p]
    pv = [own_head(_dot_nt(block_t(v_pages, i), p[i].astype(BF16))) for i in blocks]
    for i in blocks:
        acc_sc[b_idx[i]] = pv[i]
    for ref, vals in ((m_sc, m), (l_sc, l), (gate_sc, gate)):
        cur = ref[...]
        for i in blocks:
            cur = jnp.where(blk_col == b_idx[i], vals[i], cur)
        ref[...] = cur

    @pl.when(s_id == n_step - 1)
    def _():
        picked = _select_topk(gate_sc[...], blk_col.astype(F32), blk_col >= 0, axis=1)
        m_all = jnp.where(picked, m_sc[...], NEG_INF)
        kn = kn_ref[0].astype(BF16)
        vnt = vnt_ref[0].astype(BF16)
        n_new = kn.shape[0]
        key_i = lax.broadcasted_iota(jnp.int32, (n_hq, n_new), 1)
        q_i = lax.broadcasted_iota(jnp.int32, (n_hq, n_new), 0) % n_q
        s_own = jnp.where(key_i <= q_i, _dot_nt(qbd, kn), NEG_INF)
        m_tot = jnp.maximum(jnp.max(s_own, axis=-1, keepdims=True),
                            jnp.max(m_all, axis=-1, keepdims=True))
        p_own = jnp.exp(s_own - m_tot)
        c = jnp.exp(m_all - m_tot)
        l_tot = jnp.sum(p_own, axis=-1, keepdims=True) + jnp.sum(c * l_sc[...], axis=-1, keepdims=True)
        inv_l = 1.0 / l_tot
        c_t = jnp.transpose(c * inv_l)
        out = own_head(_dot_nt(vnt, (p_own * inv_l).astype(BF16)))
        for b in range(nblk):
            out = out + acc_sc[b] * c_t[b:b + 1, :]
        o_ref[0] = out


def _moba_sample(q, k_new, v_new, cache_k, cache_v, page_table):
    n_seq, n_q, _ = q.shape
    n_phys = cache_k.shape[0]
    n_pages = page_table.shape[1]
    past = n_pages * PAGE_SIZE
    assert past % MOBA_BLOCK == 0 and n_pages % PAGES_PER_STEP == 0
    assert n_q <= MOBA_BLOCK and n_q % SUBLANES == 0
    nblk = past // MOBA_BLOCK
    n_hq = H_MOBA * n_q
    lane_head = jnp.arange(MOBA_W) // HD_MOBA
    qbd = jnp.where(lane_head[None, None, None, :] == jnp.arange(H_MOBA)[None, :, None, None],
                    q[:, None, :, :] * (HD_MOBA ** -0.5), 0.0).reshape(n_seq, n_hq, MOBA_W).astype(BF16)
    n_new = max(BF16_ROWS, n_q)
    pad = ((0, 0), (0, n_new - n_q), (0, 0))
    kn = jnp.pad(k_new, pad)
    vnt = jnp.transpose(jnp.pad(v_new, pad), (0, 2, 1))
    to_t = lambda c: jnp.transpose(c, (0, 2, 3, 1)).reshape(n_phys, MOBA_W, PAGE_SIZE)

    def page_spec(i):
        return pl.BlockSpec((1, MOBA_W, PAGE_SIZE), lambda b, s, pt: (pt[b, s * PAGES_PER_STEP + i], 0, 0))

    seq3 = lambda r, c: pl.BlockSpec((1, r, c), lambda b, s, pt: (b, 0, 0))
    grid_spec = pltpu.PrefetchScalarGridSpec(
        num_scalar_prefetch=1,
        grid=(n_seq, n_pages // PAGES_PER_STEP),
        in_specs=[seq3(n_hq, MOBA_W), seq3(n_new, MOBA_W), seq3(MOBA_W, n_new)]
        + [page_spec(i) for i in range(PAGES_PER_STEP)] * 2,
        out_specs=seq3(HD_MOBA, n_hq),
        scratch_shapes=[
            pltpu.VMEM((n_hq, nblk), F32),
            pltpu.VMEM((n_hq, nblk), F32),
            pltpu.VMEM((n_hq, nblk), F32),
            pltpu.VMEM((nblk, HD_MOBA, n_hq), F32),
        ],
    )
    out = pl.pallas_call(
        functools.partial(_moba_sample_kernel, n_q=n_q),
        grid_spec=grid_spec,
        out_shape=jax.ShapeDtypeStruct((n_seq, HD_MOBA, n_hq), F32),
        compiler_params=_params(2),
        name="moba_sample",
    )(page_table, qbd, kn, vnt, *([to_t(cache_k)] * PAGES_PER_STEP), *([to_t(cache_v)] * PAGES_PER_STEP))
    return jnp.transpose(out.reshape(n_seq, HD_MOBA, H_MOBA, n_q), (0, 3, 2, 1)).reshape(n_seq, n_q, MOBA_W)


def _cumsum_rows(x):
    n = x.shape[0]
    row = lax.broadcasted_iota(jnp.int32, x.shape, 0)
    s = 1
    while s < n:
        x = x + jnp.where(row >= s, pltpu.roll(x, s, axis=0), 0.0)
        s *= 2
    return x


def _gla_chunk(q, k, v, la, st, cast):
    c = q.shape[0]
    sub = min(GLA_SUB, c)
    lane = lax.broadcasted_iota(jnp.int32, (1, GLA_KW), 1)
    head_masks = [(lane >= h * DK_GLA) & (lane < (h + 1) * DK_GLA) for h in range(H_GLA)]
    g = _cumsum_rows(la)
    krow = lax.broadcasted_iota(jnp.int32, (c, GLA_KW), 0)
    a_row = lax.broadcasted_iota(jnp.int32, (H_GLA * sub, c), 0)
    a_col = lax.broadcasted_iota(jnp.int32, (H_GLA * sub, c), 1)

    subs = range(c // sub)
    g_ref = [jnp.zeros((1, GLA_KW), F32) if i == 0 else g[i * sub - 1:i * sub] for i in subs]
    qt = [q[i * sub:(i + 1) * sub] * jnp.exp(g[i * sub:(i + 1) * sub] - g_ref[i]) for i in subs]
    kt = [jnp.where(krow < (i + 1) * sub, k * jnp.exp(g_ref[i] - g), 0.0) for i in subs]
    q_stack = [jnp.concatenate([jnp.where(hm, qt[i], 0.0) for hm in head_masks], axis=0) for i in subs]
    a = [_dot_nt(cast(q_stack[i]), cast(kt[i])) for i in subs]
    a_parts = [jnp.where(a_col <= i * sub + a_row % sub, a[i], 0.0) for i in subs]

    heads = range(H_GLA)
    qe = q * jnp.exp(g)
    g_last = g[c - 1:c]
    k_dec = cast(k * jnp.exp(g_last - g))
    st_c = cast(st)
    a_h = [jnp.concatenate([ap[h * sub:(h + 1) * sub] for ap in a_parts], axis=0) for h in heads]
    v_h = [cast(v[:, h * DV_GLA:(h + 1) * DV_GLA]) for h in heads]
    o_inter = [_dot_nt(cast(jnp.where(head_masks[h], qe, 0.0)), st_c) for h in heads]
    o_intra = [_dot(cast(a_h[h]), v_h[h]) for h in heads]
    upd = [_dot_tn(v_h[h], k_dec) for h in heads]
    st_new = st * jnp.exp(g_last)
    for h in heads:
        st_new = st_new + jnp.where(head_masks[h], upd[h], 0.0)
    return [o_intra[h] + o_inter[h] for h in heads], st_new


def _gla_kernel(q_ref, k_ref, v_ref, la_ref, r_ref, s0_ref, nw_ref, o_ref, sfin_ref, st_sc, *, chunk):
    t = pl.program_id(1)

    @pl.when(t == 0)
    def _():
        st_sc[...] = s0_ref[...]

    cast = (lambda a: a.astype(BF16)) if chunk >= BF16_ROWS else (lambda a: a)
    n_group = q_ref.shape[0]
    sts = [st_sc[g] for g in range(n_group)]
    for c in range(q_ref.shape[1] // chunk):
        rows = slice(c * chunk, (c + 1) * chunk)
        for g in range(n_group):
            q = q_ref[g, rows, :].astype(F32) * (DK_GLA ** -0.5)
            k = k_ref[g, rows, :].astype(F32)
            outs, sts[g] = _gla_chunk(q, k, v_ref[g, rows, :], la_ref[g, rows, :], sts[g], cast)
            for h, o in enumerate(outs):
                lanes = slice(h * DV_GLA, (h + 1) * DV_GLA)
                r = r_ref[g, rows, lanes]
                o = o * lax.rsqrt(jnp.mean(o * o, axis=-1, keepdims=True) + EPS)
                o_ref[g, rows, lanes] = (o * nw_ref[:, lanes] * (r * _sigmoid(r))).astype(o_ref.dtype)
    for g in range(n_group):
        st_sc[g] = sts[g]

    @pl.when(t == pl.num_programs(1) - 1)
    def _():
        for g in range(n_group):
            sfin_ref[g] = sts[g]


def _gla(qg, kg, vg, la, rg, state_t, nw, n_seq, seq_len, out_dtype):
    chunk = math.gcd(seq_len, GLA_CHUNK)
    tl = min(seq_len, SEQ_TILE)
    assert seq_len % tl == 0 and tl % chunk == 0
    nt = seq_len // tl
    n = n_seq * seq_len
    n_group = math.gcd(n_seq, max(1, GLA_STEP_CHUNKS * chunk // tl))
    seq3 = lambda a: a.reshape(n_seq, seq_len, a.shape[-1])
    row = lambda w: pl.BlockSpec((n_group, tl, w), lambda b, t: (b, t, 0))
    st_spec = pl.BlockSpec((n_group, DV_GLA, GLA_KW), lambda b, t: (b, 0, 0))
    o, st = pl.pallas_call(
        functools.partial(_gla_kernel, chunk=chunk),
        grid=(n_seq // n_group, nt),
        in_specs=[row(GLA_KW), row(GLA_KW), row(GLA_VW), row(GLA_KW), row(GLA_VW), st_spec,
                  _const_spec((1, GLA_VW))],
        out_specs=[row(GLA_VW), st_spec],
        out_shape=[jax.ShapeDtypeStruct((n_seq, seq_len, GLA_VW), out_dtype),
                   jax.ShapeDtypeStruct((n_seq, DV_GLA, GLA_KW), F32)],
        scratch_shapes=[pltpu.VMEM((n_group, DV_GLA, GLA_KW), F32)],
        compiler_params=_params(2),
        name="gla",
    )(seq3(qg), seq3(kg), seq3(vg), seq3(la), seq3(rg), state_t, nw)
    return o.reshape(n, GLA_VW), st


def _state_to_t(s):
    n = s.shape[0]
    return jnp.transpose(s, (0, 3, 1, 2)).reshape(n, DV_GLA, GLA_KW)


def _state_from_t(st):
    n = st.shape[0]
    return jnp.transpose(st.reshape(n, DV_GLA, H_GLA, DK_GLA), (0, 2, 3, 1))


def _cross_kernel(q_ref, mk_ref, mv_ref, o_ref, *, n_mem, rows_by_head):
    scale = HD_CROSS ** -0.5
    small = q_ref.shape[1] < BF16_ROWS
    cast = (lambda a: a) if small else (lambda a: a.astype(BF16))
    units = [(g, h) for g in range(q_ref.shape[0]) for h in range(H_CROSS)]
    lanes = lambda h: slice(h * HD_CROSS, (h + 1) * HD_CROSS)

    def mem_head(ref, g, h):
        if rows_by_head:
            return ref[g, pl.ds(h, n_mem, stride=H_CROSS), :]
        return ref[g, :, lanes(h)]

    s = [_dot_nt(cast(q_ref[g, :, lanes(h)]), cast(mem_head(mk_ref, g, h))) * scale for g, h in units]
    p = [jnp.exp(si - jnp.max(si, axis=-1, keepdims=True)) for si in s]
    l = [jnp.sum(pi, axis=-1, keepdims=True) for pi in p]
    for (g, h), pi, li in zip(units, p, l):
        o_ref[g, :, lanes(h)] = (_dot(cast(pi), cast(mem_head(mv_ref, g, h))) / li).astype(o_ref.dtype)


def _cross(qc, mk, mv, n_seq, seq_len, n_mem, out_dtype):
    tl = min(seq_len, CROSS_STEP_ROWS)
    assert seq_len % tl == 0
    nt = seq_len // tl
    n_group = math.gcd(n_seq, min(CROSS_MAX_GROUP, max(1, CROSS_STEP_ROWS // tl)))
    row = pl.BlockSpec((n_group, tl, CROSS_W), lambda b, t: (b, t, 0))
    rows_by_head = mk.ndim == 4
    mem_shape = (n_mem * H_CROSS, HD_CROSS) if rows_by_head else (n_mem, CROSS_W)
    mem = pl.BlockSpec((n_group,) + mem_shape, lambda b, t: (b, 0, 0))
    o = pl.pallas_call(
        functools.partial(_cross_kernel, n_mem=n_mem, rows_by_head=rows_by_head),
        grid=(n_seq // n_group, nt),
        in_specs=[row, mem, mem],
        out_specs=row,
        out_shape=jax.ShapeDtypeStruct((n_seq, seq_len, CROSS_W), out_dtype),
        compiler_params=_params(2),
        name="cross",
    )(qc.reshape(n_seq, seq_len, CROSS_W), mk.reshape((n_seq,) + mem_shape), mv.reshape((n_seq,) + mem_shape))
    return o.reshape(n_seq * seq_len, CROSS_W)


def _merge_kernel(x_ref, om_ref, og_ref, oc_ref, nw_ref, wg_ref, bg_ref, wbm_ref, wbg_ref, wbc_ref,
                  wo_ref, h_ref):
    x = x_ref[...]
    d = x.shape[1]
    xb = _rms(x, nw_ref[...]).astype(BF16)
    merged = jnp.zeros(x.shape, F32)
    for i, (o_ref, wb_ref) in enumerate(((om_ref, wbm_ref), (og_ref, wbg_ref), (oc_ref, wbc_ref))):
        cols = slice(i * d, (i + 1) * d)
        gate = _sigmoid(_dot(xb, wg_ref[:, cols]) + bg_ref[:, cols])
        merged = merged + gate * _dot(o_ref[...].astype(BF16), wb_ref[...])
    h_ref[...] = x + _dot(merged.astype(BF16), wo_ref[...])


def _merge(x2d, om, og, oc, nw, wg, bg, wbm, wbg, wbc, wo):
    n, d = x2d.shape
    tm = min(ROW_TILE, n)
    row = lambda w: pl.BlockSpec((tm, w), lambda i: (i, 0))
    return pl.pallas_call(
        _merge_kernel,
        grid=(n // tm,),
        in_specs=[row(d), row(MOBA_W), row(GLA_VW), row(CROSS_W), _const_spec((1, d)),
                  _const_spec(wg.shape), _const_spec(bg.shape), _const_spec(wbm.shape),
                  _const_spec(wbg.shape), _const_spec(wbc.shape), _const_spec(wo.shape)],
        out_specs=row(d),
        out_shape=jax.ShapeDtypeStruct((n, d), F32),
        compiler_params=_params(1),
        name="merge",
    )(x2d, om, og, oc, nw, wg, bg, wbm, wbg, wbc, wo)


def _gelu_tanh(x):
    return 0.5 * x * (1.0 + jnp.tanh(0.7978845608028654 * (x + 0.044715 * (x * x * x))))


def _ffn_body(h_ref, nw_ref, wup_ref, wc_ref, bc_ref, wdn_ref, nf_ref, y_ref, prev_rows, store_u):
    h = h_ref[...]
    hb = _rms(h, nw_ref[...]).astype(BF16)
    d_ff = wdn_ref.shape[0]
    n_col_chunks = FFN_COL_CHUNKS
    fc = d_ff // n_col_chunks
    assert fc % LANES == 0
    acc = jnp.zeros(h.shape, F32)
    for c in range(n_col_chunks):
        cols = slice(c * fc, (c + 1) * fc)
        u = _dot(hb, wup_ref[:, cols])
        gate = _dot(hb, wup_ref[:, d_ff + c * fc:d_ff + (c + 1) * fc])
        u1, u2 = prev_rows(cols, u)
        store_u(cols, u)
        conv = bc_ref[:, cols] + wc_ref[0:1, cols] * u2 + wc_ref[1:2, cols] * u1 + wc_ref[2:3, cols] * u
        act = (_gelu_tanh(conv) * gate).astype(BF16)
        acc = acc + _dot(act, wdn_ref[cols, :])
    y_ref[...] = _rms(h + acc, nf_ref[...])


def _ffn_long_kernel(h_ref, nw_ref, wup_ref, wc_ref, bc_ref, wdn_ref, nf_ref, y_ref, cs_ref, carry_sc):
    tm = h_ref.shape[0]

    @pl.when(pl.program_id(1) == 0)
    def _():
        carry_sc[...] = jnp.zeros(carry_sc.shape, F32)

    row = lax.broadcasted_iota(jnp.int32, (tm, 1), 0)

    def prev_rows(cols, u):
        last = carry_sc[SUBLANES - 1:SUBLANES, cols]
        last2 = carry_sc[SUBLANES - 2:SUBLANES - 1, cols]
        u1 = jnp.where(row == 0, last, pltpu.roll(u, 1, axis=0))
        u2 = jnp.where(row == 0, last2, jnp.where(row == 1, last, pltpu.roll(u, 2, axis=0)))
        return u1, u2

    def store_u(cols, u):
        carry_sc[:, cols] = u[tm - SUBLANES:tm]
        cs_ref[0, :, cols] = u[tm - (CONV_W - 1):tm]

    _ffn_body(h_ref, nw_ref, wup_ref, wc_ref, bc_ref, wdn_ref, nf_ref, y_ref, prev_rows, store_u)


def _ffn_short_kernel(h_ref, p1_ref, p2_ref, nw_ref, wup_ref, wc_ref, bc_ref, wdn_ref, nf_ref,
                      y_ref, u_ref, *, seq_len):
    tm = h_ref.shape[0]
    pos = lax.broadcasted_iota(jnp.int32, (tm, 1), 0) % seq_len

    def prev_rows(cols, u):
        u1 = jnp.where(pos >= 1, pltpu.roll(u, 1, axis=0), p1_ref[:, cols])
        u2 = jnp.where(pos >= 2, pltpu.roll(u, 2, axis=0), p2_ref[:, cols])
        return u1, u2

    def store_u(cols, u):
        u_ref[:, cols] = u

    _ffn_body(h_ref, nw_ref, wup_ref, wc_ref, bc_ref, wdn_ref, nf_ref, y_ref, prev_rows, store_u)


def _ffn_weights_specs(d, wup, wc, bc, wdn):
    return [_const_spec((1, d)), _const_spec(wup.shape), _const_spec(wc.shape), _const_spec(bc.shape),
            _const_spec(wdn.shape), _const_spec((1, d))]


def _ffn_long(h2d, n_seq, seq_len, nw, wup, wc, bc, wdn, nf):
    n, d = h2d.shape
    d_ff = wdn.shape[0]
    tm = min(ROW_TILE, n)
    assert seq_len % tm == 0
    nt = seq_len // tm
    row = pl.BlockSpec((tm, d), lambda b, t: (b * nt + t, 0))
    return pl.pallas_call(
        _ffn_long_kernel,
        grid=(n_seq, nt),
        in_specs=[row] + _ffn_weights_specs(d, wup, wc, bc, wdn),
        out_specs=[row, pl.BlockSpec((1, CONV_W - 1, d_ff), lambda b, t: (b, 0, 0))],
        out_shape=[jax.ShapeDtypeStruct((n, d), F32),
                   jax.ShapeDtypeStruct((n_seq, CONV_W - 1, d_ff), F32)],
        scratch_shapes=[pltpu.VMEM((SUBLANES, d_ff), F32)],
        compiler_params=_params(2),
        name="ffn_long",
    )(h2d, nw, wup, wc, bc, wdn, nf)


def _ffn_short(h2d, conv_prev, n_seq, seq_len, nw, wup, wc, bc, wdn, nf):
    n, d = h2d.shape
    d_ff = wdn.shape[0]
    tm = min(ROW_TILE, n)
    assert tm % seq_len == 0 and n % tm == 0 and seq_len >= CONV_W - 1
    pos = jnp.arange(seq_len)[None, :, None]
    older, newer = conv_prev[:, 0][:, None, :], conv_prev[:, 1][:, None, :]
    p1 = jnp.where(pos == 0, newer, 0.0).reshape(n, d_ff)
    p2 = jnp.where(pos == 0, older, jnp.where(pos == 1, newer, 0.0)).reshape(n, d_ff)
    row = lambda w: pl.BlockSpec((tm, w), lambda i: (i, 0))
    y, u = pl.pallas_call(
        functools.partial(_ffn_short_kernel, seq_len=seq_len),
        grid=(n // tm,),
        in_specs=[row(d), row(d_ff), row(d_ff)] + _ffn_weights_specs(d, wup, wc, bc, wdn),
        out_specs=[row(d), row(d_ff)],
        out_shape=[jax.ShapeDtypeStruct((n, d), F32), jax.ShapeDtypeStruct((n, d_ff), F32)],
        compiler_params=_params(1),
        name="ffn_short",
    )(h2d, p1, p2, nw, wup, wc, bc, wdn, nf)
    return y, u.reshape(n_seq, seq_len, d_ff)[:, seq_len - (CONV_W - 1):]


def _prep_weights(norm_mix, w_in, w_gla_a2, b_gla_a, norm_gla, norm_mem, w_mem_kv, w_br_moba, w_br_gla,
                  w_br_cross, w_gate, b_gate, w_out, norm_ffn, w_up, w_conv, b_conv, w_down, norm_final):
    d = w_in.shape[0]
    o_a = C_QC
    w_in_p = jnp.concatenate(
        [w_in[:, :o_a].astype(BF16), w_in[:, o_a + GLA_RANK:].astype(BF16),
         w_in[:, o_a:o_a + GLA_RANK].astype(BF16), jnp.zeros((d, A_PAD - GLA_RANK), BF16)], axis=1)
    wa2_p = jnp.concatenate([w_gla_a2, jnp.zeros((A_PAD - GLA_RANK, GLA_KW), w_gla_a2.dtype)],
                            axis=0).astype(BF16)
    r2 = lambda a: a.reshape(1, -1)
    return dict(
        norm_mix=r2(norm_mix), w_in=w_in_p, wa2=wa2_p, ba=r2(b_gla_a), norm_gla=r2(norm_gla),
        norm_mem=r2(norm_mem), w_mem_kv=w_mem_kv.astype(BF16), wbm=w_br_moba.astype(BF16),
        wbg=w_br_gla.astype(BF16), wbc=w_br_cross.astype(BF16), wg=w_gate.astype(BF16), bg=r2(b_gate),
        wo=w_out.astype(BF16), norm_ffn=r2(norm_ffn), wup=w_up.astype(BF16), wc=w_conv, bc=r2(b_conv),
        wdn=w_down.astype(BF16), norm_final=r2(norm_final))


def kernel(x_prompt, x_sample, cache_moba_k, cache_moba_v, state_gla, state_conv, cache_mem_k, cache_mem_v, page_table, mem_prompt, norm_mix, w_in, w_gla_a2, b_gla_a, norm_gla, norm_mem, w_mem_kv, w_br_moba, w_br_gla, w_br_cross, w_gate, b_gate, w_out, norm_ffn, w_up, w_conv, b_conv, w_down, norm_final):
    depth = w_in.shape[0]
    assert depth == 1
    bp, sp, d = x_prompt.shape
    bs, ss, _ = x_sample.shape
    n_mem = mem_prompt.shape[1]
    w = _prep_weights(norm_mix[0], w_in[0], w_gla_a2[0], b_gla_a[0], norm_gla[0], norm_mem[0], w_mem_kv[0],
                      w_br_moba[0], w_br_gla[0], w_br_cross[0], w_gate[0], b_gate[0], w_out[0], norm_ffn[0],
                      w_up[0], w_conv[0], b_conv[0], w_down[0], norm_final)

    def mix(x2d, o_m, o_g, o_c):
        return _merge(x2d, o_m, o_g, o_c, w["norm_mix"], w["wg"], w["bg"], w["wbm"], w["wbg"], w["wbc"], w["wo"])

    ffn_w = (w["norm_ffn"], w["wup"], w["wc"], w["bc"], w["wdn"], w["norm_final"])

    xp = x_prompt.reshape(bp * sp, d)
    qm, kt_p, vt_p, qg, kg, vg, rg, qc, la, ksum, kb, vb = _project(
        xp, w["norm_mix"], w["w_in"], w["wa2"], w["ba"], BF16, MOBA_Q_SCALE, kv_seq_len=sp)
    o_m = _moba_prompt(qm, kb, vb, ksum.reshape(-1, MOBA_W), bp, sp)
    mk_p, mv_p = _memory_kv(mem_prompt.reshape(bp * n_mem, d), w["norm_mem"], w["w_mem_kv"])
    o_g, gla_p = _gla(qg, kg, vg, la, rg, jnp.zeros((bp, DV_GLA, GLA_KW), F32), w["norm_gla"], bp, sp, BF16)
    o_c = _cross(qc, mk_p, mv_p, bp, sp, n_mem, BF16)
    h_p = mix(xp, o_m, o_g, o_c)
    y_p, conv_p = _ffn_long(h_p, bp, sp, *ffn_w)

    xs = x_sample.reshape(bs * ss, d)
    qm, k_s, v_s, qg, kg, vg, rg, qc, la, _, _, _ = _project(
        xs, w["norm_mix"], w["w_in"], w["wa2"], w["ba"], F32, 1.0)
    r3 = lambda a: a.reshape(bs, ss, MOBA_W)
    o_m = _moba_sample(r3(qm), r3(k_s), r3(v_s), cache_moba_k[0], cache_moba_v[0],
                       page_table).reshape(bs * ss, MOBA_W)
    o_g, gla_s = _gla(qg, kg, vg, la, rg, _state_to_t(state_gla[0]), w["norm_gla"], bs, ss, F32)
    o_c = _cross(qc, cache_mem_k[0], cache_mem_v[0], bs, ss, n_mem, F32)
    h_s = mix(xs, o_m, o_g, o_c)
    y_s, conv_s = _ffn_short(h_s, state_conv[0], bs, ss, *ffn_w)

    kv5 = lambda a, b, s: a.reshape(1, b, s, H_MOBA, HD_MOBA)
    kv5_t = lambda a: jnp.transpose(a.reshape(bp, H_MOBA, HD_MOBA, sp), (0, 3, 1, 2))[None]
    return (y_p.reshape(bp, sp, d), y_s.reshape(bs, ss, d),
            kv5_t(kt_p), kv5_t(vt_p), kv5(k_s, bs, ss), kv5(v_s, bs, ss),
            _state_from_t(gla_p)[None], _state_from_t(gla_s)[None],
            conv_p[None], conv_s[None],
            mk_p.reshape(1, bp, n_mem, H_CROSS, HD_CROSS), mv_p.reshape(1, bp, n_mem, H_CROSS, HD_CROSS))
```

```python
import functools
import math

import jax
import jax.numpy as jnp
from jax import lax
from jax.experimental import pallas as pl
from jax.experimental.pallas import tpu as pltpu

F32 = jnp.float32
BF16 = jnp.bfloat16
NEG_INF = float("-inf")

EPS = 1e-6
H_MOBA, HD_MOBA = 8, 64
MOBA_W = H_MOBA * HD_MOBA
MOBA_BLOCK = 256
MOBA_TOPK = 3
PAGE_SIZE = 128
H_GLA, DK_GLA, DV_GLA = 4, 64, 128
GLA_KW, GLA_VW = H_GLA * DK_GLA, H_GLA * DV_GLA
GLA_RANK = 16
GLA_TAU = 16.0
GLA_CHUNK = 64
GLA_SUB = 16
H_CROSS, HD_CROSS = 4, 128
CROSS_W = H_CROSS * HD_CROSS
CONV_W = 3

LANES = 128
SUBLANES = 8
BF16_ROWS = 2 * SUBLANES
VMEM_LIMIT_BYTES = 56 * 1024 * 1024

MASK_BIAS = -1e30
NO_INDEX = float(2 ** 24)
MOBA_GROUP = 8
MOBA_GROUP_FINE = 2
MOBA_TQ = 2 * MOBA_BLOCK
MOBA_Q_SCALE = HD_MOBA ** -0.5 * math.log2(math.e)
ROW_TILE = 512
SEQ_TILE = 256
FFN_COL_CHUNKS = 2
GLA_STEP_CHUNKS = 8
CROSS_STEP_ROWS = 512
CROSS_MAX_GROUP = 8
PAGES_PER_STEP = 32
A_PAD = LANES

C_QM, C_K, C_V = 0, MOBA_W, 2 * MOBA_W
C_QG = 3 * MOBA_W
C_KG = C_QG + GLA_KW
C_VG = C_KG + GLA_KW
C_RG = C_VG + GLA_VW
C_QC = C_RG + GLA_VW
C_A = C_QC + CROSS_W
IN_COLS_PAD = C_A + A_PAD


def _params(n_axes):
    return pltpu.CompilerParams(
        dimension_semantics=("arbitrary",) * n_axes,
        vmem_limit_bytes=VMEM_LIMIT_BYTES,
    )


def _const_spec(shape):
    nd = len(shape)
    return pl.BlockSpec(shape, lambda *_: (0,) * nd, pipeline_mode=pl.Buffered(1))


def _rms(x, w):
    return x * lax.rsqrt(jnp.mean(x * x, axis=-1, keepdims=True) + EPS) * w


def _sigmoid(x):
    return 1.0 / (1.0 + jnp.exp(-x))


def _dot(a, b):
    return jnp.dot(a, b, preferred_element_type=F32)


def _dot_nt(a, b):
    return lax.dot_general(a, b, (((1,), (1,)), ((), ())), preferred_element_type=F32)


def _dot_tn(a, b):
    return lax.dot_general(a, b, (((0,), (0,)), ((), ())), preferred_element_type=F32)


def _proj_kernel(x_ref, nw_ref, w_ref, wa2_ref, ba_ref,
                 qm_ref, k_ref, v_ref, qg_ref, kg_ref, vg_ref, rg_ref, qc_ref, la_ref, ksum_ref,
                 kb_ref, vb_ref, *, qm_scale, kv_transposed):
    xb = _rms(x_ref[...], nw_ref[...]).astype(BF16)

    def mm(lo, hi):
        return _dot(xb, w_ref[:, lo:hi])

    def store_kv(ref, val):
        if kv_transposed:
            ref[0] = val.T
        else:
            ref[...] = val

    qm_ref[...] = (mm(C_QM, C_K) * qm_scale).astype(qm_ref.dtype)
    k = mm(C_K, C_V)
    store_kv(k_ref, k)
    kb_ref[...] = k.astype(BF16)
    for g in range(k.shape[0] // MOBA_BLOCK):
        ksum_ref[g] = jnp.sum(k[g * MOBA_BLOCK:(g + 1) * MOBA_BLOCK], axis=0, keepdims=True)
    v = mm(C_V, C_QG)
    store_kv(v_ref, v)
    vb_ref[...] = v.astype(BF16)
    qg_ref[...] = mm(C_QG, C_KG).astype(qg_ref.dtype)
    kg_ref[...] = mm(C_KG, C_VG).astype(kg_ref.dtype)
    vg_ref[...] = mm(C_VG, C_RG).astype(vg_ref.dtype)
    rg_ref[...] = mm(C_RG, C_QC)
    qc_ref[...] = mm(C_QC, C_A).astype(qc_ref.dtype)
    z = _dot(mm(C_A, IN_COLS_PAD).astype(BF16), wa2_ref[...]) + ba_ref[...]
    la_ref[...] = (jnp.minimum(z, 0.0) - jnp.log1p(jnp.exp(-jnp.abs(z)))) * (1.0 / GLA_TAU)


def _project(x2d, nw, w_in_p, wa2_p, ba, act_dtype, qm_scale, kv_seq_len=None):
    n, d = x2d.shape
    tm = min(ROW_TILE, n)
    assert n % tm == 0 and tm % MOBA_BLOCK == 0
    row = lambda w: pl.BlockSpec((tm, w), lambda i: (i, 0))
    widths = (MOBA_W, MOBA_W, MOBA_W, GLA_KW, GLA_KW, GLA_VW, GLA_VW, CROSS_W, GLA_KW)
    dtypes = (act_dtype, F32, F32, act_dtype, act_dtype, act_dtype, F32, act_dtype, F32)
    out_shape = [jax.ShapeDtypeStruct((n, w), dt) for w, dt in zip(widths, dtypes)]
    out_shape.append(jax.ShapeDtypeStruct((n // MOBA_BLOCK, 1, MOBA_W), F32))
    out_shape += [jax.ShapeDtypeStruct((n, MOBA_W), BF16)] * 2
    out_specs = [row(w) for w in widths]
    out_specs.append(pl.BlockSpec((tm // MOBA_BLOCK, 1, MOBA_W), lambda i: (i, 0, 0)))
    out_specs += [row(MOBA_W)] * 2
    if kv_seq_len is not None:
        assert kv_seq_len % tm == 0
        nt = kv_seq_len // tm
        for i in (1, 2):
            out_shape[i] = jax.ShapeDtypeStruct((n // kv_seq_len, MOBA_W, kv_seq_len), F32)
            out_specs[i] = pl.BlockSpec((1, MOBA_W, tm), lambda i: (i // nt, 0, i % nt))
    return pl.pallas_call(
        functools.partial(_proj_kernel, qm_scale=qm_scale, kv_transposed=kv_seq_len is not None),
        grid=(n // tm,),
        in_specs=[row(d), _const_spec((1, d)), _const_spec(w_in_p.shape),
                  _const_spec(wa2_p.shape), _const_spec((1, GLA_KW))],
        out_specs=out_specs,
        out_shape=out_shape,
        compiler_params=_params(1),
        name="proj",
    )(x2d, nw, w_in_p, wa2_p, ba)


def _memkv_kernel(m_ref, nw_ref, w_ref, mk_ref, mv_ref):
    mb = _rms(m_ref[...], nw_ref[...]).astype(BF16)
    mk_ref[...] = _dot(mb, w_ref[:, :CROSS_W])
    mv_ref[...] = _dot(mb, w_ref[:, CROSS_W:])


def _memory_kv(mem2d, nw, w_kv):
    n, d = mem2d.shape
    tm = min(ROW_TILE, n)
    assert n % tm == 0
    return pl.pallas_call(
        _memkv_kernel,
        grid=(n // tm,),
        in_specs=[pl.BlockSpec((tm, d), lambda i: (i, 0)), _const_spec((1, d)), _const_spec(w_kv.shape)],
        out_specs=[pl.BlockSpec((tm, CROSS_W), lambda i: (i, 0))] * 2,
        out_shape=[jax.ShapeDtypeStruct((n, CROSS_W), F32)] * 2,
        compiler_params=_params(1),
        name="memkv",
    )(mem2d, nw, w_kv)


def _select_topk(gate, idx, valid, axis):
    return _select_topk_each([gate], idx, valid, axis)[0]


def _select_topk_each(gates, idx, valid, axis):
    gs = [jnp.where(valid, gate, NEG_INF) for gate in gates]
    picked = [jnp.zeros(gate.shape, jnp.bool_) for gate in gates]
    for _ in range(MOBA_TOPK):
        ms = [jnp.max(g, axis=axis, keepdims=True) for g in gs]
        firsts = [jnp.min(jnp.where(g == m, idx, NO_INDEX), axis=axis, keepdims=True) for g, m in zip(gs, ms)]
        picks = [(idx == first) & (m > NEG_INF) & valid for first, m in zip(firsts, ms)]
        picked = [a | b for a, b in zip(picked, picks)]
        gs = [jnp.where(pick, NEG_INF, g) for pick, g in zip(picks, gs)]
    return picked


def _moba_prompt_kernel(q_ref, k_ref, v_ref, ksum_ref, o_ref, ka_sc, va_sc, s_sc, *, group, fine):
    t = pl.program_id(2)
    tq = q_ref.shape[0]
    blk = MOBA_BLOCK
    nblk = ksum_ref.shape[0]
    seq_len = k_ref.shape[0]
    n_heads = LANES // HD_MOBA
    spare_off = [((half + 1) % n_heads) * HD_MOBA for half in range(n_heads)]

    @pl.when(t == 0)
    def _():
        key_blk = lax.broadcasted_iota(jnp.int32, (seq_len, LANES), 0) // blk
        key_lane = lax.broadcasted_iota(jnp.int32, (seq_len, LANES), 1)
        k = k_ref[...]
        v = v_ref[...]
        for half in range(n_heads):
            in_head = (key_lane >= half * HD_MOBA) & (key_lane < (half + 1) * HD_MOBA)
            onehot = jnp.where(key_lane - spare_off[half] == key_blk, 1.0, 0.0).astype(BF16)
            ka_sc[half] = jnp.where(in_head, k, onehot)
            va_sc[half] = jnp.where(in_head, v, jnp.ones_like(v))

    q = q_ref[...]
    kmean = ksum_ref[...] * (1.0 / blk)
    km_hi = kmean.astype(BF16)
    km_lo = (kmean - km_hi.astype(F32)).astype(BF16)
    lane = lax.broadcasted_iota(jnp.int32, (1, LANES), 1)
    causal = (lax.broadcasted_iota(jnp.int32, (blk, blk), 1) <= lax.broadcasted_iota(jnp.int32, (blk, blk), 0))
    blk_id = lax.broadcasted_iota(jnp.int32, (nblk, tq), 0)
    place_row = lax.broadcasted_iota(jnp.int32, (nblk, LANES), 0)
    place_lane = lax.broadcasted_iota(jnp.int32, (nblk, LANES), 1)
    first_blk = t * (tq // blk)
    own_blk = first_blk + lax.broadcasted_iota(jnp.int32, (nblk, tq), 1) // blk
    own_parts = range(tq // blk)
    q_rows = [slice(j * blk, (j + 1) * blk) for j in own_parts]
    k_rows = [pl.ds(pl.multiple_of(t * tq + j * blk, blk), blk) for j in own_parts]
    heads = range(n_heads)

    in_head = [(lane >= half * HD_MOBA) & (lane < (half + 1) * HD_MOBA) for half in heads]
    qh = [jnp.where(in_head[half], q, jnp.zeros_like(q)) for half in heads]
    gate_t = [_dot_nt(km_hi, qh[half]) + _dot_nt(km_lo, qh[half]) for half in heads]
    s_own = [[jnp.where(causal, _dot_nt(qh[half][q_rows[j]], ka_sc[half, k_rows[j], :]), MASK_BIAS)
              for j in own_parts] for half in heads]
    picked_t = _select_topk_each(gate_t, blk_id.astype(F32), blk_id < own_blk, axis=0)
    place = [jnp.where(place_lane - spare_off[half] == place_row, 1.0, 0.0).astype(BF16) for half in heads]
    picked = [_dot_tn(jnp.where(picked_t[half], 1.0, 0.0).astype(BF16), place[half]) for half in heads]
    q_aug = []
    for half in heads:
        in_range = (lane >= spare_off[half]) & (lane < spare_off[half] + nblk)
        bias = jnp.where(in_range & (picked[half] < 0.5), MASK_BIAS, 0.0)
        q_aug.append(qh[half] + bias.astype(BF16))

    n_fine = (first_blk + tq // blk - 1 + fine - 1) // fine
    per_group = group // fine
    tail = n_fine % per_group
    tail = jnp.where(tail == per_group - 1, 0, tail) if per_group > 2 else tail
    n_groups = (n_fine - tail + per_group - 1) // per_group

    def unit_rows(u, n_units):
        return pl.ds(pl.multiple_of(u * (fine * blk), fine * blk), n_units * fine * blk)

    def score_units(u, n_units, mxs):
        s = [_dot_nt(q_aug[half], ka_sc[half, unit_rows(u, n_units), :]) for half in heads]
        out = []
        for half in heads:
            mx = mxs[half]
            for j in range(n_units):
                s_sc[half, u + j] = s[half][:, j * fine * blk:(j + 1) * fine * blk]
            for i in range(n_units * fine):
                mx = jnp.maximum(mx, s[half][:, i * blk:(i + 1) * blk])
            out.append(mx)
        return tuple(out)

    def value_units(u, n_units, accs):
        p = [jnp.concatenate([jnp.exp2(s_sc[half, u + j] - m[half]).astype(BF16) for j in range(n_units)], axis=1)
             for half in heads]
        return tuple(accs[half] + _dot(p[half], va_sc[half, unit_rows(u, n_units), :]) for half in heads)

    tail_start = n_groups * per_group
    mx0 = tuple(jnp.concatenate(s_own[half], axis=0) for half in heads)
    mxs = lax.fori_loop(0, n_groups, lambda g, c: score_units(g * per_group, per_group, c), mx0)
    mxs = lax.fori_loop(0, tail, lambda r, c: score_units(tail_start + r, 1, c), mxs)
    m = [jnp.max(mx, axis=-1, keepdims=True) for mx in mxs]

    acc0 = tuple(
        jnp.concatenate([_dot(jnp.exp2(s_own[half][j] - m[half][q_rows[j]]).astype(BF16), va_sc[half, k_rows[j], :])
                         for j in own_parts], axis=0)
        for half in heads)
    accs = lax.fori_loop(0, n_groups, lambda g, c: value_units(g * per_group, per_group, c), acc0)
    accs = lax.fori_loop(0, tail, lambda r, c: value_units(tail_start + r, 1, c), accs)

    o = jnp.zeros((tq, LANES), F32)
    for half in heads:
        row_sum = accs[half][:, spare_off[half]:spare_off[half] + 1]
        o = jnp.where(in_head[half], accs[half] / row_sum, o)
    o_ref[...] = o.astype(o_ref.dtype)


def _moba_prompt(qm, kb, vb, ksum, n_seq, seq_len):
    assert seq_len % MOBA_BLOCK == 0
    nblk = seq_len // MOBA_BLOCK
    assert nblk <= HD_MOBA
    group, fine = (MOBA_GROUP, MOBA_GROUP_FINE) if nblk % MOBA_GROUP == 0 else (1, 1)
    tq = MOBA_TQ if seq_len % MOBA_TQ == 0 else MOBA_BLOCK
    nt = seq_len // tq
    n = n_seq * seq_len
    n_pair = MOBA_W // LANES
    n_heads = LANES // HD_MOBA
    seq_block = pl.BlockSpec((seq_len, LANES), lambda b, hp, t: (b, hp), pipeline_mode=pl.Buffered(1))
    return pl.pallas_call(
        functools.partial(_moba_prompt_kernel, group=group, fine=fine),
        grid=(n_seq, n_pair, nt),
        in_specs=[
            pl.BlockSpec((tq, LANES), lambda b, hp, t: (b * nt + t, hp)),
            seq_block, seq_block,
            pl.BlockSpec((nblk, LANES), lambda b, hp, t: (b, hp)),
        ],
        out_specs=pl.BlockSpec((tq, LANES), lambda b, hp, t: (b * nt + t, hp)),
        out_shape=jax.ShapeDtypeStruct((n, MOBA_W), BF16),
        scratch_shapes=[pltpu.VMEM((n_heads, seq_len, LANES), BF16),
                        pltpu.VMEM((n_heads, seq_len, LANES), BF16),
                        pltpu.VMEM((n_heads, nblk // fine, tq, fine * MOBA_BLOCK), F32)],
        compiler_params=_params(3),
        name="moba_prompt",
    )(qm, kb, vb, ksum)


def _moba_sample_kernel(pt_ref, qbd_ref, kn_ref, vnt_ref, *rest, n_q):
    del pt_ref
    pps = PAGES_PER_STEP
    k_pages, v_pages = rest[:pps], rest[pps:2 * pps]
    o_ref, m_sc, l_sc, gate_sc, acc_sc = rest[2 * pps:]
    s_id = pl.program_id(1)
    n_step = pl.num_programs(1)
    nblk = m_sc.shape[1]
    n_hq = qbd_ref.shape[1]
    qbd = qbd_ref[0]
    col_head = lax.broadcasted_iota(jnp.int32, (HD_MOBA, n_hq), 1) // n_q
    blk_col = lax.broadcasted_iota(jnp.int32, (n_hq, nblk), 1)
    pages_per_blk = MOBA_BLOCK // PAGE_SIZE

    def own_head(pvt):
        out = pvt[0:HD_MOBA]
        for h in range(1, H_MOBA):
            out = jnp.where(col_head == h, pvt[h * HD_MOBA:(h + 1) * HD_MOBA], out)
        return out

    def block_t(page_refs, i):
        pages = range(i * pages_per_blk, (i + 1) * pages_per_blk)
        return jnp.concatenate([page_refs[j][0] for j in pages], axis=1).astype(BF16)

    blocks = range(pps // pages_per_blk)
    b_idx = [s_id * (pps // pages_per_blk) + i for i in blocks]
    s = [_dot(qbd, block_t(k_pages, i)) for i in blocks]
    gate = [jnp.sum(si, axis=-1, keepdims=True) for si in s]
    m = [jnp.max(si, axis=-1, keepdims=True) for si in s]
    p = [jnp.exp(si - mi) for si, mi in zip(s, m)]
    l = [jnp.sum(pi, axis=-1, keepdims=True) for pi in p]
    pv = [own_head(_dot_nt(block_t(v_pages, i), p[i].astype(BF16))) for i in blocks]
    for i in blocks:
        acc_sc[b_idx[i]] = pv[i]
    for ref, vals in ((m_sc, m), (l_sc, l), (gate_sc, gate)):
        cur = ref[...]
        for i in blocks:
            cur = jnp.where(blk_col == b_idx[i], vals[i], cur)
        ref[...] = cur

    @pl.when(s_id == n_step - 1)
    def _():
        picked = _select_topk(gate_sc[...], blk_col.astype(F32), blk_col >= 0, axis=1)
        m_all = jnp.where(picked, m_sc[...], NEG_INF)
        kn = kn_ref[0].astype(BF16)
        vnt = vnt_ref[0].astype(BF16)
        n_new = kn.shape[0]
        key_i = lax.broadcasted_iota(jnp.int32, (n_hq, n_new), 1)
        q_i = lax.broadcasted_iota(jnp.int32, (n_hq, n_new), 0) % n_q
        s_own = jnp.where(key_i <= q_i, _dot_nt(qbd, kn), NEG_INF)
        m_tot = jnp.maximum(jnp.max(s_own, axis=-1, keepdims=True),
                            jnp.max(m_all, axis=-1, keepdims=True))
        p_own = jnp.exp(s_own - m_tot)
        c = jnp.exp(m_all - m_tot)
        l_tot = jnp.sum(p_own, axis=-1, keepdims=True) + jnp.sum(c * l_sc[...], axis=-1, keepdims=True)
        inv_l = 1.0 / l_tot
        c_t = jnp.transpose(c * inv_l)
        out = own_head(_dot_nt(vnt, (p_own * inv_l).astype(BF16)))
        for b in range(nblk):
            out = out + acc_sc[b] * c_t[b:b + 1, :]
        o_ref[0] = out


def _moba_sample(q, k_new, v_new, cache_k, cache_v, page_table):
    n_seq, n_q, _ = q.shape
    n_phys = cache_k.shape[0]
    n_pages = page_table.shape[1]
    past = n_pages * PAGE_SIZE
    assert past % MOBA_BLOCK == 0 and n_pages % PAGES_PER_STEP == 0
    assert n_q <= MOBA_BLOCK and n_q % SUBLANES == 0
    nblk = past // MOBA_BLOCK
    n_hq = H_MOBA * n_q
    lane_head = jnp.arange(MOBA_W) // HD_MOBA
    qbd = jnp.where(lane_head[None, None, None, :] == jnp.arange(H_MOBA)[None, :, None, None],
                    q[:, None, :, :] * (HD_MOBA ** -0.5), 0.0).reshape(n_seq, n_hq, MOBA_W).astype(BF16)
    n_new = max(BF16_ROWS, n_q)
    pad = ((0, 0), (0, n_new - n_q), (0, 0))
    kn = jnp.pad(k_new, pad)
    vnt = jnp.transpose(jnp.pad(v_new, pad), (0, 2, 1))
    to_t = lambda c: jnp.transpose(c, (0, 2, 3, 1)).reshape(n_phys, MOBA_W, PAGE_SIZE)

    def page_spec(i):
        return pl.BlockSpec((1, MOBA_W, PAGE_SIZE), lambda b, s, pt: (pt[b, s * PAGES_PER_STEP + i], 0, 0))

    seq3 = lambda r, c: pl.BlockSpec((1, r, c), lambda b, s, pt: (b, 0, 0))
    grid_spec = pltpu.PrefetchScalarGridSpec(
        num_scalar_prefetch=1,
        grid=(n_seq, n_pages // PAGES_PER_STEP),
        in_specs=[seq3(n_hq, MOBA_W), seq3(n_new, MOBA_W), seq3(MOBA_W, n_new)]
        + [page_spec(i) for i in range(PAGES_PER_STEP)] * 2,
        out_specs=seq3(HD_MOBA, n_hq),
        scratch_shapes=[
            pltpu.VMEM((n_hq, nblk), F32),
            pltpu.VMEM((n_hq, nblk), F32),
            pltpu.VMEM((n_hq, nblk), F32),
            pltpu.VMEM((nblk, HD_MOBA, n_hq), F32),
        ],
    )
    out = pl.pallas_call(
        functools.partial(_moba_sample_kernel, n_q=n_q),
        grid_spec=grid_spec,
        out_shape=jax.ShapeDtypeStruct((n_seq, HD_MOBA, n_hq), F32),
        compiler_params=_params(2),
        name="moba_sample",
    )(page_table, qbd, kn, vnt, *([to_t(cache_k)] * PAGES_PER_STEP), *([to_t(cache_v)] * PAGES_PER_STEP))
    return jnp.transpose(out.reshape(n_seq, HD_MOBA, H_MOBA, n_q), (0, 3, 2, 1)).reshape(n_seq, n_q, MOBA_W)


def _cumsum_rows(x):
    n = x.shape[0]
    row = lax.broadcasted_iota(jnp.int32, x.shape, 0)
    s = 1
    while s < n:
        x = x + jnp.where(row >= s, pltpu.roll(x, s, axis=0), 0.0)
        s *= 2
    return x


def _gla_chunk(q, k, v, la, st, cast):
    c = q.shape[0]
    sub = min(GLA_SUB, c)
    lane = lax.broadcasted_iota(jnp.int32, (1, GLA_KW), 1)
    head_masks = [(lane >= h * DK_GLA) & (lane < (h + 1) * DK_GLA) for h in range(H_GLA)]
    g = _cumsum_rows(la)
    krow = lax.broadcasted_iota(jnp.int32, (c, GLA_KW), 0)
    a_row = lax.broadcasted_iota(jnp.int32, (H_GLA * sub, c), 0)
    a_col = lax.broadcasted_iota(jnp.int32, (H_GLA * sub, c), 1)

    subs = range(c // sub)
    g_ref = [jnp.zeros((1, GLA_KW), F32) if i == 0 else g[i * sub - 1:i * sub] for i in subs]
    qt = [q[i * sub:(i + 1) * sub] * jnp.exp(g[i * sub:(i + 1) * sub] - g_ref[i]) for i in subs]
    kt = [jnp.where(krow < (i + 1) * sub, k * jnp.exp(g_ref[i] - g), 0.0) for i in subs]
    q_stack = [jnp.concatenate([jnp.where(hm, qt[i], 0.0) for hm in head_masks], axis=0) for i in subs]
    a = [_dot_nt(cast(q_stack[i]), cast(kt[i])) for i in subs]
    a_parts = [jnp.where(a_col <= i * sub + a_row % sub, a[i], 0.0) for i in subs]

    heads = range(H_GLA)
    qe = q * jnp.exp(g)
    g_last = g[c - 1:c]
    k_dec = cast(k * jnp.exp(g_last - g))
    st_c = cast(st)
    a_h = [jnp.concatenate([ap[h * sub:(h + 1) * sub] for ap in a_parts], axis=0) for h in heads]
    v_h = [cast(v[:, h * DV_GLA:(h + 1) * DV_GLA]) for h in heads]
    o_inter = [_dot_nt(cast(jnp.where(head_masks[h], qe, 0.0)), st_c) for h in heads]
    o_intra = [_dot(cast(a_h[h]), v_h[h]) for h in heads]
    upd = [_dot_tn(v_h[h], k_dec) for h in heads]
    st_new = st * jnp.exp(g_last)
    for h in heads:
        st_new = st_new + jnp.where(head_masks[h], upd[h], 0.0)
    return [o_intra[h] + o_inter[h] for h in heads], st_new


def _gla_kernel(q_ref, k_ref, v_ref, la_ref, r_ref, s0_ref, nw_ref, o_ref, sfin_ref, st_sc, *, chunk):
    t = pl.program_id(1)

    @pl.when(t == 0)
    def _():
        st_sc[...] = s0_ref[...]

    cast = (lambda a: a.astype(BF16)) if chunk >= BF16_ROWS else (lambda a: a)
    n_group = q_ref.shape[0]
    sts = [st_sc[g] for g in range(n_group)]
    for c in range(q_ref.shape[1] // chunk):
        rows = slice(c * chunk, (c + 1) * chunk)
        for g in range(n_group):
            q = q_ref[g, rows, :].astype(F32) * (DK_GLA ** -0.5)
            k = k_ref[g, rows, :].astype(F32)
            outs, sts[g] = _gla_chunk(q, k, v_ref[g, rows, :], la_ref[g, rows, :], sts[g], cast)
            for h, o in enumerate(outs):
                lanes = slice(h * DV_GLA, (h + 1) * DV_GLA)
                r = r_ref[g, rows, lanes]
                o = o * lax.rsqrt(jnp.mean(o * o, axis=-1, keepdims=True) + EPS)
                o_ref[g, rows, lanes] = (o * nw_ref[:, lanes] * (r * _sigmoid(r))).astype(o_ref.dtype)
    for g in range(n_group):
        st_sc[g] = sts[g]

    @pl.when(t == pl.num_programs(1) - 1)
    def _():
        for g in range(n_group):
            sfin_ref[g] = sts[g]


def _gla(qg, kg, vg, la, rg, state_t, nw, n_seq, seq_len, out_dtype):
    chunk = math.gcd(seq_len, GLA_CHUNK)
    tl = min(seq_len, SEQ_TILE)
    assert seq_len % tl == 0 and tl % chunk == 0
    nt = seq_len // tl
    n = n_seq * seq_len
    n_group = math.gcd(n_seq, max(1, GLA_STEP_CHUNKS * chunk // tl))
    seq3 = lambda a: a.reshape(n_seq, seq_len, a.shape[-1])
    row = lambda w: pl.BlockSpec((n_group, tl, w), lambda b, t: (b, t, 0))
    st_spec = pl.BlockSpec((n_group, DV_GLA, GLA_KW), lambda b, t: (b, 0, 0))
    o, st = pl.pallas_call(
        functools.partial(_gla_kernel, chunk=chunk),
        grid=(n_seq // n_group, nt),
        in_specs=[row(GLA_KW), row(GLA_KW), row(GLA_VW), row(GLA_KW), row(GLA_VW), st_spec,
                  _const_spec((1, GLA_VW))],
        out_specs=[row(GLA_VW), st_spec],
        out_shape=[jax.ShapeDtypeStruct((n_seq, seq_len, GLA_VW), out_dtype),
                   jax.ShapeDtypeStruct((n_seq, DV_GLA, GLA_KW), F32)],
        scratch_shapes=[pltpu.VMEM((n_group, DV_GLA, GLA_KW), F32)],
        compiler_params=_params(2),
        name="gla",
    )(seq3(qg), seq3(kg), seq3(vg), seq3(la), seq3(rg), state_t, nw)
    return o.reshape(n, GLA_VW), st


def _state_to_t(s):
    n = s.shape[0]
    return jnp.transpose(s, (0, 3, 1, 2)).reshape(n, DV_GLA, GLA_KW)


def _state_from_t(st):
    n = st.shape[0]
    return jnp.transpose(st.reshape(n, DV_GLA, H_GLA, DK_GLA), (0, 2, 3, 1))


def _cross_kernel(q_ref, mk_ref, mv_ref, o_ref, *, n_mem, rows_by_head):
    scale = HD_CROSS ** -0.5
    small = q_ref.shape[1] < BF16_ROWS
    cast = (lambda a: a) if small else (lambda a: a.astype(BF16))
    units = [(g, h) for g in range(q_ref.shape[0]) for h in range(H_CROSS)]
    lanes = lambda h: slice(h * HD_CROSS, (h + 1) * HD_CROSS)

    def mem_head(ref, g, h):
        if rows_by_head:
            return ref[g, pl.ds(h, n_mem, stride=H_CROSS), :]
        return ref[g, :, lanes(h)]

    s = [_dot_nt(cast(q_ref[g, :, lanes(h)]), cast(mem_head(mk_ref, g, h))) * scale for g, h in units]
    p = [jnp.exp(si - jnp.max(si, axis=-1, keepdims=True)) for si in s]
    l = [jnp.sum(pi, axis=-1, keepdims=True) for pi in p]
    for (g, h), pi, li in zip(units, p, l):
        o_ref[g, :, lanes(h)] = (_dot(cast(pi), cast(mem_head(mv_ref, g, h))) / li).astype(o_ref.dtype)


def _cross(qc, mk, mv, n_seq, seq_len, n_mem, out_dtype):
    tl = min(seq_len, CROSS_STEP_ROWS)
    assert seq_len % tl == 0
    nt = seq_len // tl
    n_group = math.gcd(n_seq, min(CROSS_MAX_GROUP, max(1, CROSS_STEP_ROWS // tl)))
    row = pl.BlockSpec((n_group, tl, CROSS_W), lambda b, t: (b, t, 0))
    rows_by_head = mk.ndim == 4
    mem_shape = (n_mem * H_CROSS, HD_CROSS) if rows_by_head else (n_mem, CROSS_W)
    mem = pl.BlockSpec((n_group,) + mem_shape, lambda b, t: (b, 0, 0))
    o = pl.pallas_call(
        functools.partial(_cross_kernel, n_mem=n_mem, rows_by_head=rows_by_head),
        grid=(n_seq // n_group, nt),
        in_specs=[row, mem, mem],
        out_specs=row,
        out_shape=jax.ShapeDtypeStruct((n_seq, seq_len, CROSS_W), out_dtype),
        compiler_params=_params(2),
        name="cross",
    )(qc.reshape(n_seq, seq_len, CROSS_W), mk.reshape((n_seq,) + mem_shape), mv.reshape((n_seq,) + mem_shape))
    return o.reshape(n_seq * seq_len, CROSS_W)


def _merge_kernel(x_ref, om_ref, og_ref, oc_ref, nw_ref, wg_ref, bg_ref, wbm_ref, wbg_ref, wbc_ref,
                  wo_ref, h_ref):
    x = x_ref[...]
    d = x.shape[1]
    xb = _rms(x, nw_ref[...]).astype(BF16)
    merged = jnp.zeros(x.shape, F32)
    for i, (o_ref, wb_ref) in enumerate(((om_ref, wbm_ref), (og_ref, wbg_ref), (oc_ref, wbc_ref))):
        cols = slice(i * d, (i + 1) * d)
        gate = _sigmoid(_dot(xb, wg_ref[:, cols]) + bg_ref[:, cols])
        merged = merged + gate * _dot(o_ref[...].astype(BF16), wb_ref[...])
    h_ref[...] = x + _dot(merged.astype(BF16), wo_ref[...])


def _merge(x2d, om, og, oc, nw, wg, bg, wbm, wbg, wbc, wo):
    n, d = x2d.shape
    tm = min(ROW_TILE, n)
    row = lambda w: pl.BlockSpec((tm, w), lambda i: (i, 0))
    return pl.pallas_call(
        _merge_kernel,
        grid=(n // tm,),
        in_specs=[row(d), row(MOBA_W), row(GLA_VW), row(CROSS_W), _const_spec((1, d)),
                  _const_spec(wg.shape), _const_spec(bg.shape), _const_spec(wbm.shape),
                  _const_spec(wbg.shape), _const_spec(wbc.shape), _const_spec(wo.shape)],
        out_specs=row(d),
        out_shape=jax.ShapeDtypeStruct((n, d), F32),
        compiler_params=_params(1),
        name="merge",
    )(x2d, om, og, oc, nw, wg, bg, wbm, wbg, wbc, wo)


def _gelu_tanh(x):
    return 0.5 * x * (1.0 + jnp.tanh(0.7978845608028654 * (x + 0.044715 * (x * x * x))))


def _ffn_body(h_ref, nw_ref, wup_ref, wc_ref, bc_ref, wdn_ref, nf_ref, y_ref, prev_rows, store_u):
    h = h_ref[...]
    hb = _rms(h, nw_ref[...]).astype(BF16)
    d_ff = wdn_ref.shape[0]
    n_col_chunks = FFN_COL_CHUNKS
    fc = d_ff // n_col_chunks
    assert fc % LANES == 0
    acc = jnp.zeros(h.shape, F32)
    for c in range(n_col_chunks):
        cols = slice(c * fc, (c + 1) * fc)
        u = _dot(hb, wup_ref[:, cols])
        gate = _dot(hb, wup_ref[:, d_ff + c * fc:d_ff + (c + 1) * fc])
        u1, u2 = prev_rows(cols, u)
        store_u(cols, u)
        conv = bc_ref[:, cols] + wc_ref[0:1, cols] * u2 + wc_ref[1:2, cols] * u1 + wc_ref[2:3, cols] * u
        act = (_gelu_tanh(conv) * gate).astype(BF16)
        acc = acc + _dot(act, wdn_ref[cols, :])
    y_ref[...] = _rms(h + acc, nf_ref[...])


def _ffn_long_kernel(h_ref, nw_ref, wup_ref, wc_ref, bc_ref, wdn_ref, nf_ref, y_ref, cs_ref, carry_sc):
    tm = h_ref.shape[0]

    @pl.when(pl.program_id(1) == 0)
    def _():
        carry_sc[...] = jnp.zeros(carry_sc.shape, F32)

    row = lax.broadcasted_iota(jnp.int32, (tm, 1), 0)

    def prev_rows(cols, u):
        last = carry_sc[SUBLANES - 1:SUBLANES, cols]
        last2 = carry_sc[SUBLANES - 2:SUBLANES - 1, cols]
        u1 = jnp.where(row == 0, last, pltpu.roll(u, 1, axis=0))
        u2 = jnp.where(row == 0, last2, jnp.where(row == 1, last, pltpu.roll(u, 2, axis=0)))
        return u1, u2

    def store_u(cols, u):
        carry_sc[:, cols] = u[tm - SUBLANES:tm]
        cs_ref[0, :, cols] = u[tm - (CONV_W - 1):tm]

    _ffn_body(h_ref, nw_ref, wup_ref, wc_ref, bc_ref, wdn_ref, nf_ref, y_ref, prev_rows, store_u)


def _ffn_short_kernel(h_ref, p1_ref, p2_ref, nw_ref, wup_ref, wc_ref, bc_ref, wdn_ref, nf_ref,
                      y_ref, u_ref, *, seq_len):
    tm = h_ref.shape[0]
    pos = lax.broadcasted_iota(jnp.int32, (tm, 1), 0) % seq_len

    def prev_rows(cols, u):
        u1 = jnp.where(pos >= 1, pltpu.roll(u, 1, axis=0), p1_ref[:, cols])
        u2 = jnp.where(pos >= 2, pltpu.roll(u, 2, axis=0), p2_ref[:, cols])
        return u1, u2

    def store_u(cols, u):
        u_ref[:, cols] = u

    _ffn_body(h_ref, nw_ref, wup_ref, wc_ref, bc_ref, wdn_ref, nf_ref, y_ref, prev_rows, store_u)


def _ffn_weights_specs(d, wup, wc, bc, wdn):
    return [_const_spec((1, d)), _const_spec(wup.shape), _const_spec(wc.shape), _const_spec(bc.shape),
            _const_spec(wdn.shape), _const_spec((1, d))]


def _ffn_long(h2d, n_seq, seq_len, nw, wup, wc, bc, wdn, nf):
    n, d = h2d.shape
    d_ff = wdn.shape[0]
    tm = min(ROW_TILE, n)
    assert seq_len % tm == 0
    nt = seq_len // tm
    row = pl.BlockSpec((tm, d), lambda b, t: (b * nt + t, 0))
    return pl.pallas_call(
        _ffn_long_kernel,
        grid=(n_seq, nt),
        in_specs=[row] + _ffn_weights_specs(d, wup, wc, bc, wdn),
        out_specs=[row, pl.BlockSpec((1, CONV_W - 1, d_ff), lambda b, t: (b, 0, 0))],
        out_shape=[jax.ShapeDtypeStruct((n, d), F32),
                   jax.ShapeDtypeStruct((n_seq, CONV_W - 1, d_ff), F32)],
        scratch_shapes=[pltpu.VMEM((SUBLANES, d_ff), F32)],
        compiler_params=_params(2),
        name="ffn_long",
    )(h2d, nw, wup, wc, bc, wdn, nf)


def _ffn_short(h2d, conv_prev, n_seq, seq_len, nw, wup, wc, bc, wdn, nf):
    n, d = h2d.shape
    d_ff = wdn.shape[0]
    tm = min(ROW_TILE, n)
    assert tm % seq_len == 0 and n % tm == 0 and seq_len >= CONV_W - 1
    pos = jnp.arange(seq_len)[None, :, None]
    older, newer = conv_prev[:, 0][:, None, :], conv_prev[:, 1][:, None, :]
    p1 = jnp.where(pos == 0, newer, 0.0).reshape(n, d_ff)
    p2 = jnp.where(pos == 0, older, jnp.where(pos == 1, newer, 0.0)).reshape(n, d_ff)
    row = lambda w: pl.BlockSpec((tm, w), lambda i: (i, 0))
    y, u = pl.pallas_call(
        functools.partial(_ffn_short_kernel, seq_len=seq_len),
        grid=(n // tm,),
        in_specs=[row(d), row(d_ff), row(d_ff)] + _ffn_weights_specs(d, wup, wc, bc, wdn),
        out_specs=[row(d), row(d_ff)],
        out_shape=[jax.ShapeDtypeStruct((n, d), F32), jax.ShapeDtypeStruct((n, d_ff), F32)],
        compiler_params=_params(1),
        name="ffn_short",
    )(h2d, p1, p2, nw, wup, wc, bc, wdn, nf)
    return y, u.reshape(n_seq, seq_len, d_ff)[:, seq_len - (CONV_W - 1):]


def _prep_weights(norm_mix, w_in, w_gla_a2, b_gla_a, norm_gla, norm_mem, w_mem_kv, w_br_moba, w_br_gla,
                  w_br_cross, w_gate, b_gate, w_out, norm_ffn, w_up, w_conv, b_conv, w_down, norm_final):
    d = w_in.shape[0]
    o_a = C_QC
    w_in_p = jnp.concatenate(
        [w_in[:, :o_a].astype(BF16), w_in[:, o_a + GLA_RANK:].astype(BF16),
         w_in[:, o_a:o_a + GLA_RANK].astype(BF16), jnp.zeros((d, A_PAD - GLA_RANK), BF16)], axis=1)
    wa2_p = jnp.concatenate([w_gla_a2, jnp.zeros((A_PAD - GLA_RANK, GLA_KW), w_gla_a2.dtype)],
                            axis=0).astype(BF16)
    r2 = lambda a: a.reshape(1, -1)
    return dict(
        norm_mix=r2(norm_mix), w_in=w_in_p, wa2=wa2_p, ba=r2(b_gla_a), norm_gla=r2(norm_gla),
        norm_mem=r2(norm_mem), w_mem_kv=w_mem_kv.astype(BF16), wbm=w_br_moba.astype(BF16),
        wbg=w_br_gla.astype(BF16), wbc=w_br_cross.astype(BF16), wg=w_gate.astype(BF16), bg=r2(b_gate),
        wo=w_out.astype(BF16), norm_ffn=r2(norm_ffn), wup=w_up.astype(BF16), wc=w_conv, bc=r2(b_conv),
        wdn=w_down.astype(BF16), norm_final=r2(norm_final))


def kernel(x_prompt, x_sample, cache_moba_k, cache_moba_v, state_gla, state_conv, cache_mem_k, cache_mem_v, page_table, mem_prompt, norm_mix, w_in, w_gla_a2, b_gla_a, norm_gla, norm_mem, w_mem_kv, w_br_moba, w_br_gla, w_br_cross, w_gate, b_gate, w_out, norm_ffn, w_up, w_conv, b_conv, w_down, norm_final):
    depth = w_in.shape[0]
    assert depth == 1
    bp, sp, d = x_prompt.shape
    bs, ss, _ = x_sample.shape
    n_mem = mem_prompt.shape[1]
    w = _prep_weights(norm_mix[0], w_in[0], w_gla_a2[0], b_gla_a[0], norm_gla[0], norm_mem[0], w_mem_kv[0],
                      w_br_moba[0], w_br_gla[0], w_br_cross[0], w_gate[0], b_gate[0], w_out[0], norm_ffn[0],
                      w_up[0], w_conv[0], b_conv[0], w_down[0], norm_final)

    def mix(x2d, o_m, o_g, o_c):
        return _merge(x2d, o_m, o_g, o_c, w["norm_mix"], w["wg"], w["bg"], w["wbm"], w["wbg"], w["wbc"], w["wo"])

    ffn_w = (w["norm_ffn"], w["wup"], w["wc"], w["bc"], w["wdn"], w["norm_final"])

    xp = x_prompt.reshape(bp * sp, d)
    qm, kt_p, vt_p, qg, kg, vg, rg, qc, la, ksum, kb, vb = _project(
        xp, w["norm_mix"], w["w_in"], w["wa2"], w["ba"], BF16, MOBA_Q_SCALE, kv_seq_len=sp)
    o_m = _moba_prompt(qm, kb, vb, ksum.reshape(-1, MOBA_W), bp, sp)
    mk_p, mv_p = _memory_kv(mem_prompt.reshape(bp * n_mem, d), w["norm_mem"], w["w_mem_kv"])
    o_g, gla_p = _gla(qg, kg, vg, la, rg, jnp.zeros((bp, DV_GLA, GLA_KW), F32), w["norm_gla"], bp, sp, BF16)
    o_c = _cross(qc, mk_p, mv_p, bp, sp, n_mem, BF16)
    h_p = mix(xp, o_m, o_g, o_c)
    y_p, conv_p = _ffn_long(h_p, bp, sp, *ffn_w)

    xs = x_sample.reshape(bs * ss, d)
    qm, k_s, v_s, qg, kg, vg, rg, qc, la, _, _, _ = _project(
        xs, w["norm_mix"], w["w_in"], w["wa2"], w["ba"], F32, 1.0)
    r3 = lambda a: a.reshape(bs, ss, MOBA_W)
    o_m = _moba_sample(r3(qm), r3(k_s), r3(v_s), cache_moba_k[0], cache_moba_v[0],
                       page_table).reshape(bs * ss, MOBA_W)
    o_g, gla_s = _gla(qg, kg, vg, la, rg, _state_to_t(state_gla[0]), w["norm_gla"], bs, ss, F32)
    o_c = _cross(qc, cache_mem_k[0], cache_mem_v[0], bs, ss, n_mem, F32)
    h_s = mix(xs, o_m, o_g, o_c)
    y_s, conv_s = _ffn_short(h_s, state_conv[0], bs, ss, *ffn_w)

    kv5 = lambda a, b, s: a.reshape(1, b, s, H_MOBA, HD_MOBA)
    kv5_t = lambda a: jnp.transpose(a.reshape(bp, H_MOBA, HD_MOBA, sp), (0, 3, 1, 2))[None]
    return (y_p.reshape(bp, sp, d), y_s.reshape(bs, ss, d),
            kv5_t(kt_p), kv5_t(vt_p), kv5(k_s, bs, ss), kv5(v_s, bs, ss),
            _state_from_t(gla_p)[None], _state_from_t(gla_s)[None],
            conv_p[None], conv_s[None],
            mk_p.reshape(1, bp, n_mem, H_CROSS, HD_CROSS), mv_p.reshape(1, bp, n_mem, H_CROSS, HD_CROSS))
```

```python
import functools
import math

import jax
import jax.numpy as jnp
from jax import lax
from jax.experimental import pallas as pl
from jax.experimental.pallas import tpu as pltpu

F32 = jnp.float32
BF16 = jnp.bfloat16
NEG_INF = float("-inf")

EPS = 1e-6
H_MOBA, HD_MOBA = 8, 64
MOBA_W = H_MOBA * HD_MOBA
MOBA_BLOCK = 256
MOBA_TOPK = 3
PAGE_SIZE = 128
H_GLA, DK_GLA, DV_GLA = 4, 64, 128
GLA_KW, GLA_VW = H_GLA * DK_GLA, H_GLA * DV_GLA
GLA_RANK = 16
GLA_TAU = 16.0
GLA_CHUNK = 64
GLA_SUB = 16
H_CROSS, HD_CROSS = 4, 128
CROSS_W = H_CROSS * HD_CROSS
CONV_W = 3

LANES = 128
SUBLANES = 8
BF16_ROWS = 2 * SUBLANES
VMEM_LIMIT_BYTES = 56 * 1024 * 1024

MASK_BIAS = -1e30
NO_INDEX = float(2 ** 24)
MOBA_GROUP = 8
MOBA_GROUP_FINE = 4
MOBA_TQ = 2 * MOBA_BLOCK
MOBA_Q_SCALE = HD_MOBA ** -0.5 * math.log2(math.e)
ROW_TILE = 512
SEQ_TILE = 256
FFN_COL_CHUNKS = 2
GLA_STEP_CHUNKS = 8
CROSS_STEP_ROWS = 512
CROSS_MAX_GROUP = 8
PAGES_PER_STEP = 32
A_PAD = LANES

C_QM, C_K, C_V = 0, MOBA_W, 2 * MOBA_W
C_QG = 3 * MOBA_W
C_KG = C_QG + GLA_KW
C_VG = C_KG + GLA_KW
C_RG = C_VG + GLA_VW
C_QC = C_RG + GLA_VW
C_A = C_QC + CROSS_W
IN_COLS_PAD = C_A + A_PAD


def _params(n_axes):
    return pltpu.CompilerParams(
        dimension_semantics=("arbitrary",) * n_axes,
        vmem_limit_bytes=VMEM_LIMIT_BYTES,
    )


def _const_spec(shape):
    nd = len(shape)
    return pl.BlockSpec(shape, lambda *_: (0,) * nd, pipeline_mode=pl.Buffered(1))


def _rms(x, w):
    return x * lax.rsqrt(jnp.mean(x * x, axis=-1, keepdims=True) + EPS) * w


def _sigmoid(x):
    return 1.0 / (1.0 + jnp.exp(-x))


def _dot(a, b):
    return jnp.dot(a, b, preferred_element_type=F32)


def _dot_nt(a, b):
    return lax.dot_general(a, b, (((1,), (1,)), ((), ())), preferred_element_type=F32)


def _dot_tn(a, b):
    return lax.dot_general(a, b, (((0,), (0,)), ((), ())), preferred_element_type=F32)


def _proj_kernel(x_ref, nw_ref, w_ref, wa2_ref, ba_ref,
                 qm_ref, k_ref, v_ref, qg_ref, kg_ref, vg_ref, rg_ref, qc_ref, la_ref, ksum_ref,
                 kb_ref, vb_ref, *, qm_scale, kv_transposed):
    xb = _rms(x_ref[...], nw_ref[...]).astype(BF16)

    def mm(lo, hi):
        return _dot(xb, w_ref[:, lo:hi])

    def store_kv(ref, val):
        if kv_transposed:
            ref[0] = val.T
        else:
            ref[...] = val

    qm_ref[...] = (mm(C_QM, C_K) * qm_scale).astype(qm_ref.dtype)
    k = mm(C_K, C_V)
    store_kv(k_ref, k)
    kb_ref[...] = k.astype(BF16)
    for g in range(k.shape[0] // MOBA_BLOCK):
        ksum_ref[g] = jnp.sum(k[g * MOBA_BLOCK:(g + 1) * MOBA_BLOCK], axis=0, keepdims=True)
    v = mm(C_V, C_QG)
    store_kv(v_ref, v)
    vb_ref[...] = v.astype(BF16)
    qg_ref[...] = mm(C_QG, C_KG).astype(qg_ref.dtype)
    kg_ref[...] = mm(C_KG, C_VG).astype(kg_ref.dtype)
    vg_ref[...] = mm(C_VG, C_RG).astype(vg_ref.dtype)
    rg_ref[...] = mm(C_RG, C_QC)
    qc_ref[...] = mm(C_QC, C_A).astype(qc_ref.dtype)
    z = _dot(mm(C_A, IN_COLS_PAD).astype(BF16), wa2_ref[...]) + ba_ref[...]
    la_ref[...] = (jnp.minimum(z, 0.0) - jnp.log1p(jnp.exp(-jnp.abs(z)))) * (1.0 / GLA_TAU)


def _project(x2d, nw, w_in_p, wa2_p, ba, act_dtype, qm_scale, kv_seq_len=None):
    n, d = x2d.shape
    tm = min(ROW_TILE, n)
    assert n % tm == 0 and tm % MOBA_BLOCK == 0
    row = lambda w: pl.BlockSpec((tm, w), lambda i: (i, 0))
    widths = (MOBA_W, MOBA_W, MOBA_W, GLA_KW, GLA_KW, GLA_VW, GLA_VW, CROSS_W, GLA_KW)
    dtypes = (act_dtype, F32, F32, act_dtype, act_dtype, act_dtype, F32, act_dtype, F32)
    out_shape = [jax.ShapeDtypeStruct((n, w), dt) for w, dt in zip(widths, dtypes)]
    out_shape.append(jax.ShapeDtypeStruct((n // MOBA_BLOCK, 1, MOBA_W), F32))
    out_shape += [jax.ShapeDtypeStruct((n, MOBA_W), BF16)] * 2
    out_specs = [row(w) for w in widths]
    out_specs.append(pl.BlockSpec((tm // MOBA_BLOCK, 1, MOBA_W), lambda i: (i, 0, 0)))
    out_specs += [row(MOBA_W)] * 2
    if kv_seq_len is not None:
        assert kv_seq_len % tm == 0
        nt = kv_seq_len // tm
        for i in (1, 2):
            out_shape[i] = jax.ShapeDtypeStruct((n // kv_seq_len, MOBA_W, kv_seq_len), F32)
            out_specs[i] = pl.BlockSpec((1, MOBA_W, tm), lambda i: (i // nt, 0, i % nt))
    return pl.pallas_call(
        functools.partial(_proj_kernel, qm_scale=qm_scale, kv_transposed=kv_seq_len is not None),
        grid=(n // tm,),
        in_specs=[row(d), _const_spec((1, d)), _const_spec(w_in_p.shape),
                  _const_spec(wa2_p.shape), _const_spec((1, GLA_KW))],
        out_specs=out_specs,
        out_shape=out_shape,
        compiler_params=_params(1),
        name="proj",
    )(x2d, nw, w_in_p, wa2_p, ba)


def _memkv_kernel(m_ref, nw_ref, w_ref, mk_ref, mv_ref):
    mb = _rms(m_ref[...], nw_ref[...]).astype(BF16)
    mk_ref[...] = _dot(mb, w_ref[:, :CROSS_W])
    mv_ref[...] = _dot(mb, w_ref[:, CROSS_W:])


def _memory_kv(mem2d, nw, w_kv):
    n, d = mem2d.shape
    tm = min(ROW_TILE, n)
    assert n % tm == 0
    return pl.pallas_call(
        _memkv_kernel,
        grid=(n // tm,),
        in_specs=[pl.BlockSpec((tm, d), lambda i: (i, 0)), _const_spec((1, d)), _const_spec(w_kv.shape)],
        out_specs=[pl.BlockSpec((tm, CROSS_W), lambda i: (i, 0))] * 2,
        out_shape=[jax.ShapeDtypeStruct((n, CROSS_W), F32)] * 2,
        compiler_params=_params(1),
        name="memkv",
    )(mem2d, nw, w_kv)


def _select_topk(gate, idx, valid, axis):
    return _select_topk_each([gate], idx, valid, axis)[0]


def _select_topk_each(gates, idx, valid, axis):
    gs = [jnp.where(valid, gate, NEG_INF) for gate in gates]
    picked = [jnp.zeros(gate.shape, jnp.bool_) for gate in gates]
    for _ in range(MOBA_TOPK):
        ms = [jnp.max(g, axis=axis, keepdims=True) for g in gs]
        firsts = [jnp.min(jnp.where(g == m, idx, NO_INDEX), axis=axis, keepdims=True) for g, m in zip(gs, ms)]
        picks = [(idx == first) & (m > NEG_INF) & valid for first, m in zip(firsts, ms)]
        picked = [a | b for a, b in zip(picked, picks)]
        gs = [jnp.where(pick, NEG_INF, g) for pick, g in zip(picks, gs)]
    return picked


def _moba_prompt_kernel(q_ref, k_ref, v_ref, ksum_ref, o_ref, ka_sc, va_sc, s_sc, *, group, fine):
    t = pl.program_id(2)
    tq = q_ref.shape[0]
    blk = MOBA_BLOCK
    nblk = ksum_ref.shape[0]
    seq_len = k_ref.shape[0]
    n_heads = LANES // HD_MOBA
    spare_off = [((half + 1) % n_heads) * HD_MOBA for half in range(n_heads)]

    @pl.when(t == 0)
    def _():
        key_blk = lax.broadcasted_iota(jnp.int32, (seq_len, LANES), 0) // blk
        key_lane = lax.broadcasted_iota(jnp.int32, (seq_len, LANES), 1)
        k = k_ref[...]
        v = v_ref[...]
        for half in range(n_heads):
            in_head = (key_lane >= half * HD_MOBA) & (key_lane < (half + 1) * HD_MOBA)
            onehot = jnp.where(key_lane - spare_off[half] == key_blk, 1.0, 0.0).astype(BF16)
            ka_sc[half] = jnp.where(in_head, k, onehot)
            va_sc[half] = jnp.where(in_head, v, jnp.ones_like(v))

    q = q_ref[...]
    kmean = ksum_ref[...] * (1.0 / blk)
    km_hi = kmean.astype(BF16)
    km_lo = (kmean - km_hi.astype(F32)).astype(BF16)
    lane = lax.broadcasted_iota(jnp.int32, (1, LANES), 1)
    causal = (lax.broadcasted_iota(jnp.int32, (blk, blk), 1) <= lax.broadcasted_iota(jnp.int32, (blk, blk), 0))
    blk_id = lax.broadcasted_iota(jnp.int32, (nblk, tq), 0)
    place_row = lax.broadcasted_iota(jnp.int32, (nblk, LANES), 0)
    place_lane = lax.broadcasted_iota(jnp.int32, (nblk, LANES), 1)
    first_blk = t * (tq // blk)
    own_blk = first_blk + lax.broadcasted_iota(jnp.int32, (nblk, tq), 1) // blk
    own_parts = range(tq // blk)
    q_rows = [slice(j * blk, (j + 1) * blk) for j in own_parts]
    k_rows = [pl.ds(pl.multiple_of(t * tq + j * blk, blk), blk) for j in own_parts]
    heads = range(n_heads)

    in_head = [(lane >= half * HD_MOBA) & (lane < (half + 1) * HD_MOBA) for half in heads]
    qh = [jnp.where(in_head[half], q, jnp.zeros_like(q)) for half in heads]
    gate_t = [_dot_nt(km_hi, qh[half]) + _dot_nt(km_lo, qh[half]) for half in heads]
    s_own = [[jnp.where(causal, _dot_nt(qh[half][q_rows[j]], ka_sc[half, k_rows[j], :]), MASK_BIAS)
              for j in own_parts] for half in heads]
    picked_t = _select_topk_each(gate_t, blk_id.astype(F32), blk_id < own_blk, axis=0)
    place = [jnp.where(place_lane - spare_off[half] == place_row, 1.0, 0.0).astype(BF16) for half in heads]
    picked = [_dot_tn(jnp.where(picked_t[half], 1.0, 0.0).astype(BF16), place[half]) for half in heads]
    q_aug = []
    for half in heads:
        in_range = (lane >= spare_off[half]) & (lane < spare_off[half] + nblk)
        bias = jnp.where(in_range & (picked[half] < 0.5), MASK_BIAS, 0.0)
        q_aug.append(qh[half] + bias.astype(BF16))

    n_fine = (first_blk + tq // blk - 1 + fine - 1) // fine
    per_group = group // fine
    tail = n_fine % per_group
    tail = jnp.where(tail == per_group - 1, 0, tail) if per_group > 2 else tail
    n_groups = (n_fine - tail + per_group - 1) // per_group

    def unit_rows(u, n_units):
        return pl.ds(pl.multiple_of(u * (fine * blk), fine * blk), n_units * fine * blk)

    def score_units(u, n_units, mxs):
        s = [_dot_nt(q_aug[half], ka_sc[half, unit_rows(u, n_units), :]) for half in heads]
        out = []
        for half in heads:
            mx = mxs[half]
            for j in range(n_units):
                s_sc[half, u + j] = s[half][:, j * fine * blk:(j + 1) * fine * blk]
            for i in range(n_units * fine):
                mx = jnp.maximum(mx, s[half][:, i * blk:(i + 1) * blk])
            out.append(mx)
        return tuple(out)

    def value_units(u, n_units, accs):
        p = [jnp.concatenate([jnp.exp2(s_sc[half, u + j] - m[half]).astype(BF16) for j in range(n_units)], axis=1)
             for half in heads]
        return tuple(accs[half] + _dot(p[half], va_sc[half, unit_rows(u, n_units), :]) for half in heads)

    tail_start = n_groups * per_group
    mx0 = tuple(jnp.concatenate(s_own[half], axis=0) for half in heads)
    mxs = lax.fori_loop(0, n_groups, lambda g, c: score_units(g * per_group, per_group, c), mx0)
    mxs = lax.fori_loop(0, tail, lambda r, c: score_units(tail_start + r, 1, c), mxs)
    m = [jnp.max(mx, axis=-1, keepdims=True) for mx in mxs]

    acc0 = tuple(
        jnp.concatenate([_dot(jnp.exp2(s_own[half][j] - m[half][q_rows[j]]).astype(BF16), va_sc[half, k_rows[j], :])
                         for j in own_parts], axis=0)
        for half in heads)
    accs = lax.fori_loop(0, n_groups, lambda g, c: value_units(g * per_group, per_group, c), acc0)
    accs = lax.fori_loop(0, tail, lambda r, c: value_units(tail_start + r, 1, c), accs)

    o = jnp.zeros((tq, LANES), F32)
    for half in heads:
        row_sum = accs[half][:, spare_off[half]:spare_off[half] + 1]
        o = jnp.where(in_head[half], accs[half] / row_sum, o)
    o_ref[...] = o.astype(o_ref.dtype)


def _moba_prompt(qm, kb, vb, ksum, n_seq, seq_len):
    assert seq_len % MOBA_BLOCK == 0
    nblk = seq_len // MOBA_BLOCK
    assert nblk <= HD_MOBA
    group, fine = (MOBA_GROUP, MOBA_GROUP_FINE) if nblk % MOBA_GROUP == 0 else (1, 1)
    tq = MOBA_TQ if seq_len % MOBA_TQ == 0 else MOBA_BLOCK
    nt = seq_len // tq
    n = n_seq * seq_len
    n_pair = MOBA_W // LANES
    n_heads = LANES // HD_MOBA
    seq_block = pl.BlockSpec((seq_len, LANES), lambda b, hp, t: (b, hp), pipeline_mode=pl.Buffered(1))
    return pl.pallas_call(
        functools.partial(_moba_prompt_kernel, group=group, fine=fine),
        grid=(n_seq, n_pair, nt),
        in_specs=[
            pl.BlockSpec((tq, LANES), lambda b, hp, t: (b * nt + t, hp)),
            seq_block, seq_block,
            pl.BlockSpec((nblk, LANES), lambda b, hp, t: (b, hp)),
        ],
        out_specs=pl.BlockSpec((tq, LANES), lambda b, hp, t: (b * nt + t, hp)),
        out_shape=jax.ShapeDtypeStruct((n, MOBA_W), BF16),
        scratch_shapes=[pltpu.VMEM((n_heads, seq_len, LANES), BF16),
                        pltpu.VMEM((n_heads, seq_len, LANES), BF16),
                        pltpu.VMEM((n_heads, nblk // fine, tq, fine * MOBA_BLOCK), F32)],
        compiler_params=_params(3),
        name="moba_prompt",
    )(qm, kb, vb, ksum)


def _moba_sample_kernel(pt_ref, qbd_ref, kn_ref, vnt_ref, *rest, n_q):
    del pt_ref
    pps = PAGES_PER_STEP
    k_pages, v_pages = rest[:pps], rest[pps:2 * pps]
    o_ref, m_sc, l_sc, gate_sc, acc_sc = rest[2 * pps:]
    s_id = pl.program_id(1)
    n_step = pl.num_programs(1)
    nblk = m_sc.shape[1]
    n_hq = qbd_ref.shape[1]
    qbd = qbd_ref[0]
    col_head = lax.broadcasted_iota(jnp.int32, (HD_MOBA, n_hq), 1) // n_q
    blk_col = lax.broadcasted_iota(jnp.int32, (n_hq, nblk), 1)
    pages_per_blk = MOBA_BLOCK // PAGE_SIZE

    def own_head(pvt):
        out = pvt[0:HD_MOBA]
        for h in range(1, H_MOBA):
            out = jnp.where(col_head == h, pvt[h * HD_MOBA:(h + 1) * HD_MOBA], out)
        return out

    def block_t(page_refs, i):
        pages = range(i * pages_per_blk, (i + 1) * pages_per_blk)
        return jnp.concatenate([page_refs[j][0] for j in pages], axis=1).astype(BF16)

    blocks = range(pps // pages_per_blk)
    b_idx = [s_id * (pps // pages_per_blk) + i for i in blocks]
    s = [_dot(qbd, block_t(k_pages, i)) for i in blocks]
    gate = [jnp.sum(si, axis=-1, keepdims=True) for si in s]
    m = [jnp.max(si, axis=-1, keepdims=True) for si in s]
    p = [jnp.exp(si - mi) for si, mi in zip(s, m)]
    l = [jnp.sum(pi, axis=-1, keepdims=True) for pi in p]
    pv = [own_head(_dot_nt(block_t(v_pages, i), p[i].astype(BF16))) for i in blocks]
    for i in blocks:
        acc_sc[b_idx[i]] = pv[i]
    for ref, vals in ((m_sc, m), (l_sc, l), (gate_sc, gate)):
        cur = ref[...]
        for i in blocks:
            cur = jnp.where(blk_col == b_idx[i], vals[i], cur)
        ref[...] = cur

    @pl.when(s_id == n_step - 1)
    def _():
        picked = _select_topk(gate_sc[...], blk_col.astype(F32), blk_col >= 0, axis=1)
        m_all = jnp.where(picked, m_sc[...], NEG_INF)
        kn = kn_ref[0].astype(BF16)
        vnt = vnt_ref[0].astype(BF16)
        n_new = kn.shape[0]
        key_i = lax.broadcasted_iota(jnp.int32, (n_hq, n_new), 1)
        q_i = lax.broadcasted_iota(jnp.int32, (n_hq, n_new), 0) % n_q
        s_own = jnp.where(key_i <= q_i, _dot_nt(qbd, kn), NEG_INF)
        m_tot = jnp.maximum(jnp.max(s_own, axis=-1, keepdims=True),
                            jnp.max(m_all, axis=-1, keepdims=True))
        p_own = jnp.exp(s_own - m_tot)
        c = jnp.exp(m_all - m_tot)
        l_tot = jnp.sum(p_own, axis=-1, keepdims=True) + jnp.sum(c * l_sc[...], axis=-1, keepdims=True)
        inv_l = 1.0 / l_tot
        c_t = jnp.transpose(c * inv_l)
        out = own_head(_dot_nt(vnt, (p_own * inv_l).astype(BF16)))
        for b in range(nblk):
            out = out + acc_sc[b] * c_t[b:b + 1, :]
        o_ref[0] = out


def _moba_sample(q, k_new, v_new, cache_k, cache_v, page_table):
    n_seq, n_q, _ = q.shape
    n_phys = cache_k.shape[0]
    n_pages = page_table.shape[1]
    past = n_pages * PAGE_SIZE
    assert past % MOBA_BLOCK == 0 and n_pages % PAGES_PER_STEP == 0
    assert n_q <= MOBA_BLOCK and n_q % SUBLANES == 0
    nblk = past // MOBA_BLOCK
    n_hq = H_MOBA * n_q
    lane_head = jnp.arange(MOBA_W) // HD_MOBA
    qbd = jnp.where(lane_head[None, None, None, :] == jnp.arange(H_MOBA)[None, :, None, None],
                    q[:, None, :, :] * (HD_MOBA ** -0.5), 0.0).reshape(n_seq, n_hq, MOBA_W).astype(BF16)
    n_new = max(BF16_ROWS, n_q)
    pad = ((0, 0), (0, n_new - n_q), (0, 0))
    kn = jnp.pad(k_new, pad)
    vnt = jnp.transpose(jnp.pad(v_new, pad), (0, 2, 1))
    to_t = lambda c: jnp.transpose(c, (0, 2, 3, 1)).reshape(n_phys, MOBA_W, PAGE_SIZE)

    def page_spec(i):
        return pl.BlockSpec((1, MOBA_W, PAGE_SIZE), lambda b, s, pt: (pt[b, s * PAGES_PER_STEP + i], 0, 0))

    seq3 = lambda r, c: pl.BlockSpec((1, r, c), lambda b, s, pt: (b, 0, 0))
    grid_spec = pltpu.PrefetchScalarGridSpec(
        num_scalar_prefetch=1,
        grid=(n_seq, n_pages // PAGES_PER_STEP),
        in_specs=[seq3(n_hq, MOBA_W), seq3(n_new, MOBA_W), seq3(MOBA_W, n_new)]
        + [page_spec(i) for i in range(PAGES_PER_STEP)] * 2,
        out_specs=seq3(HD_MOBA, n_hq),
        scratch_shapes=[
            pltpu.VMEM((n_hq, nblk), F32),
            pltpu.VMEM((n_hq, nblk), F32),
            pltpu.VMEM((n_hq, nblk), F32),
            pltpu.VMEM((nblk, HD_MOBA, n_hq), F32),
        ],
    )
    out = pl.pallas_call(
        functools.partial(_moba_sample_kernel, n_q=n_q),
        grid_spec=grid_spec,
        out_shape=jax.ShapeDtypeStruct((n_seq, HD_MOBA, n_hq), F32),
        compiler_params=_params(2),
        name="moba_sample",
    )(page_table, qbd, kn, vnt, *([to_t(cache_k)] * PAGES_PER_STEP), *([to_t(cache_v)] * PAGES_PER_STEP))
    return jnp.transpose(out.reshape(n_seq, HD_MOBA, H_MOBA, n_q), (0, 3, 2, 1)).reshape(n_seq, n_q, MOBA_W)


def _cumsum_rows(x):
    n = x.shape[0]
    row = lax.broadcasted_iota(jnp.int32, x.shape, 0)
    s = 1
    while s < n:
        x = x + jnp.where(row >= s, pltpu.roll(x, s, axis=0), 0.0)
        s *= 2
    return x


def _gla_chunk(q, k, v, la, st, cast):
    c = q.shape[0]
    sub = min(GLA_SUB, c)
    lane = lax.broadcasted_iota(jnp.int32, (1, GLA_KW), 1)
    head_masks = [(lane >= h * DK_GLA) & (lane < (h + 1) * DK_GLA) for h in range(H_GLA)]
    g = _cumsum_rows(la)
    krow = lax.broadcasted_iota(jnp.int32, (c, GLA_KW), 0)
    a_row = lax.broadcasted_iota(jnp.int32, (H_GLA * sub, c), 0)
    a_col = lax.broadcasted_iota(jnp.int32, (H_GLA * sub, c), 1)

    subs = range(c // sub)
    g_ref = [jnp.zeros((1, GLA_KW), F32) if i == 0 else g[i * sub - 1:i * sub] for i in subs]
    qt = [q[i * sub:(i + 1) * sub] * jnp.exp(g[i * sub:(i + 1) * sub] - g_ref[i]) for i in subs]
    kt = [jnp.where(krow < (i + 1) * sub, k * jnp.exp(g_ref[i] - g), 0.0) for i in subs]
    q_stack = [jnp.concatenate([jnp.where(hm, qt[i], 0.0) for hm in head_masks], axis=0) for i in subs]
    a = [_dot_nt(cast(q_stack[i]), cast(kt[i])) for i in subs]
    a_parts = [jnp.where(a_col <= i * sub + a_row % sub, a[i], 0.0) for i in subs]

    heads = range(H_GLA)
    qe = q * jnp.exp(g)
    g_last = g[c - 1:c]
    k_dec = cast(k * jnp.exp(g_last - g))
    st_c = cast(st)
    a_h = [jnp.concatenate([ap[h * sub:(h + 1) * sub] for ap in a_parts], axis=0) for h in heads]
    v_h = [cast(v[:, h * DV_GLA:(h + 1) * DV_GLA]) for h in heads]
    o_inter = [_dot_nt(cast(jnp.where(head_masks[h], qe, 0.0)), st_c) for h in heads]
    o_intra = [_dot(cast(a_h[h]), v_h[h]) for h in heads]
    upd = [_dot_tn(v_h[h], k_dec) for h in heads]
    st_new = st * jnp.exp(g_last)
    for h in heads:
        st_new = st_new + jnp.where(head_masks[h], upd[h], 0.0)
    return [o_intra[h] + o_inter[h] for h in heads], st_new


def _gla_kernel(q_ref, k_ref, v_ref, la_ref, r_ref, s0_ref, nw_ref, o_ref, sfin_ref, st_sc, *, chunk):
    t = pl.program_id(1)

    @pl.when(t == 0)
    def _():
        st_sc[...] = s0_ref[...]

    cast = (lambda a: a.astype(BF16)) if chunk >= BF16_ROWS else (lambda a: a)
    n_group = q_ref.shape[0]
    sts = [st_sc[g] for g in range(n_group)]
    for c in range(q_ref.shape[1] // chunk):
        rows = slice(c * chunk, (c + 1) * chunk)
        for g in range(n_group):
            q = q_ref[g, rows, :].astype(F32) * (DK_GLA ** -0.5)
            k = k_ref[g, rows, :].astype(F32)
            outs, sts[g] = _gla_chunk(q, k, v_ref[g, rows, :], la_ref[g, rows, :], sts[g], cast)
            for h, o in enumerate(outs):
                lanes = slice(h * DV_GLA, (h + 1) * DV_GLA)
                r = r_ref[g, rows, lanes]
                o = o * lax.rsqrt(jnp.mean(o * o, axis=-1, keepdims=True) + EPS)
                o_ref[g, rows, lanes] = (o * nw_ref[:, lanes] * (r * _sigmoid(r))).astype(o_ref.dtype)
    for g in range(n_group):
        st_sc[g] = sts[g]

    @pl.when(t == pl.num_programs(1) - 1)
    def _():
        for g in range(n_group):
            sfin_ref[g] = sts[g]


def _gla(qg, kg, vg, la, rg, state_t, nw, n_seq, seq_len, out_dtype):
    chunk = math.gcd(seq_len, GLA_CHUNK)
    tl = min(seq_len, SEQ_TILE)
    assert seq_len % tl == 0 and tl % chunk == 0
    nt = seq_len // tl
    n = n_seq * seq_len
    n_group = math.gcd(n_seq, max(1, GLA_STEP_CHUNKS * chunk // tl))
    seq3 = lambda a: a.reshape(n_seq, seq_len, a.shape[-1])
    row = lambda w: pl.BlockSpec((n_group, tl, w), lambda b, t: (b, t, 0))
    st_spec = pl.BlockSpec((n_group, DV_GLA, GLA_KW), lambda b, t: (b, 0, 0))
    o, st = pl.pallas_call(
        functools.partial(_gla_kernel, chunk=chunk),
        grid=(n_seq // n_group, nt),
        in_specs=[row(GLA_KW), row(GLA_KW), row(GLA_VW), row(GLA_KW), row(GLA_VW), st_spec,
                  _const_spec((1, GLA_VW))],
        out_specs=[row(GLA_VW), st_spec],
        out_shape=[jax.ShapeDtypeStruct((n_seq, seq_len, GLA_VW), out_dtype),
                   jax.ShapeDtypeStruct((n_seq, DV_GLA, GLA_KW), F32)],
        scratch_shapes=[pltpu.VMEM((n_group, DV_GLA, GLA_KW), F32)],
        compiler_params=_params(2),
        name="gla",
    )(seq3(qg), seq3(kg), seq3(vg), seq3(la), seq3(rg), state_t, nw)
    return o.reshape(n, GLA_VW), st


def _state_to_t(s):
    n = s.shape[0]
    return jnp.transpose(s, (0, 3, 1, 2)).reshape(n, DV_GLA, GLA_KW)


def _state_from_t(st):
    n = st.shape[0]
    return jnp.transpose(st.reshape(n, DV_GLA, H_GLA, DK_GLA), (0, 2, 3, 1))


def _cross_kernel(q_ref, mk_ref, mv_ref, o_ref, *, n_mem, rows_by_head):
    scale = HD_CROSS ** -0.5
    small = q_ref.shape[1] < BF16_ROWS
    cast = (lambda a: a) if small else (lambda a: a.astype(BF16))
    units = [(g, h) for g in range(q_ref.shape[0]) for h in range(H_CROSS)]
    lanes = lambda h: slice(h * HD_CROSS, (h + 1) * HD_CROSS)

    def mem_head(ref, g, h):
        if rows_by_head:
            return ref[g, pl.ds(h, n_mem, stride=H_CROSS), :]
        return ref[g, :, lanes(h)]

    s = [_dot_nt(cast(q_ref[g, :, lanes(h)]), cast(mem_head(mk_ref, g, h))) * scale for g, h in units]
    p = [jnp.exp(si - jnp.max(si, axis=-1, keepdims=True)) for si in s]
    l = [jnp.sum(pi, axis=-1, keepdims=True) for pi in p]
    for (g, h), pi, li in zip(units, p, l):
        o_ref[g, :, lanes(h)] = (_dot(cast(pi), cast(mem_head(mv_ref, g, h))) / li).astype(o_ref.dtype)


def _cross(qc, mk, mv, n_seq, seq_len, n_mem, out_dtype):
    tl = min(seq_len, CROSS_STEP_ROWS)
    assert seq_len % tl == 0
    nt = seq_len // tl
    n_group = math.gcd(n_seq, min(CROSS_MAX_GROUP, max(1, CROSS_STEP_ROWS // tl)))
    row = pl.BlockSpec((n_group, tl, CROSS_W), lambda b, t: (b, t, 0))
    rows_by_head = mk.ndim == 4
    mem_shape = (n_mem * H_CROSS, HD_CROSS) if rows_by_head else (n_mem, CROSS_W)
    mem = pl.BlockSpec((n_group,) + mem_shape, lambda b, t: (b, 0, 0))
    o = pl.pallas_call(
        functools.partial(_cross_kernel, n_mem=n_mem, rows_by_head=rows_by_head),
        grid=(n_seq // n_group, nt),
        in_specs=[row, mem, mem],
        out_specs=row,
        out_shape=jax.ShapeDtypeStruct((n_seq, seq_len, CROSS_W), out_dtype),
        compiler_params=_params(2),
        name="cross",
    )(qc.reshape(n_seq, seq_len, CROSS_W), mk.reshape((n_seq,) + mem_shape), mv.reshape((n_seq,) + mem_shape))
    return o.reshape(n_seq * seq_len, CROSS_W)


def _merge_kernel(x_ref, om_ref, og_ref, oc_ref, nw_ref, wg_ref, bg_ref, wbm_ref, wbg_ref, wbc_ref,
                  wo_ref, h_ref):
    x = x_ref[...]
    d = x.shape[1]
    xb = _rms(x, nw_ref[...]).astype(BF16)
    merged = jnp.zeros(x.shape, F32)
    for i, (o_ref, wb_ref) in enumerate(((om_ref, wbm_ref), (og_ref, wbg_ref), (oc_ref, wbc_ref))):
        cols = slice(i * d, (i + 1) * d)
        gate = _sigmoid(_dot(xb, wg_ref[:, cols]) + bg_ref[:, cols])
        merged = merged + gate * _dot(o_ref[...].astype(BF16), wb_ref[...])
    h_ref[...] = x + _dot(merged.astype(BF16), wo_ref[...])


def _merge(x2d, om, og, oc, nw, wg, bg, wbm, wbg, wbc, wo):
    n, d = x2d.shape
    tm = min(ROW_TILE, n)
    row = lambda w: pl.BlockSpec((tm, w), lambda i: (i, 0))
    return pl.pallas_call(
        _merge_kernel,
        grid=(n // tm,),
        in_specs=[row(d), row(MOBA_W), row(GLA_VW), row(CROSS_W), _const_spec((1, d)),
                  _const_spec(wg.shape), _const_spec(bg.shape), _const_spec(wbm.shape),
                  _const_spec(wbg.shape), _const_spec(wbc.shape), _const_spec(wo.shape)],
        out_specs=row(d),
        out_shape=jax.ShapeDtypeStruct((n, d), F32),
        compiler_params=_params(1),
        name="merge",
    )(x2d, om, og, oc, nw, wg, bg, wbm, wbg, wbc, wo)


def _gelu_tanh(x):
    return 0.5 * x * (1.0 + jnp.tanh(0.7978845608028654 * (x + 0.044715 * (x * x * x))))


def _ffn_body(h_ref, nw_ref, wup_ref, wc_ref, bc_ref, wdn_ref, nf_ref, y_ref, prev_rows, store_u):
    h = h_ref[...]
    hb = _rms(h, nw_ref[...]).astype(BF16)
    d_ff = wdn_ref.shape[0]
    n_col_chunks = FFN_COL_CHUNKS
    fc = d_ff // n_col_chunks
    assert fc % LANES == 0
    acc = jnp.zeros(h.shape, F32)
    for c in range(n_col_chunks):
        cols = slice(c * fc, (c + 1) * fc)
        u = _dot(hb, wup_ref[:, cols])
        gate = _dot(hb, wup_ref[:, d_ff + c * fc:d_ff + (c + 1) * fc])
        u1, u2 = prev_rows(cols, u)
        store_u(cols, u)
        conv = bc_ref[:, cols] + wc_ref[0:1, cols] * u2 + wc_ref[1:2, cols] * u1 + wc_ref[2:3, cols] * u
        act = (_gelu_tanh(conv) * gate).astype(BF16)
        acc = acc + _dot(act, wdn_ref[cols, :])
    y_ref[...] = _rms(h + acc, nf_ref[...])


def _ffn_long_kernel(h_ref, nw_ref, wup_ref, wc_ref, bc_ref, wdn_ref, nf_ref, y_ref, cs_ref, carry_sc):
    tm = h_ref.shape[0]

    @pl.when(pl.program_id(1) == 0)
    def _():
        carry_sc[...] = jnp.zeros(carry_sc.shape, F32)

    row = lax.broadcasted_iota(jnp.int32, (tm, 1), 0)

    def prev_rows(cols, u):
        last = carry_sc[SUBLANES - 1:SUBLANES, cols]
        last2 = carry_sc[SUBLANES - 2:SUBLANES - 1, cols]
        u1 = jnp.where(row == 0, last, pltpu.roll(u, 1, axis=0))
        u2 = jnp.where(row == 0, last2, jnp.where(row == 1, last, pltpu.roll(u, 2, axis=0)))
        return u1, u2

    def store_u(cols, u):
        carry_sc[:, cols] = u[tm - SUBLANES:tm]
        cs_ref[0, :, cols] = u[tm - (CONV_W - 1):tm]

    _ffn_body(h_ref, nw_ref, wup_ref, wc_ref, bc_ref, wdn_ref, nf_ref, y_ref, prev_rows, store_u)


def _ffn_short_kernel(h_ref, p1_ref, p2_ref, nw_ref, wup_ref, wc_ref, bc_ref, wdn_ref, nf_ref,
                      y_ref, u_ref, *, seq_len):
    tm = h_ref.shape[0]
    pos = lax.broadcasted_iota(jnp.int32, (tm, 1), 0) % seq_len

    def prev_rows(cols, u):
        u1 = jnp.where(pos >= 1, pltpu.roll(u, 1, axis=0), p1_ref[:, cols])
        u2 = jnp.where(pos >= 2, pltpu.roll(u, 2, axis=0), p2_ref[:, cols])
        return u1, u2

    def store_u(cols, u):
        u_ref[:, cols] = u

    _ffn_body(h_ref, nw_ref, wup_ref, wc_ref, bc_ref, wdn_ref, nf_ref, y_ref, prev_rows, store_u)


def _ffn_weights_specs(d, wup, wc, bc, wdn):
    return [_const_spec((1, d)), _const_spec(wup.shape), _const_spec(wc.shape), _const_spec(bc.shape),
            _const_spec(wdn.shape), _const_spec((1, d))]


def _ffn_long(h2d, n_seq, seq_len, nw, wup, wc, bc, wdn, nf):
    n, d = h2d.shape
    d_ff = wdn.shape[0]
    tm = min(ROW_TILE, n)
    assert seq_len % tm == 0
    nt = seq_len // tm
    row = pl.BlockSpec((tm, d), lambda b, t: (b * nt + t, 0))
    return pl.pallas_call(
        _ffn_long_kernel,
        grid=(n_seq, nt),
        in_specs=[row] + _ffn_weights_specs(d, wup, wc, bc, wdn),
        out_specs=[row, pl.BlockSpec((1, CONV_W - 1, d_ff), lambda b, t: (b, 0, 0))],
        out_shape=[jax.ShapeDtypeStruct((n, d), F32),
                   jax.ShapeDtypeStruct((n_seq, CONV_W - 1, d_ff), F32)],
        scratch_shapes=[pltpu.VMEM((SUBLANES, d_ff), F32)],
        compiler_params=_params(2),
        name="ffn_long",
    )(h2d, nw, wup, wc, bc, wdn, nf)


def _ffn_short(h2d, conv_prev, n_seq, seq_len, nw, wup, wc, bc, wdn, nf):
    n, d = h2d.shape
    d_ff = wdn.shape[0]
    tm = min(ROW_TILE, n)
    assert tm % seq_len == 0 and n % tm == 0 and seq_len >= CONV_W - 1
    pos = jnp.arange(seq_len)[None, :, None]
    older, newer = conv_prev[:, 0][:, None, :], conv_prev[:, 1][:, None, :]
    p1 = jnp.where(pos == 0, newer, 0.0).reshape(n, d_ff)
    p2 = jnp.where(pos == 0, older, jnp.where(pos == 1, newer, 0.0)).reshape(n, d_ff)
    row = lambda w: pl.BlockSpec((tm, w), lambda i: (i, 0))
    y, u = pl.pallas_call(
        functools.partial(_ffn_short_kernel, seq_len=seq_len),
        grid=(n // tm,),
        in_specs=[row(d), row(d_ff), row(d_ff)] + _ffn_weights_specs(d, wup, wc, bc, wdn),
        out_specs=[row(d), row(d_ff)],
        out_shape=[jax.ShapeDtypeStruct((n, d), F32), jax.ShapeDtypeStruct((n, d_ff), F32)],
        compiler_params=_params(1),
        name="ffn_short",
    )(h2d, p1, p2, nw, wup, wc, bc, wdn, nf)
    return y, u.reshape(n_seq, seq_len, d_ff)[:, seq_len - (CONV_W - 1):]


def _prep_weights(norm_mix, w_in, w_gla_a2, b_gla_a, norm_gla, norm_mem, w_mem_kv, w_br_moba, w_br_gla,
                  w_br_cross, w_gate, b_gate, w_out, norm_ffn, w_up, w_conv, b_conv, w_down, norm_final):
    d = w_in.shape[0]
    o_a = C_QC
    w_in_p = jnp.concatenate(
        [w_in[:, :o_a].astype(BF16), w_in[:, o_a + GLA_RANK:].astype(BF16),
         w_in[:, o_a:o_a + GLA_RANK].astype(BF16), jnp.zeros((d, A_PAD - GLA_RANK), BF16)], axis=1)
    wa2_p = jnp.concatenate([w_gla_a2, jnp.zeros((A_PAD - GLA_RANK, GLA_KW), w_gla_a2.dtype)],
                            axis=0).astype(BF16)
    r2 = lambda a: a.reshape(1, -1)
    return dict(
        norm_mix=r2(norm_mix), w_in=w_in_p, wa2=wa2_p, ba=r2(b_gla_a), norm_gla=r2(norm_gla),
        norm_mem=r2(norm_mem), w_mem_kv=w_mem_kv.astype(BF16), wbm=w_br_moba.astype(BF16),
        wbg=w_br_gla.astype(BF16), wbc=w_br_cross.astype(BF16), wg=w_gate.astype(BF16), bg=r2(b_gate),
        wo=w_out.astype(BF16), norm_ffn=r2(norm_ffn), wup=w_up.astype(BF16), wc=w_conv, bc=r2(b_conv),
        wdn=w_down.astype(BF16), norm_final=r2(norm_final))


def kernel(x_prompt, x_sample, cache_moba_k, cache_moba_v, state_gla, state_conv, cache_mem_k, cache_mem_v, page_table, mem_prompt, norm_mix, w_in, w_gla_a2, b_gla_a, norm_gla, norm_mem, w_mem_kv, w_br_moba, w_br_gla, w_br_cross, w_gate, b_gate, w_out, norm_ffn, w_up, w_conv, b_conv, w_down, norm_final):
    depth = w_in.shape[0]
    assert depth == 1
    bp, sp, d = x_prompt.shape
    bs, ss, _ = x_sample.shape
    n_mem = mem_prompt.shape[1]
    w = _prep_weights(norm_mix[0], w_in[0], w_gla_a2[0], b_gla_a[0], norm_gla[0], norm_mem[0], w_mem_kv[0],
                      w_br_moba[0], w_br_gla[0], w_br_cross[0], w_gate[0], b_gate[0], w_out[0], norm_ffn[0],
                      w_up[0], w_conv[0], b_conv[0], w_down[0], norm_final)

    def mix(x2d, o_m, o_g, o_c):
        return _merge(x2d, o_m, o_g, o_c, w["norm_mix"], w["wg"], w["bg"], w["wbm"], w["wbg"], w["wbc"], w["wo"])

    ffn_w = (w["norm_ffn"], w["wup"], w["wc"], w["bc"], w["wdn"], w["norm_final"])

    xp = x_prompt.reshape(bp * sp, d)
    qm, kt_p, vt_p, qg, kg, vg, rg, qc, la, ksum, kb, vb = _project(
        xp, w["norm_mix"], w["w_in"], w["wa2"], w["ba"], BF16, MOBA_Q_SCALE, kv_seq_len=sp)
    o_m = _moba_prompt(qm, kb, vb, ksum.reshape(-1, MOBA_W), bp, sp)
    mk_p, mv_p = _memory_kv(mem_prompt.reshape(bp * n_mem, d), w["norm_mem"], w["w_mem_kv"])
    o_g, gla_p = _gla(qg, kg, vg, la, rg, jnp.zeros((bp, DV_GLA, GLA_KW), F32), w["norm_gla"], bp, sp, BF16)
    o_c = _cross(qc, mk_p, mv_p, bp, sp, n_mem, BF16)
    h_p = mix(xp, o_m, o_g, o_c)
    y_p, conv_p = _ffn_long(h_p, bp, sp, *ffn_w)

    xs = x_sample.reshape(bs * ss, d)
    qm, k_s, v_s, qg, kg, vg, rg, qc, la, _, _, _ = _project(
        xs, w["norm_mix"], w["w_in"], w["wa2"], w["ba"], F32, 1.0)
    r3 = lambda a: a.reshape(bs, ss, MOBA_W)
    o_m = _moba_sample(r3(qm), r3(k_s), r3(v_s), cache_moba_k[0], cache_moba_v[0],
                       page_table).reshape(bs * ss, MOBA_W)
    o_g, gla_s = _gla(qg, kg, vg, la, rg, _state_to_t(state_gla[0]), w["norm_gla"], bs, ss, F32)
    o_c = _cross(qc, cache_mem_k[0], cache_mem_v[0], bs, ss, n_mem, F32)
    h_s = mix(xs, o_m, o_g, o_c)
    y_s, conv_s = _ffn_short(h_s, state_conv[0], bs, ss, *ffn_w)

    kv5 = lambda a, b, s: a.reshape(1, b, s, H_MOBA, HD_MOBA)
    kv5_t = lambda a: jnp.transpose(a.reshape(bp, H_MOBA, HD_MOBA, sp), (0, 3, 1, 2))[None]
    return (y_p.reshape(bp, sp, d), y_s.reshape(bs, ss, d),
            kv5_t(kt_p), kv5_t(vt_p), kv5(k_s, bs, ss), kv5(v_s, bs, ss),
            _state_from_t(gla_p)[None], _state_from_t(gla_s)[None],
            conv_p[None], conv_s[None],
            mk_p.reshape(1, bp, n_mem, H_CROSS, HD_CROSS), mv_p.reshape(1, bp, n_mem, H_CROSS, HD_CROSS))
```
